```python
import math
import jax, jax.numpy as jnp
from jax import lax
import numpy as np

D_MODEL = 1024
BATCH = 32
SEQ = 256
DEPTH = 4
DEC_BATCH = 8
DEC_SEQ = 1024
PAST_LEN = 512

GRID_W = 64
N_EVEN = (DEPTH + 1) // 2
N_ODD = DEPTH // 2
A_W = D_MODEL // 2
DH_A = 64
H_A = A_W // DH_A
NA_KH = 8
NA_KW = 16
QCB = 16
KCB = 32
NCB = GRID_W // QCB
B_W = D_MODEL // 2
DK = 128
DV = 128
H_B = B_W // DK
CONV_K = 3
CHUNK = 64
MIX_W = A_W + B_W
IN_COLS = 3 * A_W + 4 * B_W + 4 * H_B
POOL_WINDOWS = (2, 4, 8, 16)
POOL_G = D_MODEL // len(POOL_WINDOWS)
D_FF = 7 * D_MODEL // 2
N_EXP = 8
TOP_K = 2
EPS = 1e-6
NEG_INF = -1e30

kernel_name = 'hybrid_diffusion_prefix_step'


def _rmsnorm(x, g):
    xf = x.astype(jnp.float32)
    y = xf * lax.rsqrt(jnp.mean(xf * xf, axis=-1, keepdims=True) + EPS)
    return y.astype(x.dtype) * g


def _l2norm(x):
    xf = x.astype(jnp.float32)
    return (xf * lax.rsqrt(jnp.sum(xf * xf, axis=-1, keepdims=True) + EPS)).astype(x.dtype)


def _adaln(cond, w, b):
    m = jax.nn.silu(cond) @ w + b
    return [t[:, None, :] for t in jnp.split(m, 6, axis=-1)]


def _modulate(x, g, shift, scale):
    return _rmsnorm(x, g) * (1 + scale) + shift


def _dwconv_centred(x, w):
    T = x.shape[1]
    left = (CONV_K - 1) // 2
    xp = jnp.pad(x, ((0, 0), (left, CONV_K - 1 - left), (0, 0)))
    y = xp[:, 0:T] * w[0]
    for i in range(1, CONV_K):
        y = y + xp[:, i:i + T] * w[i]
    return y


def _even_project(h, w_in, conv_w, a_log, dt_bias):
    bsz, T, _ = h.shape
    proj = h @ w_in
    a_qkv, b_qkv, b_z, b_ab = jnp.split(proj, [3 * A_W, 3 * A_W + 3 * B_W, 3 * A_W + 4 * B_W], axis=-1)
    qa, ka, va = [t.reshape(bsz, T, H_A, DH_A) for t in jnp.split(a_qkv, 3, axis=-1)]
    b_qkv = jax.nn.silu(_dwconv_centred(b_qkv, conv_w))
    qb, kb, vb = jnp.split(b_qkv, 3, axis=-1)
    qb = _l2norm(qb.reshape(bsz, T, H_B, DK))
    kb = _l2norm(kb.reshape(bsz, T, H_B, DK))
    vb = vb.reshape(bsz, T, H_B, DV)
    z = b_z.reshape(bsz, T, H_B, DV)
    ab = b_ab.astype(jnp.float32).reshape(bsz, T, 2, 2, H_B)
    g = -jnp.exp(a_log.astype(jnp.float32)) * jax.nn.softplus(ab[:, :, 0] + dt_bias.astype(jnp.float32))
    beta = jax.nn.sigmoid(ab[:, :, 1])
    return qa, ka, va, qb, kb, vb, z, g, beta


def _delta_chunked(q, k, v, g, beta, s0):
    bsz, T, H, _ = q.shape
    dv = v.shape[-1]
    n = T // CHUNK

    def chunks(a):
        a = a.astype(jnp.float32).reshape((bsz, n, CHUNK, H) + a.shape[3:])
        return jnp.moveaxis(a, (1, 3), (0, 2))

    q = chunks(q) * (DK ** -0.5)
    k, v, g, beta = chunks(k), chunks(v), chunks(g), chunks(beta)
    gc = jnp.cumsum(g, axis=-1)
    kb = k * beta[..., None]
    vb = v * beta[..., None]
    lower = jnp.tril(jnp.ones((CHUNK, CHUNK), bool))
    strict = jnp.tril(jnp.ones((CHUNK, CHUNK), bool), -1)
    diff = gc[..., :, None] - gc[..., None, :]
    decay = jnp.where(lower, jnp.exp(jnp.where(lower, diff, 0.0)), 0.0)
    lmat = jnp.where(strict, jnp.einsum('nbhid,nbhjd->nbhij', kb, k) * decay, 0.0)
    eye = jnp.eye(CHUNK, dtype=jnp.float32)
    tinv = lax.linalg.triangular_solve(eye + lmat, jnp.broadcast_to(eye, lmat.shape), left_side=True, lower=True)
    u = tinv @ vb
    w = tinv @ (kb * jnp.exp(gc)[..., None])
    attn = jnp.einsum('nbhid,nbhjd->nbhij', q, k) * decay
    qg = q * jnp.exp(gc)[..., None]
    kd = k * jnp.exp(gc[..., -1:] - gc)[..., None]
    gl = jnp.exp(gc[..., -1])[..., None, None]

    def step(S, xs):
        u_i, w_i, qg_i, a_i, kd_i, gl_i = xs
        v_new = u_i - w_i @ S
        o_i = qg_i @ S + a_i @ v_new
        S = S * gl_i + jnp.swapaxes(kd_i, -1, -2) @ v_new
        return S, o_i

    s_fin, o = lax.scan(step, s0.astype(jnp.float32), (u, w, qg, attn, kd, gl))
    o = jnp.moveaxis(o, (0, 2), (1, 3)).reshape(bsz, T, H, dv)
    return o, s_fin


def _delta_bidir(q, k, v, g, beta, s0):
    o_f, s_f = _delta_chunked(q, k, v, g[:, :, 0], beta[:, :, 0], s0[:, 0])
    flip = lambda a: jnp.flip(a, axis=1)
    o_b, s_b = _delta_chunked(flip(q), flip(k), flip(v), flip(g[:, :, 1]), flip(beta[:, :, 1]), s0[:, 1])
    return (o_f + flip(o_b)).astype(v.dtype), jnp.stack([s_f, s_b], axis=1)


def _even_output(oa, ob, z, o_gain, w_out):
    bsz, T = oa.shape[:2]
    obf = ob.astype(jnp.float32)
    yb = obf * lax.rsqrt(jnp.mean(obf * obf, axis=-1, keepdims=True) + EPS) * o_gain * jax.nn.silu(z.astype(jnp.float32))
    y = jnp.concatenate([oa.reshape(bsz, T, A_W), yb.astype(oa.dtype).reshape(bsz, T, B_W)], axis=-1)
    return y @ w_out


def _ctx_attention(q, k, v):
    s = jnp.einsum('bqhd,bkhd->bhqk', q, k).astype(jnp.float32) * (q.shape[-1] ** -0.5)
    p = jax.nn.softmax(s, axis=-1).astype(v.dtype)
    return jnp.einsum('bhqk,bkhd->bqhd', p, v)


def _na_latent_attention(q, k, v, k_ctx, v_ctx, rpb):
    bsz, N, H, dh = q.shape
    rows = N // GRID_W
    kh = min(NA_KH, rows)
    r = np.arange(rows)
    row_idx = np.clip(r - kh // 2, 0, rows - kh)[:, None] + np.arange(kh)[None, :]
    qcol = np.arange(GRID_W).reshape(NCB, QCB)
    col_lo = np.clip(qcol - NA_KW // 2, 0, GRID_W - NA_KW)
    col_idx = np.minimum(col_lo[:, 0], GRID_W - KCB)[:, None] + np.arange(KCB)[None, :]
    kc = col_idx[:, None, :]
    valid = (kc >= col_lo[..., None]) & (kc < col_lo[..., None] + NA_KW)
    dr = row_idx - r[:, None] + (NA_KH - 1)
    dc = np.clip(kc - qcol[..., None], 1 - NA_KW, NA_KW - 1) + (NA_KW - 1)
    bias = rpb[:, dr[:, None, None, :, None], dc[None, :, :, None, :]].astype(jnp.float32)
    bias = jnp.where(valid[None, None, :, :, None, :], bias, NEG_INF)
    ri = row_idx[:, None, :, None]
    ci = col_idx[None, :, None, :]
    k_loc = k.reshape(bsz, rows, GRID_W, H, dh)[:, ri, ci]
    v_loc = v.reshape(bsz, rows, GRID_W, H, dh)[:, ri, ci]
    qb = q.reshape(bsz, rows, NCB, QCB, H, dh)
    scale = dh ** -0.5
    s_loc = jnp.einsum('brcqhd,brcjmhd->bhrcqjm', qb, k_loc).astype(jnp.float32) * scale + bias[None]
    s_ctx = jnp.einsum('brcqhd,blhd->bhrcql', qb, k_ctx).astype(jnp.float32) * scale
    n_loc = kh * KCB
    logits = jnp.concatenate([s_loc.reshape(s_loc.shape[:5] + (n_loc,)), s_ctx], axis=-1)
    p = jax.nn.softmax(logits, axis=-1).astype(v.dtype)
    p_loc = p[..., :n_loc].reshape(s_loc.shape)
    p_ctx = p[..., n_loc:]
    o = (jnp.einsum('bhrcqjm,brcjmhd->brcqhd', p_loc, v_loc)
         + jnp.einsum('bhrcql,blhd->brcqhd', p_ctx, v_ctx))
    return o.reshape(bsz, N, H, dh)


def _pool_mixer(h, pool_w, pool_scale):
    T = h.shape[1]
    hf = h.astype(jnp.float32)
    csum = jnp.pad(jnp.cumsum(hf, axis=1), ((0, 0), (1, 0), (0, 0)))
    t = np.arange(T)
    outs = []
    for gi, w in enumerate(POOL_WINDOWS):
        lo = np.maximum(t - w // 2, 0).astype(np.int32)
        hi = np.minimum(t + (w - w // 2), T).astype(np.int32)
        sl = slice(gi * POOL_G, (gi + 1) * POOL_G)
        s = csum[:, hi, sl] - csum[:, lo, sl]
        cnt = (hi - lo).astype(np.float32)[None, :, None]
        y = (s / cnt - hf[:, :, sl]).astype(h.dtype)
        outs.append(y @ pool_w[gi])
    return jnp.concatenate(outs, axis=-1) * pool_scale


def _swiglu(h, wg, wu, wd):
    return (jax.nn.silu(h @ wg) * (h @ wu)) @ wd


def _moe_swiglu(h, w_router, w_gate, w_up, w_down):
    logits = (h @ w_router).astype(jnp.float32)
    top_v, top_i = lax.top_k(logits, TOP_K)
    top_w = jax.nn.softmax(top_v, axis=-1)
    gates = jnp.sum(jax.nn.one_hot(top_i, N_EXP, dtype=jnp.float32) * top_w[..., None], axis=-2).astype(h.dtype)
    out = jnp.zeros_like(h)
    for e in range(N_EXP):
        out = out + gates[..., e:e + 1] * _swiglu(h, w_gate[e], w_up[e], w_down[e])
    return out


def _channel_mixer(h, l, ffn_gate, ffn_up, ffn_down, w_router, moe_gate, moe_up, moe_down):
    i = l // 2
    if l % 2 == 0:
        return _swiglu(h, ffn_gate[i], ffn_up[i], ffn_down[i])
    return _moe_swiglu(h, w_router[i], moe_gate[i], moe_up[i], moe_down[i])


def setup_inputs(seed: int = 0) -> dict:
    key = jax.random.key(seed)
    keys = iter(jax.random.split(key, 40))

    def nrm(shape, scale):
        return jax.random.normal(next(keys), shape, jnp.float32) * scale

    d = D_MODEL
    dt = jnp.exp(jax.random.uniform(next(keys), (N_EVEN, 2, H_B), jnp.float32, math.log(1e-3), math.log(1e-1)))
    a_log = jnp.log(jax.random.uniform(next(keys), (N_EVEN, 2, H_B), jnp.float32, 1.0, 16.0))
    return {
        'x_prompt': nrm((BATCH, SEQ, d), 1.0),
        'x_sample': nrm((DEC_BATCH, DEC_SEQ, d), 1.0),
        'cache_k_ctx': nrm((DEC_BATCH, N_EVEN, PAST_LEN, H_A, DH_A), 1.0),
        'cache_v_ctx': nrm((DEC_BATCH, N_EVEN, PAST_LEN, H_A, DH_A), 1.0),
        'state_delta': nrm((DEC_BATCH, N_EVEN, 2, H_B, DK, DV), 0.1),
        'c': nrm((DEC_BATCH, d), 1.0),
        'c_ctx': nrm((d,), 1.0),
        'w_mod': nrm((DEPTH, d, 6 * d), 0.5 * d ** -0.5),
        'b_mod': nrm((DEPTH, 6 * d), 0.02),
        'norm_g': 1.0 + nrm((DEPTH, 2, d), 0.1),
        'final_g': 1.0 + nrm((d,), 0.1),
        'w_in': nrm((N_EVEN, d, IN_COLS), d ** -0.5),
        'conv_w': nrm((N_EVEN, CONV_K, 3 * B_W), CONV_K ** -0.5),
        'a_log': a_log,
        'dt_bias': dt + jnp.log(-jnp.expm1(-dt)),
        'rpb': nrm((N_EVEN, H_A, 2 * NA_KH - 1, 2 * NA_KW - 1), 0.1),
        'o_gain': 1.0 + nrm((N_EVEN, DV), 0.1),
        'w_out': nrm((N_EVEN, MIX_W, d), MIX_W ** -0.5),
        'ffn_gate': nrm((N_EVEN, d, D_FF), d ** -0.5),
        'ffn_up': nrm((N_EVEN, d, D_FF), d ** -0.5),
        'ffn_down': nrm((N_EVEN, D_FF, d), D_FF ** -0.5),
        'pool_w': nrm((N_ODD, len(POOL_WINDOWS), POOL_G, POOL_G), POOL_G ** -0.5),
        'pool_scale': 1.0 + nrm((N_ODD, d), 0.1),
        'w_router': nrm((N_ODD, d, N_EXP), d ** -0.5),
        'moe_gate': nrm((N_ODD, N_EXP, d, D_FF), d ** -0.5),
        'moe_up': nrm((N_ODD, N_EXP, d, D_FF), d ** -0.5),
        'moe_down': nrm((N_ODD, N_EXP, D_FF, d), D_FF ** -0.5),
    }


def reference(x_prompt, x_sample, cache_k_ctx, cache_v_ctx, state_delta, c, c_ctx,
              w_mod, b_mod, norm_g, final_g, w_in, conv_w, a_log, dt_bias, rpb, o_gain, w_out,
              ffn_gate, ffn_up, ffn_down, pool_w, pool_scale, w_router, moe_gate, moe_up, moe_down):
    x = x_prompt
    k_list, v_list, s_list = [], [], []
    for l in range(DEPTH):
        sh1, sc1, gt1, sh2, sc2, gt2 = _adaln(c_ctx[None, :], w_mod[l], b_mod[l])
        h = _modulate(x, norm_g[l, 0], sh1, sc1)
        if l % 2 == 0:
            i = l // 2
            qa, ka, va, qb, kb, vb, z, g, beta = _even_project(h, w_in[i], conv_w[i], a_log[i], dt_bias[i])
            oa = _ctx_attention(qa, ka, va)
            s0 = jnp.zeros((x.shape[0], 2, H_B, DK, DV), jnp.float32)
            ob, s_fin = _delta_bidir(qb, kb, vb, g, beta, s0)
            mix = _even_output(oa, ob, z, o_gain[i], w_out[i])
            k_list.append(ka)
            v_list.append(va)
            s_list.append(s_fin.astype(x.dtype))
        else:
            mix = _pool_mixer(h, pool_w[l // 2], pool_scale[l // 2])
        x = x + gt1 * mix
        h = _modulate(x, norm_g[l, 1], sh2, sc2)
        x = x + gt2 * _channel_mixer(h, l, ffn_gate, ffn_up, ffn_down, w_router, moe_gate, moe_up, moe_down)
    y_prompt = _rmsnorm(x, final_g)
    new_k_ctx = jnp.stack(k_list, axis=1)
    new_v_ctx = jnp.stack(v_list, axis=1)
    new_state_delta = jnp.stack(s_list, axis=1)

    x = x_sample
    for l in range(DEPTH):
        sh1, sc1, gt1, sh2, sc2, gt2 = _adaln(c, w_mod[l], b_mod[l])
        h = _modulate(x, norm_g[l, 0], sh1, sc1)
        if l % 2 == 0:
            i = l // 2
            qa, ka, va, qb, kb, vb, z, g, beta = _even_project(h, w_in[i], conv_w[i], a_log[i], dt_bias[i])
            oa = _na_latent_attention(qa, ka, va, cache_k_ctx[:, i], cache_v_ctx[:, i], rpb[i])
            ob, _ = _delta_bidir(qb, kb, vb, g, beta, state_delta[:, i])
            mix = _even_output(oa, ob, z, o_gain[i], w_out[i])
        else:
            mix = _pool_mixer(h, pool_w[l // 2], pool_scale[l // 2])
        x = x + gt1 * mix
        h = _modulate(x, norm_g[l, 1], sh2, sc2)
        x = x + gt2 * _channel_mixer(h, l, ffn_gate, ffn_up, ffn_down, w_router, moe_gate, moe_up, moe_down)
    y_sample = _rmsnorm(x, final_g)
    return (y_prompt, y_sample, new_k_ctx, new_v_ctx, new_state_delta)
```

```python
import functools
import math

import numpy as np
import jax
import jax.numpy as jnp
from jax import lax
from jax.experimental import pallas as pl
from jax.experimental.pallas import tpu as pltpu

F32 = jnp.float32
BF16 = jnp.bfloat16

D_MODEL = 1024
GRID_W = 64
DH_A = 64
H_A = 8
A_W = H_A * DH_A
NA_KH = 8
NA_KW = 16
DK = 128
DV = 128
H_B = 4
B_W = H_B * DK
CONV_K = 3
CHUNK = 64
POOL_WINDOWS = (2, 4, 8, 16)
POOL_G = D_MODEL // len(POOL_WINDOWS)
D_FF = 7 * D_MODEL // 2
N_EXP = 8
EPS = 1e-6
NEG_INF = -1e30

LANES = 128
SEG = 256
FF_TILE = 512
MOE_SUB = 256
MOE_GROUP = 8
VMEM_LIMIT = 56 * 2 ** 20


def _params(sem, vmem=VMEM_LIMIT):
    return pltpu.CompilerParams(dimension_semantics=sem, vmem_limit_bytes=vmem)


def _silu(x):
    return x * jax.nn.sigmoid(x)


def _dot(a, b):
    return jnp.dot(a, b, preferred_element_type=F32)


def _dot_nt(a, b):
    return lax.dot_general(a, b, (((1,), (1,)), ((), ())), preferred_element_type=F32)


def _dot_f32x3(a01, x):
    x1 = x.astype(BF16)
    r1 = x - x1.astype(F32)
    x2 = r1.astype(BF16)
    x3 = (r1 - x2.astype(F32)).astype(BF16)
    return _dot(a01, x1) + _dot(a01, x2) + _dot(a01, x3)


def _mask01(mask):
    return jnp.where(mask, 1.0, 0.0).astype(BF16)


def _modulate(x, g, shift, scale):
    y = x * lax.rsqrt(jnp.mean(x * x, axis=-1, keepdims=True) + EPS)
    return (y * g) * (1.0 + scale) + shift


def _pick_lane(a, lane_idx, c):
    return jnp.sum(jnp.where(lane_idx == c, a, 0.0), axis=1, keepdims=True)


def _pick_row(a, row_idx, c):
    return jnp.sum(jnp.where(row_idx == c, a, 0.0), axis=0, keepdims=True)


def _adaln_kernel(cond_ref, w_ref, b_ref, o_ref):
    s = _silu(cond_ref[...]).astype(BF16)
    o_ref[0] = _dot(s, w_ref[0].astype(BF16)) + b_ref[0]


def _adaln(cond, w_mod, b_mod):
    depth, d, six_d = w_mod.shape
    r = cond.shape[0]
    tn = six_d // 4
    return pl.pallas_call(
        _adaln_kernel,
        out_shape=jax.ShapeDtypeStruct((depth, r, six_d), F32),
        grid=(depth, six_d // tn),
        in_specs=[
            pl.BlockSpec((r, d), lambda l, j: (0, 0)),
            pl.BlockSpec((1, d, tn), lambda l, j: (l, 0, j)),
            pl.BlockSpec((1, 1, tn), lambda l, j: (l, 0, j)),
        ],
        out_specs=pl.BlockSpec((1, r, tn), lambda l, j: (l, 0, j)),
        compiler_params=_params(("arbitrary", "arbitrary")),
        name="adaln",
    )(cond, w_mod, b_mod.reshape(depth, 1, six_d))


class _Layout:
    def __init__(self, batch, seq, dec_batch, dec_seq):
        self.batch, self.seq, self.dec_batch, self.dec_seq = batch, seq, dec_batch, dec_seq
        self.nc = batch * seq
        self.nl = dec_batch * dec_seq
        self.n = self.nc + self.nl
        self.mod_rows = -(-(1 + dec_batch) // 8) * 8
        assert seq % SEG == 0 and dec_seq % SEG == 0 and dec_seq % GRID_W == 0
        self.tm = math.gcd(self.nc, 1024)
        assert self.tm % SEG == 0 and dec_seq % self.tm == 0 and self.nc % dec_seq == 0

    def mod_row(self, first_row):
        return jnp.maximum((first_row - self.nc) // self.dec_seq + 1, 0)

    def mod_spec(self, l, k, rows_per_step):
        base = (l * 6 + k) * self.mod_rows
        return pl.BlockSpec((1, 1, D_MODEL),
                            lambda i, *_: (base + self.mod_row(i * rows_per_step), 0, 0))


def _row_spec(l, k):
    return pl.BlockSpec((1, 1, D_MODEL), lambda i, *_: (l * 2 + k, 0, 0))


def _proj_kernel(x_ref, g_ref, sh_ref, sc_ref, w_ref, wab_ref, proj_ref, ab_ref, h_scr):
    @pl.when(pl.program_id(1) == 0)
    def _():
        h = _modulate(x_ref[...], g_ref[0], sh_ref[0], sc_ref[0]).astype(BF16)
        h_scr[...] = h
        ab_ref[...] = _dot(h, wab_ref[...].astype(BF16))

    proj_ref[...] = _dot(h_scr[...], w_ref[...].astype(BF16))


def _even_project(lay, x, normg, modtab, w_in, w_ab, l):
    i_even = l // 2
    tm, tn = lay.tm, 512
    ncols = 3 * A_W + 4 * B_W
    return pl.pallas_call(
        _proj_kernel,
        out_shape=(jax.ShapeDtypeStruct((lay.n, ncols), F32),
                   jax.ShapeDtypeStruct((lay.n, LANES), F32)),
        grid=(lay.n // tm, ncols // tn),
        in_specs=[
            pl.BlockSpec((tm, D_MODEL), lambda i, j: (i, 0)),
            _row_spec(l, 0),
            lay.mod_spec(l, 0, tm),
            lay.mod_spec(l, 1, tm),
            pl.BlockSpec((None, D_MODEL, tn), lambda i, j: (i_even, 0, j)),
            pl.BlockSpec((D_MODEL, LANES), lambda i, j: (0, 0)),
        ],
        out_specs=(pl.BlockSpec((tm, tn), lambda i, j: (i, j)),
                   pl.BlockSpec((tm, LANES), lambda i, j: (i, 0))),
        scratch_shapes=[pltpu.VMEM((tm, D_MODEL), BF16)],
        compiler_params=_params(("arbitrary", "arbitrary")),
        name="even_project",
    )(x, normg, modtab, modtab, w_in, w_ab)


def _softmax_av(scores, values):
    m = functools.reduce(jnp.maximum, [jnp.max(s, axis=-1, keepdims=True) for s in scores])
    ps = [jnp.exp(s - m) for s in scores]
    denom = functools.reduce(jnp.add, [jnp.sum(p, axis=-1, keepdims=True) for p in ps])
    acc = functools.reduce(jnp.add, [_dot(p.astype(BF16), v) for p, v in zip(ps, values)])
    return acc / denom


def _ctx_attn_kernel(q_ref, k_ref, v_ref, o_ref):
    q, k, v = q_ref[...], k_ref[...], v_ref[...]
    scale = DH_A ** -0.5
    outs = []
    for h in range(H_A):
        sl = slice(h * DH_A, (h + 1) * DH_A)
        s = _dot_nt(q[:, sl].astype(BF16), k[:, sl].astype(BF16)) * scale
        outs.append(_softmax_av([s], [v[:, sl].astype(BF16)]))
    o_ref[...] = jnp.concatenate(outs, axis=1)


def _na_attn_kernel(q_ref, k_ref, v_ref, kc_ref, vc_ref, bias_ref, o_ref, *, rows, kh):
    r = pl.program_id(1)
    row_lo = jnp.clip(r - kh // 2, 0, rows - kh)
    start = pl.multiple_of(row_lo * GRID_W, GRID_W)
    q = q_ref[...]
    kl = k_ref[pl.ds(start, kh * GRID_W), :]
    vl = v_ref[pl.ds(start, kh * GRID_W), :]
    kc, vc = kc_ref[...], vc_ref[...]
    scale = DH_A ** -0.5
    outs = []
    for h in range(H_A):
        sl = slice(h * DH_A, (h + 1) * DH_A)
        qh = q[:, sl].astype(BF16)
        s_loc = _dot_nt(qh, kl[:, sl].astype(BF16)) * scale + bias_ref[h, 0]
        s_ctx = _dot_nt(qh, kc[:, sl].astype(BF16)) * scale
        outs.append(_softmax_av([s_loc, s_ctx], [vl[:, sl].astype(BF16), vc[:, sl].astype(BF16)]))
    o_ref[...] = jnp.concatenate(outs, axis=1)


def _na_bias_table(rpb, rows, kh):
    r = np.arange(rows)
    row_idx = np.clip(r - kh // 2, 0, rows - kh)[:, None] + np.arange(kh)[None, :]
    dr = row_idx - r[:, None] + (NA_KH - 1)
    qcol = np.arange(GRID_W)
    kcol = np.arange(GRID_W)
    col_lo = np.clip(qcol - NA_KW // 2, 0, GRID_W - NA_KW)
    valid = (kcol[None, :] >= col_lo[:, None]) & (kcol[None, :] < col_lo[:, None] + NA_KW)
    dc = np.clip(kcol[None, :] - qcol[:, None], 1 - NA_KW, NA_KW - 1) + (NA_KW - 1)
    bias = rpb[:, dr[:, None, :, None], dc[None, :, None, :]].astype(F32)
    bias = jnp.where(valid[None, None, :, None, :], bias, NEG_INF)
    return bias.reshape(H_A, rows, GRID_W, kh * GRID_W)


def _attention(lay, proj, cache_k, cache_v, bias, i_even):
    ctx = pl.pallas_call(
        _ctx_attn_kernel,
        out_shape=jax.ShapeDtypeStruct((lay.nc, A_W), F32),
        grid=(lay.batch,),
        in_specs=[pl.BlockSpec((lay.seq, A_W), lambda b: (b, 0)),
                  pl.BlockSpec((lay.seq, A_W), lambda b: (b, 1)),
                  pl.BlockSpec((lay.seq, A_W), lambda b: (b, 2))],
        out_specs=pl.BlockSpec((lay.seq, A_W), lambda b: (b, 0)),
        compiler_params=_params(("arbitrary",)),
        name="ctx_attention",
    )(proj, proj, proj)

    rows = lay.dec_seq // GRID_W
    kh = min(NA_KH, rows)
    past = cache_k.shape[2]
    q0 = lay.nc // GRID_W
    b0 = lay.nc // lay.dec_seq
    lat = pl.pallas_call(
        functools.partial(_na_attn_kernel, rows=rows, kh=kh),
        out_shape=jax.ShapeDtypeStruct((lay.nl, A_W), F32),
        grid=(lay.dec_batch, rows),
        in_specs=[pl.BlockSpec((GRID_W, A_W), lambda b, r: (q0 + b * rows + r, 0)),
                  pl.BlockSpec((lay.dec_seq, A_W), lambda b, r: (b0 + b, 1)),
                  pl.BlockSpec((lay.dec_seq, A_W), lambda b, r: (b0 + b, 2)),
                  pl.BlockSpec((None, None, past, A_W), lambda b, r: (b, i_even, 0, 0)),
                  pl.BlockSpec((None, None, past, A_W), lambda b, r: (b, i_even, 0, 0)),
                  pl.BlockSpec((H_A, 1, GRID_W, kh * GRID_W), lambda b, r: (0, r, 0, 0))],
        out_specs=pl.BlockSpec((GRID_W, A_W), lambda b, r: (b * rows + r, 0)),
        compiler_params=_params(("arbitrary", "arbitrary")),
        name="na_attention",
    )(proj, proj, proj, cache_k, cache_v, bias)
    return jnp.concatenate([ctx, lat], axis=0)


def _delta_prep_kernel(xq_ref, xk_ref, xv_ref, wq_ref, wk_ref, wv_ref, q_ref, k_ref, v_ref):
    t = xq_ref.shape[0]
    row = lax.broadcasted_iota(jnp.int32, (t, 1), 0)

    def conv_silu(x, w):
        prev = jnp.where(row == 0, 0.0, pltpu.roll(x, 1, 0))
        nxt = jnp.where(row == t - 1, 0.0, pltpu.roll(x, t - 1, 0))
        return _silu(prev * w[0:1] + x * w[1:2] + nxt * w[2:3])

    def l2norm(x):
        return x * lax.rsqrt(jnp.sum(x * x, axis=-1, keepdims=True) + EPS)

    q_ref[...] = l2norm(conv_silu(xq_ref[...], wq_ref[...])) * (DK ** -0.5)
    k_ref[...] = l2norm(conv_silu(xk_ref[...], wk_ref[...]))
    v_ref[...] = conv_silu(xv_ref[...], wv_ref[...])


def _delta_prep(proj, conv_w, i_even, first_seq, n_seq, t):
    c0 = 3 * A_W // LANES
    xspec = lambda part: pl.BlockSpec((t, LANES), lambda b, h: (first_seq + b, c0 + part * H_B + h))
    wspec = lambda part: pl.BlockSpec((None, CONV_K, LANES), lambda b, h: (i_even, 0, part * H_B + h))
    ospec = pl.BlockSpec((t, LANES), lambda b, h: (b, h))
    shape = jax.ShapeDtypeStruct((n_seq * t, B_W), F32)
    return pl.pallas_call(
        _delta_prep_kernel,
        out_shape=(shape, shape, shape),
        grid=(n_seq, H_B),
        in_specs=[xspec(0), xspec(1), xspec(2), wspec(0), wspec(1), wspec(2)],
        out_specs=(ospec, ospec, ospec),
        compiler_params=_params(("arbitrary", "arbitrary")),
        name="delta_prep",
    )(proj, proj, proj, conv_w, conv_w, conv_w)


def _delta_direction(d, h, q, k, v, ab, alog, dtb, s_in):
    n = SEG
    lane = lax.broadcasted_iota(jnp.int32, (1, LANES), 1)
    sub = lax.broadcasted_iota(jnp.int32, (LANES, 1), 0)
    ri = lax.broadcasted_iota(jnp.int32, (n, n), 0)
    ci = lax.broadcasted_iota(jnp.int32, (n, n), 1)
    shift = int(math.log2(CHUNK))
    same = jnp.right_shift(ri, shift) == jnp.right_shift(ci, shift)
    if d == 0:
        incl, strict = same & (ci <= ri), same & (ci < ri)
    else:
        incl, strict = same & (ci >= ri), same & (ci > ri)

    col = d * H_B + h
    g_all = -jnp.exp(alog) * jax.nn.softplus(ab + dtb)
    beta = _pick_lane(jax.nn.sigmoid(ab), lane, 2 * H_B + col)
    gc_all = _dot_f32x3(_mask01(incl), g_all)
    gt_all = _dot_f32x3(_mask01(same), g_all)
    gc = _pick_lane(gc_all, lane, col)
    gt = _pick_lane(gt_all, lane, col)
    gc_row = _pick_row(gc_all.T, sub, col)
    gt_row = _pick_row(gt_all.T, sub, col)

    decay = jnp.where(incl, jnp.exp(jnp.where(incl, gc - gc_row, 0.0)), 0.0)
    kb = k * beta
    k16 = k.astype(BF16)
    lmat = jnp.where(strict, _dot_nt(kb.astype(BF16), k16) * decay, 0.0)

    x = jnp.where(ri == ci, 1.0, 0.0)
    for level in range(shift):
        bi, bj = jnp.right_shift(ri, level), jnp.right_shift(ci, level)
        siblings = jnp.right_shift(bi, 1) == jnp.right_shift(bj, 1)
        off = siblings & ((bi > bj) if d == 0 else (bi < bj))
        x16 = x.astype(BF16)
        c_blocks = jnp.where(off, lmat, 0.0).astype(BF16)
        x = x - _dot(_dot(x16, c_blocks).astype(BF16), x16)

    eg = jnp.exp(gc)
    rhs = jnp.concatenate([v * beta, kb * eg], axis=1).astype(BF16)
    uw = _dot(x.astype(BF16), rhs)
    u, w = uw[:, :DV], uw[:, DV:]
    attn = jnp.where(incl, _dot_nt(q.astype(BF16), k16) * decay, 0.0)
    qg = (q * eg).astype(BF16)
    kd_t = (k.T * jnp.exp(gt_row - gc_row)).astype(BF16)
    gl = jnp.exp(gt)

    n_chunks = n // CHUNK
    order = range(n_chunks) if d == 0 else range(n_chunks - 1, -1, -1)
    s = s_in
    v_new = [None] * n_chunks
    o_state = [None] * n_chunks
    for c in order:
        rs = slice(c * CHUNK, (c + 1) * CHUNK)
        s16 = s.astype(BF16)
        vn = u[rs] - _dot(w[rs].astype(BF16), s16)
        o_state[c] = _dot(qg[rs], s16)
        v_new[c] = vn
        pieces = [jnp.zeros((c * CHUNK, DV), BF16), vn.astype(BF16), jnp.zeros((n - (c + 1) * CHUNK, DV), BF16)]
        padded = jnp.concatenate([p for p in pieces if p.shape[0]], axis=0)
        s = s * gl[c * CHUNK:c * CHUNK + 1, :] + _dot(kd_t, padded)
    o = jnp.concatenate(o_state, axis=0) + _dot(attn.astype(BF16), jnp.concatenate(v_new, axis=0).astype(BF16))
    return o, s


def _delta_kernel(*refs, has_s0):
    (qf_ref, kf_ref, vf_ref, abf_ref, qb_ref, kb_ref, vb_ref, abb_ref, alog_ref, dtb_ref) = refs[:10]
    refs = refs[10:]
    if has_s0:
        s0_ref, refs = refs[0], refs[1:]
    of_ref, ob_ref, sfin_ref, s_scr = refs
    h = pl.program_id(1)
    seg = pl.program_id(2)

    @pl.when(seg == 0)
    def _():
        if has_s0:
            s_scr[...] = s0_ref[...]
        else:
            s_scr[...] = jnp.zeros_like(s_scr)

    alog, dtb = alog_ref[...], dtb_ref[...]
    o_f, s_f = _delta_direction(0, h, qf_ref[...], kf_ref[...], vf_ref[...], abf_ref[...], alog, dtb, s_scr[0])
    o_b, s_b = _delta_direction(1, h, qb_ref[...], kb_ref[...], vb_ref[...], abb_ref[...], alog, dtb, s_scr[1])
    of_ref[...] = o_f
    ob_ref[...] = o_b
    s_scr[0] = s_f
    s_scr[1] = s_b

    @pl.when(seg == pl.num_programs(2) - 1)
    def _():
        sfin_ref[...] = s_scr[...]


def _delta_net(q, k, v, ab, ab_first_seg, alog, dtb, s0, n_seq, t):
    nseg = t // SEG
    fwd = lambda b, h, s: (b * nseg + s, h)
    bwd = lambda b, h, s: (b * nseg + nseg - 1 - s, h)
    blk = lambda imap: pl.BlockSpec((SEG, LANES), imap)
    ab_f = pl.BlockSpec((SEG, LANES), lambda b, h, s: (ab_first_seg + b * nseg + s, 0))
    ab_b = pl.BlockSpec((SEG, LANES), lambda b, h, s: (ab_first_seg + b * nseg + nseg - 1 - s, 0))
    row = pl.BlockSpec((1, LANES), lambda b, h, s: (0, 0))
    state = pl.BlockSpec((None, 2, None, DK, DV), lambda b, h, s: (b, 0, h, 0, 0))
    in_specs = [blk(fwd), blk(fwd), blk(fwd), ab_f, blk(bwd), blk(bwd), blk(bwd), ab_b, row, row]
    args = [q, k, v, ab, q, k, v, ab, alog, dtb]
    if s0 is not None:
        in_specs.append(state)
        args.append(s0)
    oshape = jax.ShapeDtypeStruct((n_seq * t, B_W), F32)
    return pl.pallas_call(
        functools.partial(_delta_kernel, has_s0=s0 is not None),
        out_shape=(oshape, oshape, jax.ShapeDtypeStruct((n_seq, 2, H_B, DK, DV), F32)),
        grid=(n_seq, H_B, nseg),
        in_specs=in_specs,
        out_specs=(blk(fwd), blk(bwd), state),
        scratch_shapes=[pltpu.VMEM((2, DK, DV), F32)],
        compiler_params=_params(("arbitrary", "arbitrary", "arbitrary")),
        name="delta_net",
    )(*args)


def _even_out_kernel(oa_ref, of_ref, ob_ref, z_ref, gain_ref, w_ref, x_ref, gt_ref,
                     g2_ref, sh2_ref, sc2_ref, xo_ref, h2_ref):
    ob = of_ref[...] + ob_ref[...]
    z = z_ref[...]
    parts = [oa_ref[...].astype(BF16)]
    for h in range(H_B):
        sl = slice(h * DV, (h + 1) * DV)
        o_h = ob[:, sl]
        y = o_h * lax.rsqrt(jnp.mean(o_h * o_h, axis=-1, keepdims=True) + EPS) * gain_ref[...] * _silu(z[:, sl])
        parts.append(y.astype(BF16))
    mix = _dot(jnp.concatenate(parts, axis=1), w_ref[...].astype(BF16))
    xn = x_ref[...] + gt_ref[0] * mix
    xo_ref[...] = xn
    h2_ref[...] = _modulate(xn, g2_ref[0], sh2_ref[0], sc2_ref[0]).astype(BF16)


def _even_output(lay, oa, o_f, o_b, proj, o_gain, w_out, x, normg, modtab, l):
    i_even = l // 2
    tm = lay.tm
    zcol = (3 * A_W + 3 * B_W) // B_W
    half = lambda: pl.BlockSpec((tm, B_W), lambda i: (i, 0))
    full = lambda: pl.BlockSpec((tm, D_MODEL), lambda i: (i, 0))
    return pl.pallas_call(
        _even_out_kernel,
        out_shape=(jax.ShapeDtypeStruct((lay.n, D_MODEL), F32),
                   jax.ShapeDtypeStruct((lay.n, D_MODEL), BF16)),
        grid=(lay.n // tm,),
        in_specs=[half(), half(), half(),
                  pl.BlockSpec((tm, B_W), lambda i: (i, zcol)),
                  pl.BlockSpec((None, 1, DV), lambda i: (i_even, 0, 0)),
                  pl.BlockSpec((None, D_MODEL, D_MODEL), lambda i: (i_even, 0, 0)),
                  full(),
                  lay.mod_spec(l, 2, tm),
                  _row_spec(l, 1), lay.mod_spec(l, 3, tm), lay.mod_spec(l, 4, tm)],
        out_specs=(full(), full()),
        compiler_params=_params(("arbitrary",)),
        name="even_output",
    )(oa, o_f, o_b, proj, o_gain, w_out, x, modtab, normg, modtab, modtab)


def _ffn_kernel(h_ref, wg_ref, wu_ref, wd_ref, x_ref, gt_ref, o_ref, acc_ref):
    j = pl.program_id(1)

    @pl.when(j == 0)
    def _():
        acc_ref[...] = jnp.zeros_like(acc_ref)

    h = h_ref[...]
    a = (_silu(_dot(h, wg_ref[...].astype(BF16))) * _dot(h, wu_ref[...].astype(BF16))).astype(BF16)
    acc_ref[...] += _dot(a, wd_ref[...].astype(BF16))

    @pl.when(j == pl.num_programs(1) - 1)
    def _():
        o_ref[...] = x_ref[...] + gt_ref[0] * acc_ref[...]


def _dense_ffn(lay, h2, x, ffn_gate, ffn_up, ffn_down, modtab, l):
    i_even = l // 2
    tm, tf = lay.tm, FF_TILE
    return pl.pallas_call(
        _ffn_kernel,
        out_shape=jax.ShapeDtypeStruct((lay.n, D_MODEL), F32),
        grid=(lay.n // tm, D_FF // tf),
        in_specs=[pl.BlockSpec((tm, D_MODEL), lambda i, j: (i, 0)),
                  pl.BlockSpec((None, D_MODEL, tf), lambda i, j: (i_even, 0, j)),
                  pl.BlockSpec((None, D_MODEL, tf), lambda i, j: (i_even, 0, j)),
                  pl.BlockSpec((None, tf, D_MODEL), lambda i, j: (i_even, j, 0)),
                  pl.BlockSpec((tm, D_MODEL), lambda i, j: (i, 0)),
                  lay.mod_spec(l, 5, tm)],
        out_specs=pl.BlockSpec((tm, D_MODEL), lambda i, j: (i, 0)),
        scratch_shapes=[pltpu.VMEM((tm, D_MODEL), F32)],
        compiler_params=_params(("arbitrary", "arbitrary")),
        name="dense_ffn",
    )(h2, ffn_gate, ffn_up, ffn_down, x, modtab)


def _pool_kernel(xp_ref, xc_ref, xn_ref, g1_ref, sh1_ref, sc1_ref, gt1_ref, pw_ref, ps_ref,
                 g2_ref, sh2_ref, sc2_ref, wr_ref, xo_ref, h2_ref, ridx_ref, rw_ref, *, nc, seq, dec_seq):
    i = pl.program_id(0)
    first = i * SEG
    is_lat = first >= nc
    t_len = jnp.where(is_lat, dec_seq, seq)
    off = jnp.where(is_lat, (first - nc) % dec_seq, first % seq)

    g1, sh1, sc1 = g1_ref[0], sh1_ref[0], sc1_ref[0]
    x = xc_ref[...]
    h_cur = _modulate(x, g1, sh1, sc1)
    h_cat = jnp.concatenate([_modulate(xp_ref[...], g1, sh1, sc1), h_cur,
                             _modulate(xn_ref[...], g1, sh1, sc1)], axis=0)

    t = off + lax.broadcasted_iota(jnp.int32, (SEG, 1), 0)
    p = off - SEG + lax.broadcasted_iota(jnp.int32, (1, 3 * SEG), 1)
    mixes = []
    for gi, w in enumerate(POOL_WINDOWS):
        sl = slice(gi * POOL_G, (gi + 1) * POOL_G)
        lo = jnp.maximum(t - w // 2, 0)
        hi = jnp.minimum(t + (w - w // 2), t_len)
        band = _mask01((p >= lo) & (p < hi))
        hg = h_cat[:, sl]
        hg_hi = hg.astype(BF16)
        hg_lo = (hg - hg_hi.astype(F32)).astype(BF16)
        window_sum = _dot(band, hg_hi) + _dot(band, hg_lo)
        y = (window_sum / (hi - lo).astype(F32) - h_cur[:, sl]).astype(BF16)
        mixes.append(_dot(y, pw_ref[gi].astype(BF16)))
    mix = jnp.concatenate(mixes, axis=1) * ps_ref[...]
    xn = x + gt1_ref[0] * mix
    xo_ref[...] = xn
    h2 = _modulate(xn, g2_ref[0], sh2_ref[0], sc2_ref[0])
    h2_ref[...] = h2.astype(BF16)

    logits = jnp.dot(h2, wr_ref[...], preferred_element_type=F32, precision=lax.Precision.HIGHEST)
    lane = lax.broadcasted_iota(jnp.int32, logits.shape, 1)
    lane_f = lane.astype(F32)
    lg = jnp.where(lane < N_EXP, logits, -jnp.inf)
    m1 = jnp.max(lg, axis=-1, keepdims=True)
    i1 = jnp.min(jnp.where(lg == m1, lane_f, float(LANES)), axis=-1, keepdims=True)
    lg2 = jnp.where(lane_f == i1, -jnp.inf, lg)
    m2 = jnp.max(lg2, axis=-1, keepdims=True)
    i2 = jnp.min(jnp.where(lg2 == m2, lane_f, float(LANES)), axis=-1, keepdims=True)
    e = jnp.exp(m2 - m1)
    w1 = 1.0 / (1.0 + e)
    w2 = e / (1.0 + e)
    ridx_ref[...] = jnp.where(lane == 0, i1, jnp.where(lane == 1, i2, 0.0)).astype(jnp.int32)
    rw_ref[...] = jnp.where(lane == 0, w1, jnp.where(lane == 1, w2, 0.0))


def _pool_and_route(lay, x, normg, modtab, pool_w, pool_scale, w_router, l):
    i_odd = l // 2
    nblk = lay.n // SEG
    blk = lambda imap: pl.BlockSpec((SEG, D_MODEL), imap)
    lane_blk = pl.BlockSpec((SEG, LANES), lambda i: (i, 0))
    return pl.pallas_call(
        functools.partial(_pool_kernel, nc=lay.nc, seq=lay.seq, dec_seq=lay.dec_seq),
        out_shape=(jax.ShapeDtypeStruct((lay.n, D_MODEL), F32),
                   jax.ShapeDtypeStruct((lay.n, D_MODEL), BF16),
                   jax.ShapeDtypeStruct((lay.n, LANES), jnp.int32),
                   jax.ShapeDtypeStruct((lay.n, LANES), F32)),
        grid=(nblk,),
        in_specs=[blk(lambda i: (jnp.maximum(i - 1, 0), 0)),
                  blk(lambda i: (i, 0)),
                  blk(lambda i: (jnp.minimum(i + 1, nblk - 1), 0)),
                  _row_spec(l, 0), lay.mod_spec(l, 0, SEG), lay.mod_spec(l, 1, SEG), lay.mod_spec(l, 2, SEG),
                  pl.BlockSpec((None, len(POOL_WINDOWS), POOL_G, POOL_G), lambda i: (i_odd, 0, 0, 0)),
                  pl.BlockSpec((None, 1, D_MODEL), lambda i: (i_odd, 0, 0)),
                  _row_spec(l, 1), lay.mod_spec(l, 3, SEG), lay.mod_spec(l, 4, SEG),
                  pl.BlockSpec((None, D_MODEL, LANES), lambda i: (i_odd, 0, 0))],
        out_specs=(blk(lambda i: (i, 0)), blk(lambda i: (i, 0)), lane_blk, lane_blk),
        compiler_params=_params(("arbitrary",)),
        name="pool_route",
    )(x, x, x, normg, modtab, modtab, modtab, pool_w, pool_scale, normg, modtab, modtab, w_router)


def _moe_kernel(ge_ref, gs_ref, gn_ref, xs_hbm, wg_ref, wu_ref, wd_ref, ys_hbm, xbuf, acc, sem_in, sem_out):
    g = pl.program_id(0)
    j = pl.program_id(1)
    nsub = gn_ref[g]
    row0 = gs_ref[g]

    def in_copy(s):
        rows = pl.ds(pl.multiple_of(row0 + s * MOE_SUB, MOE_SUB), MOE_SUB)
        return pltpu.make_async_copy(xs_hbm.at[rows], xbuf.at[pl.ds(pl.multiple_of(s * MOE_SUB, MOE_SUB), MOE_SUB)],
                                     sem_in)

    def out_copy(s):
        rows = pl.ds(pl.multiple_of(row0 + s * MOE_SUB, MOE_SUB), MOE_SUB)
        return pltpu.make_async_copy(acc.at[pl.ds(pl.multiple_of(s * MOE_SUB, MOE_SUB), MOE_SUB)], ys_hbm.at[rows],
                                     sem_out)

    def for_each_sub(fn):
        def body(s, carry):
            fn(s)
            return carry
        lax.fori_loop(0, nsub, body, 0)

    @pl.when(j == 0)
    def _():
        for_each_sub(lambda s: in_copy(s).start())
        for_each_sub(lambda s: in_copy(s).wait())

    wg = wg_ref[...].astype(BF16)
    wu = wu_ref[...].astype(BF16)
    wd = wd_ref[...].astype(BF16)

    def sub_tile(s):
        rows = pl.ds(pl.multiple_of(s * MOE_SUB, MOE_SUB), MOE_SUB)
        xs = xbuf[rows, :]
        a = (_silu(_dot(xs, wg)) * _dot(xs, wu)).astype(BF16)
        contrib = _dot(a, wd)

        @pl.when(j == 0)
        def _():
            acc[rows, :] = contrib

        @pl.when(j > 0)
        def _():
            acc[rows, :] += contrib

    for_each_sub(sub_tile)

    @pl.when(j == pl.num_programs(1) - 1)
    def _():
        for_each_sub(lambda s: out_copy(s).start())
        for_each_sub(lambda s: out_copy(s).wait())


def _moe_plan(ridx, n_tok):
    n_pairs = 2 * n_tok
    p_max = n_pairs + N_EXP * MOE_SUB
    group_rows = MOE_SUB * MOE_GROUP
    g_max = -(-n_pairs // group_rows) + N_EXP
    e_flat = ridx[:, :2].reshape(n_pairs)
    onehot = (e_flat[:, None] == jnp.arange(N_EXP, dtype=jnp.int32)[None, :]).astype(jnp.int32)
    rank = jnp.sum((jnp.cumsum(onehot, axis=0) - onehot) * onehot, axis=1)
    counts = jnp.sum(onehot, axis=0)
    nsub_e = (counts + MOE_SUB - 1) // MOE_SUB
    region = nsub_e * MOE_SUB
    start_e = jnp.cumsum(region) - region
    dest = start_e[e_flat] + rank
    src_tok = jnp.zeros((p_max,), jnp.int32).at[dest].set(jnp.arange(n_pairs, dtype=jnp.int32) // 2)

    ngrp_e = (nsub_e + MOE_GROUP - 1) // MOE_GROUP
    gend = jnp.cumsum(ngrp_e)
    gstart = gend - ngrp_e
    total = gend[-1]
    gid = jnp.arange(g_max, dtype=jnp.int32)
    ge = jnp.minimum(jnp.searchsorted(gend, gid, side="right").astype(jnp.int32), N_EXP - 1)
    kk = gid - gstart[ge]
    live = gid < total
    last_e = jnp.minimum(jnp.searchsorted(gend, total - 1, side="right").astype(jnp.int32), N_EXP - 1)
    g_expert = jnp.where(live, ge, last_e)
    g_row = jnp.where(live, start_e[ge] + kk * group_rows, 0)
    g_nsub = jnp.where(live, jnp.clip(nsub_e[ge] - kk * MOE_GROUP, 0, MOE_GROUP), 0)
    return src_tok, dest.reshape(n_tok, 2), g_expert.astype(jnp.int32), g_row.astype(jnp.int32), g_nsub.astype(jnp.int32)


def _moe_experts(xs, g_expert, g_row, g_nsub, moe_gate, moe_up, moe_down, i_odd):
    p_max = xs.shape[0]
    g_max = g_expert.shape[0]
    tf = FF_TILE
    grid_spec = pltpu.PrefetchScalarGridSpec(
        num_scalar_prefetch=3,
        grid=(g_max, D_FF // tf),
        in_specs=[pl.BlockSpec(memory_space=pl.ANY),
                  pl.BlockSpec((None, None, D_MODEL, tf), lambda g, j, ge, gs, gn: (i_odd, ge[g], 0, j)),
                  pl.BlockSpec((None, None, D_MODEL, tf), lambda g, j, ge, gs, gn: (i_odd, ge[g], 0, j)),
                  pl.BlockSpec((None, None, tf, D_MODEL), lambda g, j, ge, gs, gn: (i_odd, ge[g], j, 0))],
        out_specs=pl.BlockSpec(memory_space=pl.ANY),
        scratch_shapes=[pltpu.VMEM((MOE_SUB * MOE_GROUP, D_MODEL), BF16),
                        pltpu.VMEM((MOE_SUB * MOE_GROUP, D_MODEL), F32),
                        pltpu.SemaphoreType.DMA, pltpu.SemaphoreType.DMA],
    )
    return pl.pallas_call(
        _moe_kernel,
        out_shape=jax.ShapeDtypeStruct((p_max, D_MODEL), F32),
        grid_spec=grid_spec,
        compiler_params=_params(("arbitrary", "arbitrary")),
        name="moe_experts",
    )(g_expert, g_row, g_nsub, xs, moe_gate, moe_up, moe_down)


def _combine_kernel(x_ref, ya_ref, yb_ref, rw_ref, gt_ref, fg_ref, o_ref, *, final):
    rw = rw_ref[...]
    y = rw[:, 0:1] * ya_ref[...] + rw[:, 1:2] * yb_ref[...]
    xn = x_ref[...] + gt_ref[0] * y
    if final:
        xn = xn * lax.rsqrt(jnp.mean(xn * xn, axis=-1, keepdims=True) + EPS) * fg_ref[...]
    o_ref[...] = xn


def _moe_combine(lay, x, ya, yb, rw, modtab, final_g, l, final):
    tm = lay.tm
    full = lambda: pl.BlockSpec((tm, D_MODEL), lambda i: (i, 0))
    return pl.pallas_call(
        functools.partial(_combine_kernel, final=final),
        out_shape=jax.ShapeDtypeStruct((lay.n, D_MODEL), F32),
        grid=(lay.n // tm,),
        in_specs=[full(), full(), full(),
                  pl.BlockSpec((tm, LANES), lambda i: (i, 0)),
                  lay.mod_spec(l, 5, tm),
                  pl.BlockSpec((1, D_MODEL), lambda i: (0, 0))],
        out_specs=full(),
        compiler_params=_params(("arbitrary",)),
        name="moe_combine",
    )(x, ya, yb, rw, modtab, final_g)


def kernel(x_prompt, x_sample, cache_k_ctx, cache_v_ctx, state_delta, c, c_ctx, w_mod, b_mod, norm_g, final_g,
           w_in, conv_w, a_log, dt_bias, rpb, o_gain, w_out, ffn_gate, ffn_up, ffn_down, pool_w, pool_scale,
           w_router, moe_gate, moe_up, moe_down):
    batch, seq, d = x_prompt.shape
    dec_batch, dec_seq, _ = x_sample.shape
    depth = w_mod.shape[0]
    n_even = w_in.shape[0]
    past = cache_k_ctx.shape[2]
    assert d == D_MODEL and depth % 2 == 0
    lay = _Layout(batch, seq, dec_batch, dec_seq)

    cond = jnp.zeros((lay.mod_rows, d), F32).at[0].set(c_ctx).at[1:1 + dec_batch].set(c)
    mod = _adaln(cond, w_mod, b_mod)
    modtab = mod.reshape(depth, lay.mod_rows, 6, d).transpose(0, 2, 1, 3).reshape(depth * 6 * lay.mod_rows, 1, d)
    normg = norm_g.reshape(depth * 2, 1, d)

    x = jnp.concatenate([x_prompt.reshape(lay.nc, d), x_sample.reshape(lay.nl, d)], axis=0)
    cache_k = cache_k_ctx.reshape(dec_batch, n_even, past, A_W)
    cache_v = cache_v_ctx.reshape(dec_batch, n_even, past, A_W)
    rows = dec_seq // GRID_W
    kh = min(NA_KH, rows)
    n_main = 3 * A_W + 4 * B_W
    pad128 = lambda a: jnp.pad(a.reshape(1, -1), ((0, 0), (0, LANES - a.size)))
    w_router_p = jnp.pad(w_router, ((0, 0), (0, 0), (0, LANES - N_EXP)))

    k_list, v_list, s_list = [], [], []
    for l in range(depth):
        i = l // 2
        if l % 2 == 0:
            w_ab = jnp.pad(w_in[i][:, n_main:], ((0, 0), (0, LANES - 4 * H_B)))
            proj, ab = _even_project(lay, x, normg, modtab, w_in, w_ab, l)
            bias = _na_bias_table(rpb[i], rows, kh)
            oa = _attention(lay, proj, cache_k, cache_v, bias, i)
            alog, dtb = pad128(a_log[i]), pad128(dt_bias[i])
            qc, kc, vc = _delta_prep(proj, conv_w, i, 0, batch, seq)
            of_c, ob_c, s_fin = _delta_net(qc, kc, vc, ab, 0, alog, dtb, None, batch, seq)
            ql, kl, vl = _delta_prep(proj, conv_w, i, lay.nc // dec_seq, dec_batch, dec_seq)
            of_l, ob_l, _ = _delta_net(ql, kl, vl, ab, lay.nc // SEG, alog, dtb, state_delta[:, i], dec_batch, dec_seq)
            o_f = jnp.concatenate([of_c, of_l], axis=0)
            o_b = jnp.concatenate([ob_c, ob_l], axis=0)
            x, h2 = _even_output(lay, oa, o_f, o_b, proj, o_gain.reshape(n_even, 1, DV), w_out, x, normg, modtab, l)
            x = _dense_ffn(lay, h2, x, ffn_gate, ffn_up, ffn_down, modtab, l)
            k_list.append(proj[:lay.nc, A_W:2 * A_W].reshape(batch, seq, H_A, DH_A))
            v_list.append(proj[:lay.nc, 2 * A_W:3 * A_W].reshape(batch, seq, H_A, DH_A))
            s_list.append(s_fin)
        else:
            x, h2, ridx, rw = _pool_and_route(lay, x, normg, modtab, pool_w,
                                              pool_scale.reshape(-1, 1, d), w_router_p, l)
            src_tok, dest, g_expert, g_row, g_nsub = _moe_plan(ridx, lay.n)
            xs = jnp.take(h2, src_tok, axis=0)
            ys = _moe_experts(xs, g_expert, g_row, g_nsub, moe_gate, moe_up, moe_down, i)
            ya = jnp.take(ys, dest[:, 0], axis=0)
            yb = jnp.take(ys, dest[:, 1], axis=0)
            x = _moe_combine(lay, x, ya, yb, rw, modtab, final_g.reshape(1, d), l, final=(l == depth - 1))
    y_prompt = x[:lay.nc].reshape(batch, seq, d)
    y_sample = x[lay.nc:].reshape(dec_batch, dec_seq, d)
    return (y_prompt, y_sample, jnp.stack(k_list, axis=1), jnp.stack(v_list, axis=1), jnp.stack(s_list, axis=1))
```

```python
import functools
import math

import numpy as np
import jax
import jax.numpy as jnp
from jax import lax
from jax.experimental import pallas as pl
from jax.experimental.pallas import tpu as pltpu

F32 = jnp.float32
BF16 = jnp.bfloat16

D_MODEL = 1024
GRID_W = 64
DH_A = 64
H_A = 8
A_W = H_A * DH_A
NA_KH = 8
NA_KW = 16
DK = 128
DV = 128
H_B = 4
B_W = H_B * DK
CONV_K = 3
CHUNK = 64
POOL_WINDOWS = (2, 4, 8, 16)
POOL_G = D_MODEL // len(POOL_WINDOWS)
D_FF = 7 * D_MODEL // 2
N_EXP = 8
EPS = 1e-6
NEG_INF = -1e30

LANES = 128
SEG = 256
FF_TILE = 512
MOE_SUB = 256
MOE_GROUP = 8
VMEM_LIMIT = 56 * 2 ** 20


def _params(sem, vmem=VMEM_LIMIT):
    return pltpu.CompilerParams(dimension_semantics=sem, vmem_limit_bytes=vmem)


def _silu(x):
    return x * jax.nn.sigmoid(x)


def _dot(a, b):
    return jnp.dot(a, b, preferred_element_type=F32)


def _dot_nt(a, b):
    return lax.dot_general(a, b, (((1,), (1,)), ((), ())), preferred_element_type=F32)


def _dot_f32x3(a01, x):
    x1 = x.astype(BF16)
    r1 = x - x1.astype(F32)
    x2 = r1.astype(BF16)
    x3 = (r1 - x2.astype(F32)).astype(BF16)
    return _dot(a01, x1) + _dot(a01, x2) + _dot(a01, x3)


def _mask01(mask):
    return jnp.where(mask, 1.0, 0.0).astype(BF16)


def _modulate(x, g, shift, scale):
    y = x * lax.rsqrt(jnp.mean(x * x, axis=-1, keepdims=True) + EPS)
    return (y * g) * (1.0 + scale) + shift


def _adaln_kernel(cond_ref, w_ref, b_ref, o_ref):
    s = _silu(cond_ref[...]).astype(BF16)
    o_ref[0] = _dot(s, w_ref[0].astype(BF16)) + b_ref[0]


def _adaln(cond, w_mod, b_mod):
    depth, d, six_d = w_mod.shape
    r = cond.shape[0]
    tn = six_d // 4
    return pl.pallas_call(
        _adaln_kernel,
        out_shape=jax.ShapeDtypeStruct((depth, r, six_d), F32),
        grid=(depth, six_d // tn),
        in_specs=[
            pl.BlockSpec((r, d), lambda l, j: (0, 0)),
            pl.BlockSpec((1, d, tn), lambda l, j: (l, 0, j)),
            pl.BlockSpec((1, 1, tn), lambda l, j: (l, 0, j)),
        ],
        out_specs=pl.BlockSpec((1, r, tn), lambda l, j: (l, 0, j)),
        compiler_params=_params(("arbitrary", "arbitrary")),
        name="adaln",
    )(cond, w_mod, b_mod.reshape(depth, 1, six_d))


class _Layout:
    def __init__(self, batch, seq, dec_batch, dec_seq):
        self.batch, self.seq, self.dec_batch, self.dec_seq = batch, seq, dec_batch, dec_seq
        self.nc = batch * seq
        self.nl = dec_batch * dec_seq
        self.n = self.nc + self.nl
        self.mod_rows = -(-(1 + dec_batch) // 8) * 8
        assert seq % SEG == 0 and dec_seq % SEG == 0 and dec_seq % GRID_W == 0
        self.tm = math.gcd(self.nc, 1024)
        assert self.tm % SEG == 0 and dec_seq % self.tm == 0 and self.nc % dec_seq == 0

    def mod_row(self, first_row):
        return jnp.maximum((first_row - self.nc) // self.dec_seq + 1, 0)

    def mod_spec(self, l, k, rows_per_step):
        base = (l * 6 + k) * self.mod_rows
        return pl.BlockSpec((1, 1, D_MODEL),
                            lambda i, *_: (base + self.mod_row(i * rows_per_step), 0, 0))


def _row_spec(l, k):
    return pl.BlockSpec((1, 1, D_MODEL), lambda i, *_: (l * 2 + k, 0, 0))


def _proj_kernel(x_ref, g_ref, sh_ref, sc_ref, w_ref, wab_ref, proj_ref, ab_ref, h_scr):
    @pl.when(pl.program_id(1) == 0)
    def _():
        h = _modulate(x_ref[...], g_ref[0], sh_ref[0], sc_ref[0]).astype(BF16)
        h_scr[...] = h
        ab_ref[...] = _dot(h, wab_ref[...].astype(BF16))

    proj_ref[...] = _dot(h_scr[...], w_ref[...].astype(BF16))


def _even_project(lay, x, normg, modtab, w_in, w_ab, l):
    i_even = l // 2
    tm, tn = lay.tm, 512
    ncols = 3 * A_W + 4 * B_W
    return pl.pallas_call(
        _proj_kernel,
        out_shape=(jax.ShapeDtypeStruct((lay.n, ncols), F32),
                   jax.ShapeDtypeStruct((lay.n, LANES), F32)),
        grid=(lay.n // tm, ncols // tn),
        in_specs=[
            pl.BlockSpec((tm, D_MODEL), lambda i, j: (i, 0)),
            _row_spec(l, 0),
            lay.mod_spec(l, 0, tm),
            lay.mod_spec(l, 1, tm),
            pl.BlockSpec((None, D_MODEL, tn), lambda i, j: (i_even, 0, j)),
            pl.BlockSpec((D_MODEL, LANES), lambda i, j: (0, 0)),
        ],
        out_specs=(pl.BlockSpec((tm, tn), lambda i, j: (i, j)),
                   pl.BlockSpec((tm, LANES), lambda i, j: (i, 0))),
        scratch_shapes=[pltpu.VMEM((tm, D_MODEL), BF16)],
        compiler_params=_params(("arbitrary", "arbitrary")),
        name="even_project",
    )(x, normg, modtab, modtab, w_in, w_ab)


def _softmax_av(scores, values):
    m = functools.reduce(jnp.maximum, [jnp.max(s, axis=-1, keepdims=True) for s in scores])
    ps = [jnp.exp(s - m) for s in scores]
    denom = functools.reduce(jnp.add, [jnp.sum(p, axis=-1, keepdims=True) for p in ps])
    acc = functools.reduce(jnp.add, [_dot(p.astype(BF16), v) for p, v in zip(ps, values)])
    return acc / denom


def _ctx_attn_kernel(q_ref, k_ref, v_ref, o_ref):
    q, k, v = q_ref[...], k_ref[...], v_ref[...]
    scale = DH_A ** -0.5
    outs = []
    for h in range(H_A):
        sl = slice(h * DH_A, (h + 1) * DH_A)
        s = _dot_nt(q[:, sl].astype(BF16), k[:, sl].astype(BF16)) * scale
        outs.append(_softmax_av([s], [v[:, sl].astype(BF16)]))
    o_ref[...] = jnp.concatenate(outs, axis=1)


def _na_attn_kernel(q_ref, k_ref, v_ref, kc_ref, vc_ref, bias_ref, o_ref, *, rows, kh):
    r = pl.program_id(1)
    row_lo = jnp.clip(r - kh // 2, 0, rows - kh)
    start = pl.multiple_of(row_lo * GRID_W, GRID_W)
    q = q_ref[...]
    kl = k_ref[pl.ds(start, kh * GRID_W), :]
    vl = v_ref[pl.ds(start, kh * GRID_W), :]
    kc, vc = kc_ref[...], vc_ref[...]
    scale = DH_A ** -0.5
    outs = []
    for h in range(H_A):
        sl = slice(h * DH_A, (h + 1) * DH_A)
        qh = q[:, sl].astype(BF16)
        s_loc = _dot_nt(qh, kl[:, sl].astype(BF16)) * scale + bias_ref[h, 0]
        s_ctx = _dot_nt(qh, kc[:, sl].astype(BF16)) * scale
        outs.append(_softmax_av([s_loc, s_ctx], [vl[:, sl].astype(BF16), vc[:, sl].astype(BF16)]))
    o_ref[...] = jnp.concatenate(outs, axis=1)


def _na_bias_table(rpb, rows, kh):
    r = np.arange(rows)
    row_idx = np.clip(r - kh // 2, 0, rows - kh)[:, None] + np.arange(kh)[None, :]
    dr = row_idx - r[:, None] + (NA_KH - 1)
    qcol = np.arange(GRID_W)
    kcol = np.arange(GRID_W)
    col_lo = np.clip(qcol - NA_KW // 2, 0, GRID_W - NA_KW)
    valid = (kcol[None, :] >= col_lo[:, None]) & (kcol[None, :] < col_lo[:, None] + NA_KW)
    dc = np.clip(kcol[None, :] - qcol[:, None], 1 - NA_KW, NA_KW - 1) + (NA_KW - 1)
    onehot = (dc[None, :, :] == np.arange(2 * NA_KW - 1)[:, None, None]).astype(np.float32)
    picked = jnp.einsum("hrjc,cqk->hrqjk", rpb[:, dr].astype(F32), onehot, precision=lax.Precision.HIGHEST)
    bias = jnp.where(valid[None, None, :, None, :], picked, NEG_INF)
    return bias.reshape(H_A, rows, GRID_W, kh * GRID_W)


def _attention(lay, proj, cache_k, cache_v, bias, i_even):
    ctx = pl.pallas_call(
        _ctx_attn_kernel,
        out_shape=jax.ShapeDtypeStruct((lay.nc, A_W), F32),
        grid=(lay.batch,),
        in_specs=[pl.BlockSpec((lay.seq, A_W), lambda b: (b, 0)),
                  pl.BlockSpec((lay.seq, A_W), lambda b: (b, 1)),
                  pl.BlockSpec((lay.seq, A_W), lambda b: (b, 2))],
        out_specs=pl.BlockSpec((lay.seq, A_W), lambda b: (b, 0)),
        compiler_params=_params(("arbitrary",)),
        name="ctx_attention",
    )(proj, proj, proj)

    rows = lay.dec_seq // GRID_W
    kh = min(NA_KH, rows)
    past = cache_k.shape[2]
    q0 = lay.nc // GRID_W
    b0 = lay.nc // lay.dec_seq
    lat = pl.pallas_call(
        functools.partial(_na_attn_kernel, rows=rows, kh=kh),
        out_shape=jax.ShapeDtypeStruct((lay.nl, A_W), F32),
        grid=(lay.dec_batch, rows),
        in_specs=[pl.BlockSpec((GRID_W, A_W), lambda b, r: (q0 + b * rows + r, 0)),
                  pl.BlockSpec((lay.dec_seq, A_W), lambda b, r: (b0 + b, 1)),
                  pl.BlockSpec((lay.dec_seq, A_W), lambda b, r: (b0 + b, 2)),
                  pl.BlockSpec((None, None, past, A_W), lambda b, r: (b, i_even, 0, 0)),
                  pl.BlockSpec((None, None, past, A_W), lambda b, r: (b, i_even, 0, 0)),
                  pl.BlockSpec((H_A, 1, GRID_W, kh * GRID_W), lambda b, r: (0, r, 0, 0))],
        out_specs=pl.BlockSpec((GRID_W, A_W), lambda b, r: (b * rows + r, 0)),
        compiler_params=_params(("arbitrary", "arbitrary")),
        name="na_attention",
    )(proj, proj, proj, cache_k, cache_v, bias)
    return jnp.concatenate([ctx, lat], axis=0)


def _delta_prep_kernel(xq_ref, xk_ref, xv_ref, wq_ref, wk_ref, wv_ref, q_ref, k_ref, v_ref):
    t = xq_ref.shape[0]
    row = lax.broadcasted_iota(jnp.int32, (t, 1), 0)

    def conv_silu(x, w):
        prev = jnp.where(row == 0, 0.0, pltpu.roll(x, 1, 0))
        nxt = jnp.where(row == t - 1, 0.0, pltpu.roll(x, t - 1, 0))
        return _silu(prev * w[0:1] + x * w[1:2] + nxt * w[2:3])

    def l2norm(x):
        return x * lax.rsqrt(jnp.sum(x * x, axis=-1, keepdims=True) + EPS)

    q_ref[...] = l2norm(conv_silu(xq_ref[...], wq_ref[...])) * (DK ** -0.5)
    k_ref[...] = l2norm(conv_silu(xk_ref[...], wk_ref[...]))
    v_ref[...] = conv_silu(xv_ref[...], wv_ref[...])


def _delta_prep(proj, conv_w, i_even, first_seq, n_seq, t):
    c0 = 3 * A_W // LANES
    xspec = lambda part: pl.BlockSpec((t, LANES), lambda b, h: (first_seq + b, c0 + part * H_B + h))
    wspec = lambda part: pl.BlockSpec((None, CONV_K, LANES), lambda b, h: (i_even, 0, part * H_B + h))
    ospec = pl.BlockSpec((t, LANES), lambda b, h: (b, h))
    shape = jax.ShapeDtypeStruct((n_seq * t, B_W), F32)
    return pl.pallas_call(
        _delta_prep_kernel,
        out_shape=(shape, shape, shape),
        grid=(n_seq, H_B),
        in_specs=[xspec(0), xspec(1), xspec(2), wspec(0), wspec(1), wspec(2)],
        out_specs=(ospec, ospec, ospec),
        compiler_params=_params(("arbitrary", "arbitrary")),
        name="delta_prep",
    )(proj, proj, proj, conv_w, conv_w, conv_w)


GATE_GC, GATE_BETA, GATE_GT = 0, 2 * H_B, 4 * H_B


def _delta_gates_kernel(ab_ref, alog_ref, dtb_ref, g_ref, gt_ref):
    n = SEG
    ab = ab_ref[...]
    lane = lax.broadcasted_iota(jnp.int32, (1, LANES), 1)
    ri = lax.broadcasted_iota(jnp.int32, (n, n), 0)
    ci = lax.broadcasted_iota(jnp.int32, (n, n), 1)
    shift = int(math.log2(CHUNK))
    same = jnp.right_shift(ri, shift) == jnp.right_shift(ci, shift)
    g_all = -jnp.exp(alog_ref[...]) * jax.nn.softplus(ab + dtb_ref[...])
    prefix = _dot_f32x3(_mask01(same & (ci <= ri)), g_all)
    total = _dot_f32x3(_mask01(same), g_all)
    gc = jnp.where(lane < H_B, prefix, total - prefix + g_all)
    table = jnp.where(lane < GATE_BETA, gc,
                      jnp.where(lane < GATE_GT, jax.nn.sigmoid(ab), pltpu.roll(total, GATE_GT, 1)))
    g_ref[...] = table
    gt_ref[...] = table.T


def _delta_gates(ab, alog, dtb):
    n = ab.shape[0]
    row = pl.BlockSpec((1, LANES), lambda i: (0, 0))
    return pl.pallas_call(
        _delta_gates_kernel,
        out_shape=(jax.ShapeDtypeStruct((n, LANES), F32), jax.ShapeDtypeStruct((LANES, n), F32)),
        grid=(n // SEG,),
        in_specs=[pl.BlockSpec((SEG, LANES), lambda i: (i, 0)), row, row],
        out_specs=(pl.BlockSpec((SEG, LANES), lambda i: (i, 0)), pl.BlockSpec((LANES, SEG), lambda i: (0, i))),
        compiler_params=_params(("arbitrary",)),
        name="delta_gates",
    )(ab, alog, dtb)


def _delta_masks(d):
    n = SEG
    ri = lax.broadcasted_iota(jnp.int32, (n, n), 0)
    ci = lax.broadcasted_iota(jnp.int32, (n, n), 1)
    shift = int(math.log2(CHUNK))
    same = jnp.right_shift(ri, shift) == jnp.right_shift(ci, shift)
    incl = same & ((ci <= ri) if d == 0 else (ci >= ri))
    strict = same & ((ci < ri) if d == 0 else (ci > ri))
    levels = []
    for level in range(shift):
        bi, bj = jnp.right_shift(ri, level), jnp.right_shift(ci, level)
        siblings = jnp.right_shift(bi, 1) == jnp.right_shift(bj, 1)
        levels.append(siblings & ((bi > bj) if d == 0 else (bi < bj)))
    return incl, strict, jnp.where(ri == ci, 1.0, 0.0), levels


def _delta_chains(chains):
    n = SEG
    n_chunks = n // CHUNK
    each = lambda fn, *lists: [fn(*args) for args in zip(*lists)]
    ds, masks, qs, ks, vs, betas, gcs, gts, gc_rows, gt_rows, states = (list(t) for t in zip(*chains))
    incls, stricts = [m[0] for m in masks], [m[1] for m in masks]
    n_levels = len(masks[0][3])

    decays = each(lambda m, gc, gr: jnp.where(m, jnp.exp(jnp.where(m, gc - gr, 0.0)), 0.0), incls, gcs, gc_rows)
    kbs = each(lambda k, b: k * b, ks, betas)
    k16s = [k.astype(BF16) for k in ks]
    grams = each(lambda kb, k16: _dot_nt(kb.astype(BF16), k16), kbs, k16s)
    lmats = each(lambda m, g, dec: jnp.where(m, g * dec, 0.0), stricts, grams, decays)

    xs = [m[2] for m in masks]
    for level in range(n_levels):
        x16s = [x.astype(BF16) for x in xs]
        cs = each(lambda m, lm: jnp.where(m[3][level], lm, 0.0).astype(BF16), masks, lmats)
        xcs = each(lambda x16, c: _dot(x16, c).astype(BF16), x16s, cs)
        xs = each(lambda x, xc, x16: x - _dot(xc, x16), xs, xcs, x16s)

    egs = [jnp.exp(gc) for gc in gcs]
    rhss = each(lambda v, b, kb, eg: jnp.concatenate([v * b, kb * eg], axis=1).astype(BF16), vs, betas, kbs, egs)
    uws = each(lambda x, rhs: _dot(x.astype(BF16), rhs), xs, rhss)
    qks = each(lambda q, k16: _dot_nt(q.astype(BF16), k16), qs, k16s)
    attns = each(lambda m, qk, dec: jnp.where(m, qk * dec, 0.0).astype(BF16), incls, qks, decays)
    qgs = each(lambda q, eg: (q * eg).astype(BF16), qs, egs)
    kd_ts = each(lambda k, gt_r, gc_r: (k.T * jnp.exp(gt_r - gc_r)).astype(BF16), ks, gt_rows, gc_rows)
    gls = [jnp.exp(gt) for gt in gts]

    v_new = [[None] * n_chunks for _ in chains]
    o_state = [[None] * n_chunks for _ in chains]
    for step in range(n_chunks):
        for i, d in enumerate(ds):
            c = step if d == 0 else n_chunks - 1 - step
            rs = slice(c * CHUNK, (c + 1) * CHUNK)
            s16 = states[i].astype(BF16)
            vn = uws[i][rs, :DV] - _dot(uws[i][rs, DV:].astype(BF16), s16)
            o_state[i][c] = _dot(qgs[i][rs], s16)
            v_new[i][c] = vn
            pieces = [jnp.zeros((c * CHUNK, DV), BF16), vn.astype(BF16),
                      jnp.zeros((n - (c + 1) * CHUNK, DV), BF16)]
            padded = jnp.concatenate([p for p in pieces if p.shape[0]], axis=0)
            states[i] = states[i] * gls[i][c * CHUNK:c * CHUNK + 1, :] + _dot(kd_ts[i], padded)
    outs = each(lambda os, a, vn: jnp.concatenate(os, axis=0) + _dot(a, jnp.concatenate(vn, axis=0).astype(BF16)),
                o_state, attns, v_new)
    return list(zip(outs, states))


def _delta_kernel(*refs, has_s0):
    dir_refs = (refs[0:5], refs[5:10])
    refs = refs[10:]
    if has_s0:
        s0_ref, refs = refs[0], refs[1:]
    of_ref, ob_ref, sfin_ref, s_scr = refs
    seg = pl.program_id(1)

    @pl.when(seg == 0)
    def _():
        if has_s0:
            s_scr[...] = s0_ref[...]
        else:
            s_scr[...] = jnp.zeros_like(s_scr)

    chains = []
    for d, (q_ref, k_ref, v_ref, g_ref, gt_ref) in enumerate(dir_refs):
        masks = _delta_masks(d)
        q, k, v, gates, gates_t = q_ref[...], k_ref[...], v_ref[...], g_ref[...], gt_ref[...]
        for h in range(H_B):
            sl = slice(h * DK, (h + 1) * DK)
            col = d * H_B + h
            pick = lambda base: gates[:, base + col:base + col + 1]
            pick_t = lambda base: gates_t[base + col:base + col + 1, :]
            chains.append((d, masks, q[:, sl], k[:, sl], v[:, sl], pick(GATE_BETA), pick(GATE_GC),
                           pick(GATE_GT), pick_t(GATE_GC), pick_t(GATE_GT), s_scr[d, h]))
    results = _delta_chains(chains)
    for d, o_ref in enumerate((of_ref, ob_ref)):
        o_ref[...] = jnp.concatenate([results[d * H_B + h][0] for h in range(H_B)], axis=1)
        for h in range(H_B):
            s_scr[d, h] = results[d * H_B + h][1]

    @pl.when(seg == pl.num_programs(1) - 1)
    def _():
        sfin_ref[...] = s_scr[...]


def _delta_net(q, k, v, gates, gates_t, first_seg, s0, n_seq, t):
    nseg = t // SEG
    fwd = lambda b, s: b * nseg + s
    bwd = lambda b, s: b * nseg + nseg - 1 - s
    state = pl.BlockSpec((None, 2, H_B, DK, DV), lambda b, s: (b, 0, 0, 0, 0))
    in_specs, args = [], []
    for seg_of in (fwd, bwd):
        in_specs += [pl.BlockSpec((SEG, B_W), lambda b, s, f=seg_of: (f(b, s), 0))] * 3
        in_specs += [pl.BlockSpec((SEG, LANES), lambda b, s, f=seg_of: (first_seg + f(b, s), 0)),
                     pl.BlockSpec((LANES, SEG), lambda b, s, f=seg_of: (0, first_seg + f(b, s)))]
        args += [q, k, v, gates, gates_t]
    if s0 is not None:
        in_specs.append(state)
        args.append(s0)
    oshape = jax.ShapeDtypeStruct((n_seq * t, B_W), F32)
    return pl.pallas_call(
        functools.partial(_delta_kernel, has_s0=s0 is not None),
        out_shape=(oshape, oshape, jax.ShapeDtypeStruct((n_seq, 2, H_B, DK, DV), F32)),
        grid=(n_seq, nseg),
        in_specs=in_specs,
        out_specs=(pl.BlockSpec((SEG, B_W), lambda b, s: (fwd(b, s), 0)),
                   pl.BlockSpec((SEG, B_W), lambda b, s: (bwd(b, s), 0)), state),
        scratch_shapes=[pltpu.VMEM((2, H_B, DK, DV), F32)],
        compiler_params=_params(("arbitrary", "arbitrary")),
        name="delta_net",
    )(*args)


def _even_out_kernel(oa_ref, of_ref, ob_ref, z_ref, gain_ref, w_ref, x_ref, gt_ref,
                     g2_ref, sh2_ref, sc2_ref, xo_ref, h2_ref):
    ob = of_ref[...] + ob_ref[...]
    z = z_ref[...]
    parts = [oa_ref[...].astype(BF16)]
    for h in range(H_B):
        sl = slice(h * DV, (h + 1) * DV)
        o_h = ob[:, sl]
        y = o_h * lax.rsqrt(jnp.mean(o_h * o_h, axis=-1, keepdims=True) + EPS) * gain_ref[...] * _silu(z[:, sl])
        parts.append(y.astype(BF16))
    mix = _dot(jnp.concatenate(parts, axis=1), w_ref[...].astype(BF16))
    xn = x_ref[...] + gt_ref[0] * mix
    xo_ref[...] = xn
    h2_ref[...] = _modulate(xn, g2_ref[0], sh2_ref[0], sc2_ref[0]).astype(BF16)


def _even_output(lay, oa, o_f, o_b, proj, o_gain, w_out, x, normg, modtab, l):
    i_even = l // 2
    tm = lay.tm
    zcol = (3 * A_W + 3 * B_W) // B_W
    half = lambda: pl.BlockSpec((tm, B_W), lambda i: (i, 0))
    full = lambda: pl.BlockSpec((tm, D_MODEL), lambda i: (i, 0))
    return pl.pallas_call(
        _even_out_kernel,
        out_shape=(jax.ShapeDtypeStruct((lay.n, D_MODEL), F32),
                   jax.ShapeDtypeStruct((lay.n, D_MODEL), BF16)),
        grid=(lay.n // tm,),
        in_specs=[half(), half(), half(),
                  pl.BlockSpec((tm, B_W), lambda i: (i, zcol)),
                  pl.BlockSpec((None, 1, DV), lambda i: (i_even, 0, 0)),
                  pl.BlockSpec((None, D_MODEL, D_MODEL), lambda i: (i_even, 0, 0)),
                  full(),
                  lay.mod_spec(l, 2, tm),
                  _row_spec(l, 1), lay.mod_spec(l, 3, tm), lay.mod_spec(l, 4, tm)],
        out_specs=(full(), full()),
        compiler_params=_params(("arbitrary",)),
        name="even_output",
    )(oa, o_f, o_b, proj, o_gain, w_out, x, modtab, normg, modtab, modtab)


def _ffn_kernel(h_ref, wg_ref, wu_ref, wd_ref, x_ref, gt_ref, o_ref, acc_ref):
    j = pl.program_id(1)

    @pl.when(j == 0)
    def _():
        acc_ref[...] = jnp.zeros_like(acc_ref)

    h = h_ref[...]
    a = (_silu(_dot(h, wg_ref[...].astype(BF16))) * _dot(h, wu_ref[...].astype(BF16))).astype(BF16)
    acc_ref[...] += _dot(a, wd_ref[...].astype(BF16))

    @pl.when(j == pl.num_programs(1) - 1)
    def _():
        o_ref[...] = x_ref[...] + gt_ref[0] * acc_ref[...]


def _dense_ffn(lay, h2, x, ffn_gate, ffn_up, ffn_down, modtab, l):
    i_even = l // 2
    tm, tf = lay.tm, FF_TILE
    return pl.pallas_call(
        _ffn_kernel,
        out_shape=jax.ShapeDtypeStruct((lay.n, D_MODEL), F32),
        grid=(lay.n // tm, D_FF // tf),
        in_specs=[pl.BlockSpec((tm, D_MODEL), lambda i, j: (i, 0)),
                  pl.BlockSpec((None, D_MODEL, tf), lambda i, j: (i_even, 0, j)),
                  pl.BlockSpec((None, D_MODEL, tf), lambda i, j: (i_even, 0, j)),
                  pl.BlockSpec((None, tf, D_MODEL), lambda i, j: (i_even, j, 0)),
                  pl.BlockSpec((tm, D_MODEL), lambda i, j: (i, 0)),
                  lay.mod_spec(l, 5, tm)],
        out_specs=pl.BlockSpec((tm, D_MODEL), lambda i, j: (i, 0)),
        scratch_shapes=[pltpu.VMEM((tm, D_MODEL), F32)],
        compiler_params=_params(("arbitrary", "arbitrary")),
        name="dense_ffn",
    )(h2, ffn_gate, ffn_up, ffn_down, x, modtab)


def _pool_kernel(xp_ref, xc_ref, xn_ref, g1_ref, sh1_ref, sc1_ref, gt1_ref, pw_ref, ps_ref,
                 g2_ref, sh2_ref, sc2_ref, wr_ref, xo_ref, h2_ref, ridx_ref, rw_ref, *, nc, seq, dec_seq):
    i = pl.program_id(0)
    first = i * SEG
    is_lat = first >= nc
    t_len = jnp.where(is_lat, dec_seq, seq)
    off = jnp.where(is_lat, (first - nc) % dec_seq, first % seq)

    g1, sh1, sc1 = g1_ref[0], sh1_ref[0], sc1_ref[0]
    x = xc_ref[...]
    h_cur = _modulate(x, g1, sh1, sc1)
    h_cat = jnp.concatenate([_modulate(xp_ref[...], g1, sh1, sc1), h_cur,
                             _modulate(xn_ref[...], g1, sh1, sc1)], axis=0)

    t = off + lax.broadcasted_iota(jnp.int32, (SEG, 1), 0)
    p = off - SEG + lax.broadcasted_iota(jnp.int32, (1, 3 * SEG), 1)
    mixes = []
    for gi, w in enumerate(POOL_WINDOWS):
        sl = slice(gi * POOL_G, (gi + 1) * POOL_G)
        lo = jnp.maximum(t - w // 2, 0)
        hi = jnp.minimum(t + (w - w // 2), t_len)
        band = _mask01((p >= lo) & (p < hi))
        hg = h_cat[:, sl]
        hg_hi = hg.astype(BF16)
        hg_lo = (hg - hg_hi.astype(F32)).astype(BF16)
        window_sum = _dot(band, hg_hi) + _dot(band, hg_lo)
        y = (window_sum / (hi - lo).astype(F32) - h_cur[:, sl]).astype(BF16)
        mixes.append(_dot(y, pw_ref[gi].astype(BF16)))
    mix = jnp.concatenate(mixes, axis=1) * ps_ref[...]
    xn = x + gt1_ref[0] * mix
    xo_ref[...] = xn
    h2 = _modulate(xn, g2_ref[0], sh2_ref[0], sc2_ref[0])
    h2_ref[...] = h2.astype(BF16)

    logits = jnp.dot(h2, wr_ref[...], preferred_element_type=F32, precision=lax.Precision.HIGHEST)
    lane = lax.broadcasted_iota(jnp.int32, logits.shape, 1)
    lane_f = lane.astype(F32)
    lg = jnp.where(lane < N_EXP, logits, -jnp.inf)
    m1 = jnp.max(lg, axis=-1, keepdims=True)
    i1 = jnp.min(jnp.where(lg == m1, lane_f, float(LANES)), axis=-1, keepdims=True)
    lg2 = jnp.where(lane_f == i1, -jnp.inf, lg)
    m2 = jnp.max(lg2, axis=-1, keepdims=True)
    i2 = jnp.min(jnp.where(lg2 == m2, lane_f, float(LANES)), axis=-1, keepdims=True)
    e = jnp.exp(m2 - m1)
    w1 = 1.0 / (1.0 + e)
    w2 = e / (1.0 + e)
    ridx_ref[...] = jnp.where(lane == 0, i1, jnp.where(lane == 1, i2, 0.0)).astype(jnp.int32)
    rw_ref[...] = jnp.where(lane == 0, w1, jnp.where(lane == 1, w2, 0.0))


def _pool_and_route(lay, x, normg, modtab, pool_w, pool_scale, w_router, l):
    i_odd = l // 2
    nblk = lay.n // SEG
    blk = lambda imap: pl.BlockSpec((SEG, D_MODEL), imap)
    lane_blk = pl.BlockSpec((SEG, LANES), lambda i: (i, 0))
    return pl.pallas_call(
        functools.partial(_pool_kernel, nc=lay.nc, seq=lay.seq, dec_seq=lay.dec_seq),
        out_shape=(jax.ShapeDtypeStruct((lay.n, D_MODEL), F32),
                   jax.ShapeDtypeStruct((lay.n, D_MODEL), BF16),
                   jax.ShapeDtypeStruct((lay.n, LANES), jnp.int32),
                   jax.ShapeDtypeStruct((lay.n, LANES), F32)),
        grid=(nblk,),
        in_specs=[blk(lambda i: (jnp.maximum(i - 1, 0), 0)),
                  blk(lambda i: (i, 0)),
                  blk(lambda i: (jnp.minimum(i + 1, nblk - 1), 0)),
                  _row_spec(l, 0), lay.mod_spec(l, 0, SEG), lay.mod_spec(l, 1, SEG), lay.mod_spec(l, 2, SEG),
                  pl.BlockSpec((None, len(POOL_WINDOWS), POOL_G, POOL_G), lambda i: (i_odd, 0, 0, 0)),
                  pl.BlockSpec((None, 1, D_MODEL), lambda i: (i_odd, 0, 0)),
                  _row_spec(l, 1), lay.mod_spec(l, 3, SEG), lay.mod_spec(l, 4, SEG),
                  pl.BlockSpec((None, D_MODEL, LANES), lambda i: (i_odd, 0, 0))],
        out_specs=(blk(lambda i: (i, 0)), blk(lambda i: (i, 0)), lane_blk, lane_blk),
        compiler_params=_params(("arbitrary",)),
        name="pool_route",
    )(x, x, x, normg, modtab, modtab, modtab, pool_w, pool_scale, normg, modtab, modtab, w_router)


def _moe_kernel(ge_ref, gs_ref, gn_ref, xs_hbm, wg_ref, wu_ref, wd_ref, ys_hbm, xbuf, acc, sem_in, sem_out):
    g = pl.program_id(0)
    j = pl.program_id(1)
    nsub = gn_ref[g]
    row0 = gs_ref[g]

    def in_copy(s):
        rows = pl.ds(pl.multiple_of(row0 + s * MOE_SUB, MOE_SUB), MOE_SUB)
        return pltpu.make_async_copy(xs_hbm.at[rows], xbuf.at[pl.ds(pl.multiple_of(s * MOE_SUB, MOE_SUB), MOE_SUB)],
                                     sem_in)

    def out_copy(s):
        rows = pl.ds(pl.multiple_of(row0 + s * MOE_SUB, MOE_SUB), MOE_SUB)
        return pltpu.make_async_copy(acc.at[pl.ds(pl.multiple_of(s * MOE_SUB, MOE_SUB), MOE_SUB)], ys_hbm.at[rows],
                                     sem_out)

    def for_each_sub(fn):
        def body(s, carry):
            fn(s)
            return carry
        lax.fori_loop(0, nsub, body, 0)

    @pl.when(j == 0)
    def _():
        for_each_sub(lambda s: in_copy(s).start())
        for_each_sub(lambda s: in_copy(s).wait())

    wg = wg_ref[...].astype(BF16)
    wu = wu_ref[...].astype(BF16)
    wd = wd_ref[...].astype(BF16)

    def sub_tile(s):
        rows = pl.ds(pl.multiple_of(s * MOE_SUB, MOE_SUB), MOE_SUB)
        xs = xbuf[rows, :]
        a = (_silu(_dot(xs, wg)) * _dot(xs, wu)).astype(BF16)
        contrib = _dot(a, wd)

        @pl.when(j == 0)
        def _():
            acc[rows, :] = contrib

        @pl.when(j > 0)
        def _():
            acc[rows, :] += contrib

    for_each_sub(sub_tile)

    @pl.when(j == pl.num_programs(1) - 1)
    def _():
        for_each_sub(lambda s: out_copy(s).start())
        for_each_sub(lambda s: out_copy(s).wait())


def _moe_plan(ridx, n_tok):
    n_pairs = 2 * n_tok
    p_max = n_pairs + N_EXP * MOE_SUB
    group_rows = MOE_SUB * MOE_GROUP
    g_max = -(-n_pairs // group_rows) + N_EXP
    e_flat = ridx[:, :2].reshape(n_pairs)
    onehot = (e_flat[:, None] == jnp.arange(N_EXP, dtype=jnp.int32)[None, :]).astype(jnp.int32)
    rank = jnp.sum((jnp.cumsum(onehot, axis=0) - onehot) * onehot, axis=1)
    counts = jnp.sum(onehot, axis=0)
    nsub_e = (counts + MOE_SUB - 1) // MOE_SUB
    region = nsub_e * MOE_SUB
    start_e = jnp.cumsum(region) - region
    dest = start_e[e_flat] + rank
    src_tok = jnp.zeros((p_max,), jnp.int32).at[dest].set(jnp.arange(n_pairs, dtype=jnp.int32) // 2)

    ngrp_e = (nsub_e + MOE_GROUP - 1) // MOE_GROUP
    gend = jnp.cumsum(ngrp_e)
    gstart = gend - ngrp_e
    total = gend[-1]
    gid = jnp.arange(g_max, dtype=jnp.int32)
    ge = jnp.minimum(jnp.searchsorted(gend, gid, side="right").astype(jnp.int32), N_EXP - 1)
    kk = gid - gstart[ge]
    live = gid < total
    last_e = jnp.minimum(jnp.searchsorted(gend, total - 1, side="right").astype(jnp.int32), N_EXP - 1)
    g_expert = jnp.where(live, ge, last_e)
    g_row = jnp.where(live, start_e[ge] + kk * group_rows, 0)
    g_nsub = jnp.where(live, jnp.clip(nsub_e[ge] - kk * MOE_GROUP, 0, MOE_GROUP), 0)
    return src_tok, dest.reshape(n_tok, 2), g_expert.astype(jnp.int32), g_row.astype(jnp.int32), g_nsub.astype(jnp.int32)


def _moe_experts(xs, g_expert, g_row, g_nsub, moe_gate, moe_up, moe_down, i_odd):
    p_max = xs.shape[0]
    g_max = g_expert.shape[0]
    tf = FF_TILE
    grid_spec = pltpu.PrefetchScalarGridSpec(
        num_scalar_prefetch=3,
        grid=(g_max, D_FF // tf),
        in_specs=[pl.BlockSpec(memory_space=pl.ANY),
                  pl.BlockSpec((None, None, D_MODEL, tf), lambda g, j, ge, gs, gn: (i_odd, ge[g], 0, j)),
                  pl.BlockSpec((None, None, D_MODEL, tf), lambda g, j, ge, gs, gn: (i_odd, ge[g], 0, j)),
                  pl.BlockSpec((None, None, tf, D_MODEL), lambda g, j, ge, gs, gn: (i_odd, ge[g], j, 0))],
        out_specs=pl.BlockSpec(memory_space=pl.ANY),
        scratch_shapes=[pltpu.VMEM((MOE_SUB * MOE_GROUP, D_MODEL), BF16),
                        pltpu.VMEM((MOE_SUB * MOE_GROUP, D_MODEL), F32),
                        pltpu.SemaphoreType.DMA, pltpu.SemaphoreType.DMA],
    )
    return pl.pallas_call(
        _moe_kernel,
        out_shape=jax.ShapeDtypeStruct((p_max, D_MODEL), F32),
        grid_spec=grid_spec,
        compiler_params=_params(("arbitrary", "arbitrary")),
        name="moe_experts",
    )(g_expert, g_row, g_nsub, xs, moe_gate, moe_up, moe_down)


def _combine_kernel(x_ref, ya_ref, yb_ref, rw_ref, gt_ref, fg_ref, o_ref, *, final):
    rw = rw_ref[...]
    y = rw[:, 0:1] * ya_ref[...] + rw[:, 1:2] * yb_ref[...]
    xn = x_ref[...] + gt_ref[0] * y
    if final:
        xn = xn * lax.rsqrt(jnp.mean(xn * xn, axis=-1, keepdims=True) + EPS) * fg_ref[...]
    o_ref[...] = xn


def _moe_combine(lay, x, ya, yb, rw, modtab, final_g, l, final):
    tm = lay.tm
    full = lambda: pl.BlockSpec((tm, D_MODEL), lambda i: (i, 0))
    return pl.pallas_call(
        functools.partial(_combine_kernel, final=final),
        out_shape=jax.ShapeDtypeStruct((lay.n, D_MODEL), F32),
        grid=(lay.n // tm,),
        in_specs=[full(), full(), full(),
                  pl.BlockSpec((tm, LANES), lambda i: (i, 0)),
                  lay.mod_spec(l, 5, tm),
                  pl.BlockSpec((1, D_MODEL), lambda i: (0, 0))],
        out_specs=full(),
        compiler_params=_params(("arbitrary",)),
        name="moe_combine",
    )(x, ya, yb, rw, modtab, final_g)


def kernel(x_prompt, x_sample, cache_k_ctx, cache_v_ctx, state_delta, c, c_ctx, w_mod, b_mod, norm_g, final_g,
           w_in, conv_w, a_log, dt_bias, rpb, o_gain, w_out, ffn_gate, ffn_up, ffn_down, pool_w, pool_scale,
           w_router, moe_gate, moe_up, moe_down):
    batch, seq, d = x_prompt.shape
    dec_batch, dec_seq, _ = x_sample.shape
    depth = w_mod.shape[0]
    n_even = w_in.shape[0]
    past = cache_k_ctx.shape[2]
    assert d == D_MODEL and depth % 2 == 0
    lay = _Layout(batch, seq, dec_batch, dec_seq)

    cond = jnp.zeros((lay.mod_rows, d), F32).at[0].set(c_ctx).at[1:1 + dec_batch].set(c)
    mod = _adaln(cond, w_mod, b_mod)
    modtab = mod.reshape(depth, lay.mod_rows, 6, d).transpose(0, 2, 1, 3).reshape(depth * 6 * lay.mod_rows, 1, d)
    normg = norm_g.reshape(depth * 2, 1, d)

    x = jnp.concatenate([x_prompt.reshape(lay.nc, d), x_sample.reshape(lay.nl, d)], axis=0)
    cache_k = cache_k_ctx.reshape(dec_batch, n_even, past, A_W)
    cache_v = cache_v_ctx.reshape(dec_batch, n_even, past, A_W)
    rows = dec_seq // GRID_W
    kh = min(NA_KH, rows)
    n_main = 3 * A_W + 4 * B_W
    pad128 = lambda a: jnp.pad(a.reshape(1, -1), ((0, 0), (0, LANES - a.size)))
    w_router_p = jnp.pad(w_router, ((0, 0), (0, 0), (0, LANES - N_EXP)))

    k_list, v_list, s_list = [], [], []
    for l in range(depth):
        i = l // 2
        if l % 2 == 0:
            w_ab = jnp.pad(w_in[i][:, n_main:], ((0, 0), (0, LANES - 4 * H_B)))
            proj, ab = _even_project(lay, x, normg, modtab, w_in, w_ab, l)
            bias = _na_bias_table(rpb[i], rows, kh)
            oa = _attention(lay, proj, cache_k, cache_v, bias, i)
            gates, gates_t = _delta_gates(ab, pad128(a_log[i]), pad128(dt_bias[i]))
            qc, kc, vc = _delta_prep(proj, conv_w, i, 0, batch, seq)
            of_c, ob_c, s_fin = _delta_net(qc, kc, vc, gates, gates_t, 0, None, batch, seq)
            ql, kl, vl = _delta_prep(proj, conv_w, i, lay.nc // dec_seq, dec_batch, dec_seq)
            of_l, ob_l, _ = _delta_net(ql, kl, vl, gates, gates_t, lay.nc // SEG, state_delta[:, i],
                                       dec_batch, dec_seq)
            o_f = jnp.concatenate([of_c, of_l], axis=0)
            o_b = jnp.concatenate([ob_c, ob_l], axis=0)
            x, h2 = _even_output(lay, oa, o_f, o_b, proj, o_gain.reshape(n_even, 1, DV), w_out, x, normg, modtab, l)
            x = _dense_ffn(lay, h2, x, ffn_gate, ffn_up, ffn_down, modtab, l)
            k_list.append(proj[:lay.nc, A_W:2 * A_W].reshape(batch, seq, H_A, DH_A))
            v_list.append(proj[:lay.nc, 2 * A_W:3 * A_W].reshape(batch, seq, H_A, DH_A))
            s_list.append(s_fin)
        else:
            x, h2, ridx, rw = _pool_and_route(lay, x, normg, modtab, pool_w,
                                              pool_scale.reshape(-1, 1, d), w_router_p, l)
            src_tok, dest, g_expert, g_row, g_nsub = _moe_plan(ridx, lay.n)
            xs = jnp.take(h2, src_tok, axis=0)
            ys = _moe_experts(xs, g_expert, g_row, g_nsub, moe_gate, moe_up, moe_down, i)
            ya = jnp.take(ys, dest[:, 0], axis=0)
            yb = jnp.take(ys, dest[:, 1], axis=0)
            x = _moe_combine(lay, x, ya, yb, rw, modtab, final_g.reshape(1, d), l, final=(l == depth - 1))
    y_prompt = x[:lay.nc].reshape(batch, seq, d)
    y_sample = x[lay.nc:].reshape(dec_batch, dec_seq, d)
    return (y_prompt, y_sample, jnp.stack(k_list, axis=1), jnp.stack(v_list, axis=1), jnp.stack(s_list, axis=1))
```

```python
import functools
import math

import numpy as np
import jax
import jax.numpy as jnp
from jax import lax
from jax.experimental import pallas as pl
from jax.experimental.pallas import tpu as pltpu

F32 = jnp.float32
BF16 = jnp.bfloat16

D_MODEL = 1024
GRID_W = 64
DH_A = 64
H_A = 8
A_W = H_A * DH_A
NA_KH = 8
NA_KW = 16
DK = 128
DV = 128
H_B = 4
B_W = H_B * DK
CONV_K = 3
CHUNK = 64
POOL_WINDOWS = (2, 4, 8, 16)
POOL_G = D_MODEL // len(POOL_WINDOWS)
D_FF = 7 * D_MODEL // 2
N_EXP = 8
EPS = 1e-6
NEG_INF = -1e30

LANES = 128
SEG = 256
FF_TILE = 512
MOE_SUB = 256
MOE_GROUP = 8
VMEM_LIMIT = 56 * 2 ** 20


def _params(sem, vmem=VMEM_LIMIT):
    return pltpu.CompilerParams(dimension_semantics=sem, vmem_limit_bytes=vmem)


def _silu(x):
    return x * jax.nn.sigmoid(x)


def _dot(a, b):
    return jnp.dot(a, b, preferred_element_type=F32)


def _dot_nt(a, b):
    return lax.dot_general(a, b, (((1,), (1,)), ((), ())), preferred_element_type=F32)


def _dot_f32x3(a01, x):
    x1 = x.astype(BF16)
    r1 = x - x1.astype(F32)
    x2 = r1.astype(BF16)
    x3 = (r1 - x2.astype(F32)).astype(BF16)
    return _dot(a01, x1) + _dot(a01, x2) + _dot(a01, x3)


def _mask01(mask):
    return jnp.where(mask, 1.0, 0.0).astype(BF16)


def _modulate(x, g, shift, scale):
    y = x * lax.rsqrt(jnp.mean(x * x, axis=-1, keepdims=True) + EPS)
    return (y * g) * (1.0 + scale) + shift


def _adaln_kernel(cond_ref, w_ref, b_ref, o_ref):
    s = _silu(cond_ref[...]).astype(BF16)
    o_ref[0] = _dot(s, w_ref[0].astype(BF16)) + b_ref[0]


def _adaln(cond, w_mod, b_mod):
    depth, d, six_d = w_mod.shape
    r = cond.shape[0]
    tn = six_d // 4
    return pl.pallas_call(
        _adaln_kernel,
        out_shape=jax.ShapeDtypeStruct((depth, r, six_d), F32),
        grid=(depth, six_d // tn),
        in_specs=[
            pl.BlockSpec((r, d), lambda l, j: (0, 0)),
            pl.BlockSpec((1, d, tn), lambda l, j: (l, 0, j)),
            pl.BlockSpec((1, 1, tn), lambda l, j: (l, 0, j)),
        ],
        out_specs=pl.BlockSpec((1, r, tn), lambda l, j: (l, 0, j)),
        compiler_params=_params(("arbitrary", "arbitrary")),
        name="adaln",
    )(cond, w_mod, b_mod.reshape(depth, 1, six_d))


class _Layout:
    def __init__(self, batch, seq, dec_batch, dec_seq):
        self.batch, self.seq, self.dec_batch, self.dec_seq = batch, seq, dec_batch, dec_seq
        self.nc = batch * seq
        self.nl = dec_batch * dec_seq
        self.n = self.nc + self.nl
        self.mod_rows = -(-(1 + dec_batch) // 8) * 8
        assert seq % SEG == 0 and dec_seq % SEG == 0 and dec_seq % GRID_W == 0
        self.tm = math.gcd(self.nc, 1024)
        assert self.tm % SEG == 0 and dec_seq % self.tm == 0 and self.nc % dec_seq == 0

    def mod_row(self, first_row):
        return jnp.maximum((first_row - self.nc) // self.dec_seq + 1, 0)

    def mod_spec(self, l, k, rows_per_step):
        base = (l * 6 + k) * self.mod_rows
        return pl.BlockSpec((1, 1, D_MODEL),
                            lambda i, *_: (base + self.mod_row(i * rows_per_step), 0, 0))


def _row_spec(l, k):
    return pl.BlockSpec((1, 1, D_MODEL), lambda i, *_: (l * 2 + k, 0, 0))


def _proj_kernel(x_ref, g_ref, sh_ref, sc_ref, w_ref, wab_ref, proj_ref, ab_ref, h_scr):
    @pl.when(pl.program_id(1) == 0)
    def _():
        h = _modulate(x_ref[...], g_ref[0], sh_ref[0], sc_ref[0]).astype(BF16)
        h_scr[...] = h
        ab_ref[...] = _dot(h, wab_ref[...].astype(BF16))

    proj_ref[...] = _dot(h_scr[...], w_ref[...].astype(BF16))


def _even_project(lay, x, normg, modtab, w_in, w_ab, l):
    i_even = l // 2
    tm, tn = lay.tm, 512
    ncols = 3 * A_W + 4 * B_W
    return pl.pallas_call(
        _proj_kernel,
        out_shape=(jax.ShapeDtypeStruct((lay.n, ncols), F32),
                   jax.ShapeDtypeStruct((lay.n, LANES), F32)),
        grid=(lay.n // tm, ncols // tn),
        in_specs=[
            pl.BlockSpec((tm, D_MODEL), lambda i, j: (i, 0)),
            _row_spec(l, 0),
            lay.mod_spec(l, 0, tm),
            lay.mod_spec(l, 1, tm),
            pl.BlockSpec((None, D_MODEL, tn), lambda i, j: (i_even, 0, j)),
            pl.BlockSpec((D_MODEL, LANES), lambda i, j: (0, 0)),
        ],
        out_specs=(pl.BlockSpec((tm, tn), lambda i, j: (i, j)),
                   pl.BlockSpec((tm, LANES), lambda i, j: (i, 0))),
        scratch_shapes=[pltpu.VMEM((tm, D_MODEL), BF16)],
        compiler_params=_params(("arbitrary", "arbitrary")),
        name="even_project",
    )(x, normg, modtab, modtab, w_in, w_ab)


def _softmax_av(scores, values):
    m = functools.reduce(jnp.maximum, [jnp.max(s, axis=-1, keepdims=True) for s in scores])
    ps = [jnp.exp(s - m) for s in scores]
    denom = functools.reduce(jnp.add, [jnp.sum(p, axis=-1, keepdims=True) for p in ps])
    acc = functools.reduce(jnp.add, [_dot(p.astype(BF16), v) for p, v in zip(ps, values)])
    return acc / denom


def _ctx_attn_kernel(q_ref, k_ref, v_ref, o_ref):
    q, k, v = q_ref[...], k_ref[...], v_ref[...]
    scale = DH_A ** -0.5
    outs = []
    for h in range(H_A):
        sl = slice(h * DH_A, (h + 1) * DH_A)
        s = _dot_nt(q[:, sl].astype(BF16), k[:, sl].astype(BF16)) * scale
        outs.append(_softmax_av([s], [v[:, sl].astype(BF16)]))
    o_ref[...] = jnp.concatenate(outs, axis=1)


def _na_attn_kernel(q_ref, k_ref, v_ref, kc_ref, vc_ref, bias_ref, o_ref, *, rows, kh):
    r = pl.program_id(1)
    row_lo = jnp.clip(r - kh // 2, 0, rows - kh)
    start = pl.multiple_of(row_lo * GRID_W, GRID_W)
    q = q_ref[...]
    kl = k_ref[pl.ds(start, kh * GRID_W), :]
    vl = v_ref[pl.ds(start, kh * GRID_W), :]
    kc, vc = kc_ref[...], vc_ref[...]
    scale = DH_A ** -0.5
    outs = []
    for h in range(H_A):
        sl = slice(h * DH_A, (h + 1) * DH_A)
        qh = q[:, sl].astype(BF16)
        s_loc = _dot_nt(qh, kl[:, sl].astype(BF16)) * scale + bias_ref[h, 0]
        s_ctx = _dot_nt(qh, kc[:, sl].astype(BF16)) * scale
        outs.append(_softmax_av([s_loc, s_ctx], [vl[:, sl].astype(BF16), vc[:, sl].astype(BF16)]))
    o_ref[...] = jnp.concatenate(outs, axis=1)


def _na_bias_table(rpb, rows, kh):
    r = np.arange(rows)
    row_idx = np.clip(r - kh // 2, 0, rows - kh)[:, None] + np.arange(kh)[None, :]
    dr = row_idx - r[:, None] + (NA_KH - 1)
    qcol = np.arange(GRID_W)
    kcol = np.arange(GRID_W)
    col_lo = np.clip(qcol - NA_KW // 2, 0, GRID_W - NA_KW)
    valid = (kcol[None, :] >= col_lo[:, None]) & (kcol[None, :] < col_lo[:, None] + NA_KW)
    dc = np.clip(kcol[None, :] - qcol[:, None], 1 - NA_KW, NA_KW - 1) + (NA_KW - 1)
    onehot = (dc[None, :, :] == np.arange(2 * NA_KW - 1)[:, None, None]).astype(np.float32)
    picked = jnp.einsum("hrjc,cqk->hrqjk", rpb[:, dr].astype(F32), onehot, precision=lax.Precision.HIGHEST)
    bias = jnp.where(valid[None, None, :, None, :], picked, NEG_INF)
    return bias.reshape(H_A, rows, GRID_W, kh * GRID_W)


def _attention(lay, proj, cache_k, cache_v, bias, i_even):
    ctx = pl.pallas_call(
        _ctx_attn_kernel,
        out_shape=jax.ShapeDtypeStruct((lay.nc, A_W), F32),
        grid=(lay.batch,),
        in_specs=[pl.BlockSpec((lay.seq, A_W), lambda b: (b, 0)),
                  pl.BlockSpec((lay.seq, A_W), lambda b: (b, 1)),
                  pl.BlockSpec((lay.seq, A_W), lambda b: (b, 2))],
        out_specs=pl.BlockSpec((lay.seq, A_W), lambda b: (b, 0)),
        compiler_params=_params(("arbitrary",)),
        name="ctx_attention",
    )(proj, proj, proj)

    rows = lay.dec_seq // GRID_W
    kh = min(NA_KH, rows)
    past = cache_k.shape[2]
    q0 = lay.nc // GRID_W
    b0 = lay.nc // lay.dec_seq
    lat = pl.pallas_call(
        functools.partial(_na_attn_kernel, rows=rows, kh=kh),
        out_shape=jax.ShapeDtypeStruct((lay.nl, A_W), F32),
        grid=(lay.dec_batch, rows),
        in_specs=[pl.BlockSpec((GRID_W, A_W), lambda b, r: (q0 + b * rows + r, 0)),
                  pl.BlockSpec((lay.dec_seq, A_W), lambda b, r: (b0 + b, 1)),
                  pl.BlockSpec((lay.dec_seq, A_W), lambda b, r: (b0 + b, 2)),
                  pl.BlockSpec((None, None, past, A_W), lambda b, r: (b, i_even, 0, 0)),
                  pl.BlockSpec((None, None, past, A_W), lambda b, r: (b, i_even, 0, 0)),
                  pl.BlockSpec((H_A, 1, GRID_W, kh * GRID_W), lambda b, r: (0, r, 0, 0))],
        out_specs=pl.BlockSpec((GRID_W, A_W), lambda b, r: (b * rows + r, 0)),
        compiler_params=_params(("arbitrary", "arbitrary")),
        name="na_attention",
    )(proj, proj, proj, cache_k, cache_v, bias)
    return jnp.concatenate([ctx, lat], axis=0)


def _delta_prep_kernel(xq_ref, xk_ref, xv_ref, wq_ref, wk_ref, wv_ref, q_ref, k_ref, v_ref):
    t = xq_ref.shape[0]
    row = lax.broadcasted_iota(jnp.int32, (t, 1), 0)

    def conv_silu(x, w):
        prev = jnp.where(row == 0, 0.0, pltpu.roll(x, 1, 0))
        nxt = jnp.where(row == t - 1, 0.0, pltpu.roll(x, t - 1, 0))
        return _silu(prev * w[0:1] + x * w[1:2] + nxt * w[2:3])

    def l2norm(x):
        return x * lax.rsqrt(jnp.sum(x * x, axis=-1, keepdims=True) + EPS)

    q_ref[...] = l2norm(conv_silu(xq_ref[...], wq_ref[...])) * (DK ** -0.5)
    k_ref[...] = l2norm(conv_silu(xk_ref[...], wk_ref[...]))
    v_ref[...] = conv_silu(xv_ref[...], wv_ref[...])


def _delta_prep(proj, conv_w, i_even, first_seq, n_seq, t):
    c0 = 3 * A_W // LANES
    xspec = lambda part: pl.BlockSpec((t, LANES), lambda b, h: (first_seq + b, c0 + part * H_B + h))
    wspec = lambda part: pl.BlockSpec((None, CONV_K, LANES), lambda b, h: (i_even, 0, part * H_B + h))
    ospec = pl.BlockSpec((t, LANES), lambda b, h: (b, h))
    shape = jax.ShapeDtypeStruct((n_seq * t, B_W), F32)
    return pl.pallas_call(
        _delta_prep_kernel,
        out_shape=(shape, shape, shape),
        grid=(n_seq, H_B),
        in_specs=[xspec(0), xspec(1), xspec(2), wspec(0), wspec(1), wspec(2)],
        out_specs=(ospec, ospec, ospec),
        compiler_params=_params(("arbitrary", "arbitrary")),
        name="delta_prep",
    )(proj, proj, proj, conv_w, conv_w, conv_w)


GATE_GC, GATE_BETA, GATE_GT = 0, 2 * H_B, 4 * H_B


def _delta_gates_kernel(ab_ref, alog_ref, dtb_ref, g_ref, gt_ref):
    n = SEG
    ab = ab_ref[...]
    lane = lax.broadcasted_iota(jnp.int32, (1, LANES), 1)
    ri = lax.broadcasted_iota(jnp.int32, (n, n), 0)
    ci = lax.broadcasted_iota(jnp.int32, (n, n), 1)
    shift = int(math.log2(CHUNK))
    same = jnp.right_shift(ri, shift) == jnp.right_shift(ci, shift)
    g_all = -jnp.exp(alog_ref[...]) * jax.nn.softplus(ab + dtb_ref[...])
    prefix = _dot_f32x3(_mask01(same & (ci <= ri)), g_all)
    total = _dot_f32x3(_mask01(same), g_all)
    gc = jnp.where(lane < H_B, prefix, total - prefix + g_all)
    table = jnp.where(lane < GATE_BETA, gc,
                      jnp.where(lane < GATE_GT, jax.nn.sigmoid(ab), pltpu.roll(total, GATE_GT, 1)))
    g_ref[...] = table
    gt_ref[...] = table.T


def _delta_gates(ab, alog, dtb):
    n = ab.shape[0]
    row = pl.BlockSpec((1, LANES), lambda i: (0, 0))
    return pl.pallas_call(
        _delta_gates_kernel,
        out_shape=(jax.ShapeDtypeStruct((n, LANES), F32), jax.ShapeDtypeStruct((LANES, n), F32)),
        grid=(n // SEG,),
        in_specs=[pl.BlockSpec((SEG, LANES), lambda i: (i, 0)), row, row],
        out_specs=(pl.BlockSpec((SEG, LANES), lambda i: (i, 0)), pl.BlockSpec((LANES, SEG), lambda i: (0, i))),
        compiler_params=_params(("arbitrary",)),
        name="delta_gates",
    )(ab, alog, dtb)


def _delta_masks(d):
    n = SEG
    ri = lax.broadcasted_iota(jnp.int32, (n, n), 0)
    ci = lax.broadcasted_iota(jnp.int32, (n, n), 1)
    shift = int(math.log2(CHUNK))
    same = jnp.right_shift(ri, shift) == jnp.right_shift(ci, shift)
    incl = same & ((ci <= ri) if d == 0 else (ci >= ri))
    strict = same & ((ci < ri) if d == 0 else (ci > ri))
    levels = []
    for level in range(shift):
        bi, bj = jnp.right_shift(ri, level), jnp.right_shift(ci, level)
        siblings = jnp.right_shift(bi, 1) == jnp.right_shift(bj, 1)
        levels.append(siblings & ((bi > bj) if d == 0 else (bi < bj)))
    return incl, strict, jnp.where(ri == ci, 1.0, 0.0), levels


def _delta_chains(chains):
    n = SEG
    n_chunks = n // CHUNK
    each = lambda fn, *lists: [fn(*args) for args in zip(*lists)]
    ds, masks, qs, ks, vs, betas, gcs, gts, gc_rows, gt_rows, states = (list(t) for t in zip(*chains))
    incls, stricts = [m[0] for m in masks], [m[1] for m in masks]
    n_levels = len(masks[0][3])

    decays = each(lambda m, gc, gr: jnp.where(m, jnp.exp(jnp.where(m, gc - gr, 0.0)), 0.0), incls, gcs, gc_rows)
    kbs = each(lambda k, b: k * b, ks, betas)
    k16s = [k.astype(BF16) for k in ks]
    grams = each(lambda kb, k16: _dot_nt(kb.astype(BF16), k16), kbs, k16s)
    lmats = each(lambda m, g, dec: jnp.where(m, g * dec, 0.0), stricts, grams, decays)

    xs = [m[2] for m in masks]
    for level in range(n_levels):
        x16s = [x.astype(BF16) for x in xs]
        cs = each(lambda m, lm: jnp.where(m[3][level], lm, 0.0).astype(BF16), masks, lmats)
        xcs = each(lambda x16, c: _dot(x16, c).astype(BF16), x16s, cs)
        xs = each(lambda x, xc, x16: x - _dot(xc, x16), xs, xcs, x16s)

    egs = [jnp.exp(gc) for gc in gcs]
    rhss = each(lambda v, b, kb, eg: jnp.concatenate([v * b, kb * eg], axis=1).astype(BF16), vs, betas, kbs, egs)
    uws = each(lambda x, rhs: _dot(x.astype(BF16), rhs), xs, rhss)
    qks = each(lambda q, k16: _dot_nt(q.astype(BF16), k16), qs, k16s)
    attns = each(lambda m, qk, dec: jnp.where(m, qk * dec, 0.0).astype(BF16), incls, qks, decays)
    qgs = each(lambda q, eg: (q * eg).astype(BF16), qs, egs)
    kd_ts = each(lambda k, gt_r, gc_r: (k.T * jnp.exp(gt_r - gc_r)).astype(BF16), ks, gt_rows, gc_rows)
    gls = [jnp.exp(gt) for gt in gts]

    v_new = [[None] * n_chunks for _ in chains]
    o_state = [[None] * n_chunks for _ in chains]
    for step in range(n_chunks):
        for i, d in enumerate(ds):
            c = step if d == 0 else n_chunks - 1 - step
            rs = slice(c * CHUNK, (c + 1) * CHUNK)
            s16 = states[i].astype(BF16)
            vn = uws[i][rs, :DV] - _dot(uws[i][rs, DV:].astype(BF16), s16)
            o_state[i][c] = _dot(qgs[i][rs], s16)
            v_new[i][c] = vn
            pieces = [jnp.zeros((c * CHUNK, DV), BF16), vn.astype(BF16),
                      jnp.zeros((n - (c + 1) * CHUNK, DV), BF16)]
            padded = jnp.concatenate([p for p in pieces if p.shape[0]], axis=0)
            states[i] = states[i] * gls[i][c * CHUNK:c * CHUNK + 1, :] + _dot(kd_ts[i], padded)
    outs = each(lambda os, a, vn: jnp.concatenate(os, axis=0) + _dot(a, jnp.concatenate(vn, axis=0).astype(BF16)),
                o_state, attns, v_new)
    return list(zip(outs, states))


def _delta_kernel(*refs, has_s0):
    dir_refs = (refs[0:5], refs[5:10])
    refs = refs[10:]
    if has_s0:
        s0_ref, refs = refs[0], refs[1:]
    of_ref, ob_ref, sfin_ref, s_scr = refs
    seg = pl.program_id(1)

    @pl.when(seg == 0)
    def _():
        if has_s0:
            s_scr[...] = s0_ref[...]
        else:
            s_scr[...] = jnp.zeros_like(s_scr)

    chains = []
    for d, (q_ref, k_ref, v_ref, g_ref, gt_ref) in enumerate(dir_refs):
        masks = _delta_masks(d)
        q, k, v, gates, gates_t = q_ref[...], k_ref[...], v_ref[...], g_ref[...], gt_ref[...]
        for h in range(H_B):
            sl = slice(h * DK, (h + 1) * DK)
            col = d * H_B + h
            pick = lambda base: gates[:, base + col:base + col + 1]
            pick_t = lambda base: gates_t[base + col:base + col + 1, :]
            chains.append((d, masks, q[:, sl], k[:, sl], v[:, sl], pick(GATE_BETA), pick(GATE_GC),
                           pick(GATE_GT), pick_t(GATE_GC), pick_t(GATE_GT), s_scr[d, h]))
    results = _delta_chains(chains)
    for d, o_ref in enumerate((of_ref, ob_ref)):
        o_ref[...] = jnp.concatenate([results[d * H_B + h][0] for h in range(H_B)], axis=1)
        for h in range(H_B):
            s_scr[d, h] = results[d * H_B + h][1]

    @pl.when(seg == pl.num_programs(1) - 1)
    def _():
        sfin_ref[...] = s_scr[...]


def _delta_net(q, k, v, gates, gates_t, first_seg, s0, n_seq, t):
    nseg = t // SEG
    fwd = lambda b, s: b * nseg + s
    bwd = lambda b, s: b * nseg + nseg - 1 - s
    state = pl.BlockSpec((None, 2, H_B, DK, DV), lambda b, s: (b, 0, 0, 0, 0))
    in_specs, args = [], []
    for seg_of in (fwd, bwd):
        in_specs += [pl.BlockSpec((SEG, B_W), lambda b, s, f=seg_of: (f(b, s), 0))] * 3
        in_specs += [pl.BlockSpec((SEG, LANES), lambda b, s, f=seg_of: (first_seg + f(b, s), 0)),
                     pl.BlockSpec((LANES, SEG), lambda b, s, f=seg_of: (0, first_seg + f(b, s)))]
        args += [q, k, v, gates, gates_t]
    if s0 is not None:
        in_specs.append(state)
        args.append(s0)
    oshape = jax.ShapeDtypeStruct((n_seq * t, B_W), F32)
    return pl.pallas_call(
        functools.partial(_delta_kernel, has_s0=s0 is not None),
        out_shape=(oshape, oshape, jax.ShapeDtypeStruct((n_seq, 2, H_B, DK, DV), F32)),
        grid=(n_seq, nseg),
        in_specs=in_specs,
        out_specs=(pl.BlockSpec((SEG, B_W), lambda b, s: (fwd(b, s), 0)),
                   pl.BlockSpec((SEG, B_W), lambda b, s: (bwd(b, s), 0)), state),
        scratch_shapes=[pltpu.VMEM((2, H_B, DK, DV), F32)],
        compiler_params=_params(("arbitrary", "arbitrary")),
        name="delta_net",
    )(*args)


def _even_out_kernel(oa_ref, of_ref, ob_ref, z_ref, gain_ref, w_ref, x_ref, gt_ref,
                     g2_ref, sh2_ref, sc2_ref, xo_ref, h2_ref):
    ob = of_ref[...] + ob_ref[...]
    z = z_ref[...]
    parts = [oa_ref[...].astype(BF16)]
    for h in range(H_B):
        sl = slice(h * DV, (h + 1) * DV)
        o_h = ob[:, sl]
        y = o_h * lax.rsqrt(jnp.mean(o_h * o_h, axis=-1, keepdims=True) + EPS) * gain_ref[...] * _silu(z[:, sl])
        parts.append(y.astype(BF16))
    mix = _dot(jnp.concatenate(parts, axis=1), w_ref[...].astype(BF16))
    xn = x_ref[...] + gt_ref[0] * mix
    xo_ref[...] = xn
    h2_ref[...] = _modulate(xn, g2_ref[0], sh2_ref[0], sc2_ref[0]).astype(BF16)


def _even_output(lay, oa, o_f, o_b, proj, o_gain, w_out, x, normg, modtab, l):
    i_even = l // 2
    tm = lay.tm
    zcol = (3 * A_W + 3 * B_W) // B_W
    half = lambda: pl.BlockSpec((tm, B_W), lambda i: (i, 0))
    full = lambda: pl.BlockSpec((tm, D_MODEL), lambda i: (i, 0))
    return pl.pallas_call(
        _even_out_kernel,
        out_shape=(jax.ShapeDtypeStruct((lay.n, D_MODEL), F32),
                   jax.ShapeDtypeStruct((lay.n, D_MODEL), BF16)),
        grid=(lay.n // tm,),
        in_specs=[half(), half(), half(),
                  pl.BlockSpec((tm, B_W), lambda i: (i, zcol)),
                  pl.BlockSpec((None, 1, DV), lambda i: (i_even, 0, 0)),
                  pl.BlockSpec((None, D_MODEL, D_MODEL), lambda i: (i_even, 0, 0)),
                  full(),
                  lay.mod_spec(l, 2, tm),
                  _row_spec(l, 1), lay.mod_spec(l, 3, tm), lay.mod_spec(l, 4, tm)],
        out_specs=(full(), full()),
        compiler_params=_params(("arbitrary",)),
        name="even_output",
    )(oa, o_f, o_b, proj, o_gain, w_out, x, modtab, normg, modtab, modtab)


def _ffn_kernel(h_ref, wg_ref, wu_ref, wd_ref, x_ref, gt_ref, o_ref, acc_ref):
    j = pl.program_id(1)

    @pl.when(j == 0)
    def _():
        acc_ref[...] = jnp.zeros_like(acc_ref)

    h = h_ref[...]
    a = (_silu(_dot(h, wg_ref[...].astype(BF16))) * _dot(h, wu_ref[...].astype(BF16))).astype(BF16)
    acc_ref[...] += _dot(a, wd_ref[...].astype(BF16))

    @pl.when(j == pl.num_programs(1) - 1)
    def _():
        o_ref[...] = x_ref[...] + gt_ref[0] * acc_ref[...]


def _dense_ffn(lay, h2, x, ffn_gate, ffn_up, ffn_down, modtab, l):
    i_even = l // 2
    tm, tf = lay.tm, FF_TILE
    return pl.pallas_call(
        _ffn_kernel,
        out_shape=jax.ShapeDtypeStruct((lay.n, D_MODEL), F32),
        grid=(lay.n // tm, D_FF // tf),
        in_specs=[pl.BlockSpec((tm, D_MODEL), lambda i, j: (i, 0)),
                  pl.BlockSpec((None, D_MODEL, tf), lambda i, j: (i_even, 0, j)),
                  pl.BlockSpec((None, D_MODEL, tf), lambda i, j: (i_even, 0, j)),
                  pl.BlockSpec((None, tf, D_MODEL), lambda i, j: (i_even, j, 0)),
                  pl.BlockSpec((tm, D_MODEL), lambda i, j: (i, 0)),
                  lay.mod_spec(l, 5, tm)],
        out_specs=pl.BlockSpec((tm, D_MODEL), lambda i, j: (i, 0)),
        scratch_shapes=[pltpu.VMEM((tm, D_MODEL), F32)],
        compiler_params=_params(("arbitrary", "arbitrary")),
        name="dense_ffn",
    )(h2, ffn_gate, ffn_up, ffn_down, x, modtab)


def _pool_kernel(xp_ref, xc_ref, xn_ref, g1_ref, sh1_ref, sc1_ref, gt1_ref, pw_ref, ps_ref,
                 g2_ref, sh2_ref, sc2_ref, wr_ref, xo_ref, h2_ref, ridx_ref, rw_ref, cnt_ref, count_scr,
                 *, nc, seq, dec_seq):
    i = pl.program_id(0)
    first = i * SEG
    is_lat = first >= nc
    t_len = jnp.where(is_lat, dec_seq, seq)
    off = jnp.where(is_lat, (first - nc) % dec_seq, first % seq)

    g1, sh1, sc1 = g1_ref[0], sh1_ref[0], sc1_ref[0]
    x = xc_ref[...]
    h_cur = _modulate(x, g1, sh1, sc1)
    h_cat = jnp.concatenate([_modulate(xp_ref[...], g1, sh1, sc1), h_cur,
                             _modulate(xn_ref[...], g1, sh1, sc1)], axis=0)

    t = off + lax.broadcasted_iota(jnp.int32, (SEG, 1), 0)
    p = off - SEG + lax.broadcasted_iota(jnp.int32, (1, 3 * SEG), 1)
    mixes = []
    for gi, w in enumerate(POOL_WINDOWS):
        sl = slice(gi * POOL_G, (gi + 1) * POOL_G)
        lo = jnp.maximum(t - w // 2, 0)
        hi = jnp.minimum(t + (w - w // 2), t_len)
        band = _mask01((p >= lo) & (p < hi))
        hg = h_cat[:, sl]
        hg_hi = hg.astype(BF16)
        hg_lo = (hg - hg_hi.astype(F32)).astype(BF16)
        window_sum = _dot(band, hg_hi) + _dot(band, hg_lo)
        y = (window_sum / (hi - lo).astype(F32) - h_cur[:, sl]).astype(BF16)
        mixes.append(_dot(y, pw_ref[gi].astype(BF16)))
    mix = jnp.concatenate(mixes, axis=1) * ps_ref[...]
    xn = x + gt1_ref[0] * mix
    xo_ref[...] = xn
    h2 = _modulate(xn, g2_ref[0], sh2_ref[0], sc2_ref[0])
    h2_ref[...] = h2

    logits = jnp.dot(h2, wr_ref[...], preferred_element_type=F32, precision=lax.Precision.HIGHEST)
    lane = lax.broadcasted_iota(jnp.int32, logits.shape, 1)
    lane_f = lane.astype(F32)
    lg = jnp.where(lane < N_EXP, logits, -jnp.inf)
    m1 = jnp.max(lg, axis=-1, keepdims=True)
    i1 = jnp.min(jnp.where(lg == m1, lane_f, float(LANES)), axis=-1, keepdims=True)
    lg2 = jnp.where(lane_f == i1, -jnp.inf, lg)
    m2 = jnp.max(lg2, axis=-1, keepdims=True)
    i2 = jnp.min(jnp.where(lg2 == m2, lane_f, float(LANES)), axis=-1, keepdims=True)
    e = jnp.exp(m2 - m1)
    w1 = 1.0 / (1.0 + e)
    w2 = e / (1.0 + e)
    rw_ref[...] = jnp.where(lane == 0, w1, jnp.where(lane == 1, w2, 0.0))

    @pl.when(i == 0)
    def _():
        count_scr[...] = jnp.zeros_like(count_scr)

    hit1, hit2 = lane_f == i1, lane_f == i2
    picks = jnp.where(hit1 | hit2, 1.0, 0.0)
    ri = lax.broadcasted_iota(jnp.int32, (SEG, SEG), 0)
    ci = lax.broadcasted_iota(jnp.int32, (SEG, SEG), 1)
    before = _dot(_mask01(ci < ri), picks.astype(BF16)) + count_scr[...]
    r1 = jnp.sum(jnp.where(hit1, before, 0.0), axis=-1, keepdims=True)
    r2 = jnp.sum(jnp.where(hit2, before, 0.0), axis=-1, keepdims=True)
    count_scr[...] += jnp.sum(picks, axis=0, keepdims=True)
    route = jnp.where(lane == 0, i1, jnp.where(lane == 1, i2, jnp.where(lane == 2, r1, jnp.where(lane == 3, r2, 0.0))))
    ridx_ref[...] = route.astype(jnp.int32)
    cnt_ref[...] = jnp.broadcast_to(count_scr[...], cnt_ref.shape)


def _pool_and_route(lay, x, normg, modtab, pool_w, pool_scale, w_router, l):
    i_odd = l // 2
    nblk = lay.n // SEG
    blk = lambda imap: pl.BlockSpec((SEG, D_MODEL), imap)
    lane_blk = pl.BlockSpec((SEG, LANES), lambda i: (i, 0))
    return pl.pallas_call(
        functools.partial(_pool_kernel, nc=lay.nc, seq=lay.seq, dec_seq=lay.dec_seq),
        out_shape=(jax.ShapeDtypeStruct((lay.n, D_MODEL), F32),
                   jax.ShapeDtypeStruct((lay.n, D_MODEL), F32),
                   jax.ShapeDtypeStruct((lay.n, LANES), jnp.int32),
                   jax.ShapeDtypeStruct((lay.n, LANES), F32),
                   jax.ShapeDtypeStruct((8, LANES), F32)),
        grid=(nblk,),
        in_specs=[blk(lambda i: (jnp.maximum(i - 1, 0), 0)),
                  blk(lambda i: (i, 0)),
                  blk(lambda i: (jnp.minimum(i + 1, nblk - 1), 0)),
                  _row_spec(l, 0), lay.mod_spec(l, 0, SEG), lay.mod_spec(l, 1, SEG), lay.mod_spec(l, 2, SEG),
                  pl.BlockSpec((None, len(POOL_WINDOWS), POOL_G, POOL_G), lambda i: (i_odd, 0, 0, 0)),
                  pl.BlockSpec((None, 1, D_MODEL), lambda i: (i_odd, 0, 0)),
                  _row_spec(l, 1), lay.mod_spec(l, 3, SEG), lay.mod_spec(l, 4, SEG),
                  pl.BlockSpec((None, D_MODEL, LANES), lambda i: (i_odd, 0, 0))],
        out_specs=(blk(lambda i: (i, 0)), blk(lambda i: (i, 0)), lane_blk, lane_blk,
                   pl.BlockSpec((8, LANES), lambda i: (0, 0))),
        scratch_shapes=[pltpu.VMEM((1, LANES), F32)],
        compiler_params=_params(("arbitrary",)),
        name="pool_route",
    )(x, x, x, normg, modtab, modtab, modtab, pool_w, pool_scale, normg, modtab, modtab, w_router)


def _dispatch_kernel(dest_ref, h_hbm, zeros_hbm, xs_hbm, sem, *, tokens):
    del zeros_hbm
    t0 = pl.program_id(0) * tokens

    def row_copy(t, k):
        return pltpu.make_async_copy(h_hbm.at[pl.ds(t0 + t, 1)], xs_hbm.at[pl.ds(dest_ref[0, 2 * t + k], 1)], sem)

    def start(t, carry):
        row_copy(t, 0).start()
        row_copy(t, 1).start()
        return carry

    def wait(t, carry):
        one_row = pltpu.make_async_copy(h_hbm.at[pl.ds(0, 1)], xs_hbm.at[pl.ds(0, 1)], sem)
        one_row.wait()
        one_row.wait()
        return carry

    lax.fori_loop(0, tokens, start, 0, unroll=8)
    lax.fori_loop(0, tokens, wait, 0, unroll=8)


def _moe_dispatch(h2, dest, p_max):
    n = h2.shape[0]
    tokens = math.gcd(n, 1024)
    zeros = jnp.zeros((p_max, D_MODEL), F32)
    return pl.pallas_call(
        functools.partial(_dispatch_kernel, tokens=tokens),
        out_shape=jax.ShapeDtypeStruct((p_max, D_MODEL), F32),
        grid=(n // tokens,),
        in_specs=[pl.BlockSpec((None, 1, 2 * tokens), lambda i: (i, 0, 0), memory_space=pltpu.SMEM),
                  pl.BlockSpec(memory_space=pl.ANY),
                  pl.BlockSpec(memory_space=pl.ANY)],
        out_specs=pl.BlockSpec(memory_space=pl.ANY),
        scratch_shapes=[pltpu.SemaphoreType.DMA],
        input_output_aliases={2: 0},
        compiler_params=_params(("arbitrary",)),
        name="moe_dispatch",
    )(dest.reshape(n // tokens, 1, 2 * tokens), h2, zeros)


def _moe_kernel(ge_ref, gs_ref, gn_ref, xs_hbm, wg_ref, wu_ref, wd_ref, ys_hbm, xbuf, x16, acc, sem_in, sem_out):
    del ge_ref
    g = pl.program_id(0)
    j = pl.program_id(1)
    last_j = pl.num_programs(1) - 1
    nsub = gn_ref[g]
    row0 = gs_ref[g]

    def in_copy(s):
        rows = pl.ds(pl.multiple_of(row0 + s * MOE_SUB, MOE_SUB), MOE_SUB)
        return pltpu.make_async_copy(xs_hbm.at[rows], xbuf.at[pl.ds(pl.multiple_of(s * MOE_SUB, MOE_SUB), MOE_SUB)],
                                     sem_in.at[s])

    def out_copy(first, n_rows):
        rows = pl.ds(pl.multiple_of(row0 + first, MOE_SUB), n_rows)
        return pltpu.make_async_copy(acc.at[pl.ds(pl.multiple_of(first, MOE_SUB), n_rows)], ys_hbm.at[rows], sem_out)

    def loop(count, fn):
        def body(s, carry):
            fn(s)
            return carry
        lax.fori_loop(0, count, body, 0)

    @pl.when(j == 0)
    def _():
        loop(nsub, lambda s: in_copy(s).start())

    wg = wg_ref[...].astype(BF16)
    wu = wu_ref[...].astype(BF16)
    wd = wd_ref[...].astype(BF16)

    def tile(first_sub, n_sub):
        n_rows = n_sub * MOE_SUB
        first = pl.multiple_of(first_sub * MOE_SUB, MOE_SUB)
        rows = pl.ds(first, n_rows)

        @pl.when(j == 0)
        def _():
            for k in range(n_sub):
                in_copy(first_sub + k).wait()
            x16[rows, :] = xbuf[rows, :].astype(BF16)

        xs = x16[rows, :]
        a = (_silu(_dot(xs, wg)) * _dot(xs, wu)).astype(BF16)
        contrib = _dot(a, wd)

        @pl.when(j == 0)
        def _():
            acc[rows, :] = contrib

        @pl.when(j > 0)
        def _():
            acc[rows, :] += contrib

        @pl.when(j == last_j)
        def _():
            out_copy(first, n_rows).start()

    n_pairs = lax.shift_right_logical(nsub, 1)
    odd = nsub - 2 * n_pairs
    loop(n_pairs, lambda p: tile(2 * p, 2))

    @pl.when(odd == 1)
    def _():
        tile(nsub - 1, 1)

    @pl.when(j == last_j)
    def _():
        loop(n_pairs, lambda p: out_copy(2 * p * MOE_SUB, 2 * MOE_SUB).wait())

        @pl.when(odd == 1)
        def _():
            out_copy((nsub - 1) * MOE_SUB, MOE_SUB).wait()


def _moe_plan(ridx, counts, n_tok):
    n_pairs = 2 * n_tok
    group_rows = MOE_SUB * MOE_GROUP
    g_max = -(-n_pairs // group_rows) + N_EXP
    counts = counts[0, :N_EXP].astype(jnp.int32)
    nsub_e = (counts + MOE_SUB - 1) // MOE_SUB
    region = nsub_e * MOE_SUB
    start_e = jnp.cumsum(region) - region
    expert = ridx[:, 0:2]
    is_e = expert[:, :, None] == jnp.arange(N_EXP, dtype=jnp.int32)[None, None, :]
    dest = jnp.sum(jnp.where(is_e, start_e[None, None, :], 0), axis=-1) + ridx[:, 2:4]

    ngrp_e = (nsub_e + MOE_GROUP - 1) // MOE_GROUP
    gend = jnp.cumsum(ngrp_e)
    gstart = gend - ngrp_e
    total = gend[-1]
    gid = jnp.arange(g_max, dtype=jnp.int32)
    ge = jnp.minimum(jnp.searchsorted(gend, gid, side="right").astype(jnp.int32), N_EXP - 1)
    kk = gid - gstart[ge]
    live = gid < total
    last_e = jnp.minimum(jnp.searchsorted(gend, total - 1, side="right").astype(jnp.int32), N_EXP - 1)
    g_expert = jnp.where(live, ge, last_e)
    g_row = jnp.where(live, start_e[ge] + kk * group_rows, 0)
    g_nsub = jnp.where(live, jnp.clip(nsub_e[ge] - kk * MOE_GROUP, 0, MOE_GROUP), 0)
    return dest, g_expert.astype(jnp.int32), g_row.astype(jnp.int32), g_nsub.astype(jnp.int32)


def _moe_experts(xs, g_expert, g_row, g_nsub, moe_gate, moe_up, moe_down, i_odd):
    p_max = xs.shape[0]
    g_max = g_expert.shape[0]
    tf = FF_TILE
    group_rows = MOE_SUB * MOE_GROUP
    grid_spec = pltpu.PrefetchScalarGridSpec(
        num_scalar_prefetch=3,
        grid=(g_max, D_FF // tf),
        in_specs=[pl.BlockSpec(memory_space=pl.ANY),
                  pl.BlockSpec((None, None, D_MODEL, tf), lambda g, j, ge, gs, gn: (i_odd, ge[g], 0, j)),
                  pl.BlockSpec((None, None, D_MODEL, tf), lambda g, j, ge, gs, gn: (i_odd, ge[g], 0, j)),
                  pl.BlockSpec((None, None, tf, D_MODEL), lambda g, j, ge, gs, gn: (i_odd, ge[g], j, 0))],
        out_specs=pl.BlockSpec(memory_space=pl.ANY),
        scratch_shapes=[pltpu.VMEM((group_rows, D_MODEL), F32),
                        pltpu.VMEM((group_rows, D_MODEL), BF16),
                        pltpu.VMEM((group_rows, D_MODEL), F32),
                        pltpu.SemaphoreType.DMA((MOE_GROUP,)), pltpu.SemaphoreType.DMA],
    )
    return pl.pallas_call(
        _moe_kernel,
        out_shape=jax.ShapeDtypeStruct((p_max, D_MODEL), F32),
        grid_spec=grid_spec,
        input_output_aliases={3: 0},
        compiler_params=_params(("arbitrary", "arbitrary")),
        name="moe_experts",
    )(g_expert, g_row, g_nsub, xs, moe_gate, moe_up, moe_down)


def _combine_kernel(dest_ref, ys_hbm, x_ref, rw_ref, gt_ref, fg_ref, o_ref, ybuf, sem, *, tokens, final):
    def row_copy(t, k):
        return pltpu.make_async_copy(ys_hbm.at[pl.ds(dest_ref[0, 2 * t + k], 1)], ybuf.at[k, pl.ds(t, 1)], sem)

    def start(t, carry):
        row_copy(t, 0).start()
        row_copy(t, 1).start()
        return carry

    def wait(t, carry):
        one_row = pltpu.make_async_copy(ys_hbm.at[pl.ds(0, 1)], ybuf.at[0, pl.ds(0, 1)], sem)
        one_row.wait()
        one_row.wait()
        return carry

    lax.fori_loop(0, tokens, start, 0, unroll=8)
    lax.fori_loop(0, tokens, wait, 0, unroll=8)
    rw = rw_ref[...]
    y = rw[:, 0:1] * ybuf[0] + rw[:, 1:2] * ybuf[1]
    xn = x_ref[...] + gt_ref[0] * y
    if final:
        xn = xn * lax.rsqrt(jnp.mean(xn * xn, axis=-1, keepdims=True) + EPS) * fg_ref[...]
    o_ref[...] = xn


def _moe_combine(lay, x, ys, dest, rw, modtab, final_g, l, final):
    tm = lay.tm
    full = lambda: pl.BlockSpec((tm, D_MODEL), lambda i: (i, 0))
    return pl.pallas_call(
        functools.partial(_combine_kernel, tokens=tm, final=final),
        out_shape=jax.ShapeDtypeStruct((lay.n, D_MODEL), F32),
        grid=(lay.n // tm,),
        in_specs=[pl.BlockSpec((None, 1, 2 * tm), lambda i: (i, 0, 0), memory_space=pltpu.SMEM),
                  pl.BlockSpec(memory_space=pl.ANY),
                  full(),
                  pl.BlockSpec((tm, LANES), lambda i: (i, 0)),
                  lay.mod_spec(l, 5, tm),
                  pl.BlockSpec((1, D_MODEL), lambda i: (0, 0))],
        out_specs=full(),
        scratch_shapes=[pltpu.VMEM((2, tm, D_MODEL), F32), pltpu.SemaphoreType.DMA],
        compiler_params=_params(("arbitrary",)),
        name="moe_combine",
    )(dest.reshape(lay.n // tm, 1, 2 * tm), ys, x, rw, modtab, final_g)


def kernel(x_prompt, x_sample, cache_k_ctx, cache_v_ctx, state_delta, c, c_ctx, w_mod, b_mod, norm_g, final_g,
           w_in, conv_w, a_log, dt_bias, rpb, o_gain, w_out, ffn_gate, ffn_up, ffn_down, pool_w, pool_scale,
           w_router, moe_gate, moe_up, moe_down):
    batch, seq, d = x_prompt.shape
    dec_batch, dec_seq, _ = x_sample.shape
    depth = w_mod.shape[0]
    n_even = w_in.shape[0]
    past = cache_k_ctx.shape[2]
    assert d == D_MODEL and depth % 2 == 0
    lay = _Layout(batch, seq, dec_batch, dec_seq)

    cond = jnp.zeros((lay.mod_rows, d), F32).at[0].set(c_ctx).at[1:1 + dec_batch].set(c)
    mod = _adaln(cond, w_mod, b_mod)
    modtab = mod.reshape(depth, lay.mod_rows, 6, d).transpose(0, 2, 1, 3).reshape(depth * 6 * lay.mod_rows, 1, d)
    normg = norm_g.reshape(depth * 2, 1, d)

    x = jnp.concatenate([x_prompt.reshape(lay.nc, d), x_sample.reshape(lay.nl, d)], axis=0)
    cache_k = cache_k_ctx.reshape(dec_batch, n_even, past, A_W)
    cache_v = cache_v_ctx.reshape(dec_batch, n_even, past, A_W)
    rows = dec_seq // GRID_W
    kh = min(NA_KH, rows)
    n_main = 3 * A_W + 4 * B_W
    pad128 = lambda a: jnp.pad(a.reshape(1, -1), ((0, 0), (0, LANES - a.size)))
    w_router_p = jnp.pad(w_router, ((0, 0), (0, 0), (0, LANES - N_EXP)))

    k_list, v_list, s_list = [], [], []
    for l in range(depth):
        i = l // 2
        if l % 2 == 0:
            w_ab = jnp.pad(w_in[i][:, n_main:], ((0, 0), (0, LANES - 4 * H_B)))
            proj, ab = _even_project(lay, x, normg, modtab, w_in, w_ab, l)
            bias = _na_bias_table(rpb[i], rows, kh)
            oa = _attention(lay, proj, cache_k, cache_v, bias, i)
            gates, gates_t = _delta_gates(ab, pad128(a_log[i]), pad128(dt_bias[i]))
            qc, kc, vc = _delta_prep(proj, conv_w, i, 0, batch, seq)
            of_c, ob_c, s_fin = _delta_net(qc, kc, vc, gates, gates_t, 0, None, batch, seq)
            ql, kl, vl = _delta_prep(proj, conv_w, i, lay.nc // dec_seq, dec_batch, dec_seq)
            of_l, ob_l, _ = _delta_net(ql, kl, vl, gates, gates_t, lay.nc // SEG, state_delta[:, i],
                                       dec_batch, dec_seq)
            o_f = jnp.concatenate([of_c, of_l], axis=0)
            o_b = jnp.concatenate([ob_c, ob_l], axis=0)
            x, h2 = _even_output(lay, oa, o_f, o_b, proj, o_gain.reshape(n_even, 1, DV), w_out, x, normg, modtab, l)
            x = _dense_ffn(lay, h2, x, ffn_gate, ffn_up, ffn_down, modtab, l)
            k_list.append(proj[:lay.nc, A_W:2 * A_W].reshape(batch, seq, H_A, DH_A))
            v_list.append(proj[:lay.nc, 2 * A_W:3 * A_W].reshape(batch, seq, H_A, DH_A))
            s_list.append(s_fin)
        else:
            x, h2, ridx, rw, counts = _pool_and_route(lay, x, normg, modtab, pool_w,
                                                      pool_scale.reshape(-1, 1, d), w_router_p, l)
            dest, g_expert, g_row, g_nsub = _moe_plan(ridx, counts, lay.n)
            xs = _moe_dispatch(h2, dest, 2 * lay.n + N_EXP * MOE_SUB)
            ys = _moe_experts(xs, g_expert, g_row, g_nsub, moe_gate, moe_up, moe_down, i)
            x = _moe_combine(lay, x, ys, dest, rw, modtab, final_g.reshape(1, d), l, final=(l == depth - 1))
    y_prompt = x[:lay.nc].reshape(batch, seq, d)
    y_sample = x[lay.nc:].reshape(dec_batch, dec_seq, d)
    return (y_prompt, y_sample, jnp.stack(k_list, axis=1), jnp.stack(v_list, axis=1), jnp.stack(s_list, axis=1))
```

```python
import functools
import math

import numpy as np
import jax
import jax.numpy as jnp
from jax import lax
from jax.experimental import pallas as pl
from jax.experimental.pallas import tpu as pltpu

F32 = jnp.float32
BF16 = jnp.bfloat16

D_MODEL = 1024
GRID_W = 64
DH_A = 64
H_A = 8
A_W = H_A * DH_A
NA_KH = 8
NA_KW = 16
DK = 128
DV = 128
H_B = 4
B_W = H_B * DK
CONV_K = 3
CHUNK = 64
POOL_WINDOWS = (2, 4, 8, 16)
POOL_G = D_MODEL // len(POOL_WINDOWS)
D_FF = 7 * D_MODEL // 2
N_EXP = 8
EPS = 1e-6
NEG_INF = -1e30

LANES = 128
SEG = 256
POOL_HALO = 64
FF_TILE = 512
PROJ_TILE = 896
MOE_SUB = 256
MOE_GROUP = 8
VMEM_LIMIT = 56 * 2 ** 20


def _params(sem, vmem=VMEM_LIMIT):
    return pltpu.CompilerParams(dimension_semantics=sem, vmem_limit_bytes=vmem)


def _silu(x):
    return x * jax.nn.sigmoid(x)


def _dot(a, b):
    return jnp.dot(a, b, preferred_element_type=F32)


def _dot_nt(a, b):
    return lax.dot_general(a, b, (((1,), (1,)), ((), ())), preferred_element_type=F32)


def _dot_f32x3(a01, x):
    x1 = x.astype(BF16)
    r1 = x - x1.astype(F32)
    x2 = r1.astype(BF16)
    x3 = (r1 - x2.astype(F32)).astype(BF16)
    return _dot(a01, x1) + _dot(a01, x2) + _dot(a01, x3)


def _dot_f32ish(a, b):
    a_hi, b_hi = a.astype(BF16), b.astype(BF16)
    a_lo, b_lo = (a - a_hi.astype(F32)).astype(BF16), (b - b_hi.astype(F32)).astype(BF16)
    return _dot(a_hi, b_hi) + (_dot(a_hi, b_lo) + _dot(a_lo, b_hi))


def _mask01(mask):
    return jnp.where(mask, 1.0, 0.0).astype(BF16)


def _modulate(x, g, shift, scale):
    y = x * lax.rsqrt(jnp.mean(x * x, axis=-1, keepdims=True) + EPS)
    return (y * g) * (1.0 + scale) + shift


def _cast_kernel(w_ref, o_ref):
    o_ref[...] = w_ref[...].astype(BF16)


def _to_bf16(w):
    layers, rows, cols = w.shape
    tr = 256
    assert rows % tr == 0
    spec = pl.BlockSpec((None, tr, cols), lambda l, r: (l, r, 0))
    return pl.pallas_call(
        _cast_kernel,
        out_shape=jax.ShapeDtypeStruct(w.shape, BF16),
        grid=(layers, rows // tr),
        in_specs=[spec],
        out_specs=spec,
        compiler_params=_params(("arbitrary", "arbitrary")),
        name="to_bf16",
    )(w)


def _adaln_kernel(cond_ref, w_ref, b_ref, o_ref):
    s = _silu(cond_ref[...]).astype(BF16)
    o_ref[0] = _dot(s, w_ref[0].astype(BF16)) + b_ref[0]


def _adaln(cond, w_mod, b_mod):
    depth, d, six_d = w_mod.shape
    r = cond.shape[0]
    tn = six_d // 4
    return pl.pallas_call(
        _adaln_kernel,
        out_shape=jax.ShapeDtypeStruct((depth, r, six_d), F32),
        grid=(depth, six_d // tn),
        in_specs=[
            pl.BlockSpec((r, d), lambda l, j: (0, 0)),
            pl.BlockSpec((1, d, tn), lambda l, j: (l, 0, j)),
            pl.BlockSpec((1, 1, tn), lambda l, j: (l, 0, j)),
        ],
        out_specs=pl.BlockSpec((1, r, tn), lambda l, j: (l, 0, j)),
        compiler_params=_params(("arbitrary", "arbitrary")),
        name="adaln",
    )(cond, w_mod, b_mod.reshape(depth, 1, six_d))


class _Layout:
    def __init__(self, batch, seq, dec_batch, dec_seq):
        self.batch, self.seq, self.dec_batch, self.dec_seq = batch, seq, dec_batch, dec_seq
        self.nc = batch * seq
        self.nl = dec_batch * dec_seq
        self.n = self.nc + self.nl
        self.mod_rows = -(-(1 + dec_batch) // 8) * 8
        assert seq % SEG == 0 and dec_seq % SEG == 0 and dec_seq % GRID_W == 0
        self.tm = math.gcd(self.nc, 1024)
        assert self.tm % SEG == 0 and dec_seq % self.tm == 0 and self.nc % dec_seq == 0

    def mod_row(self, first_row):
        return jnp.maximum((first_row - self.nc) // self.dec_seq + 1, 0)

    def mod_spec(self, l, k, rows_per_step):
        base = (l * 6 + k) * self.mod_rows
        return pl.BlockSpec((1, 1, D_MODEL),
                            lambda i, *_: (base + self.mod_row(i * rows_per_step), 0, 0))


def _row_spec(l, k):
    return pl.BlockSpec((1, 1, D_MODEL), lambda i, *_: (l * 2 + k, 0, 0))


def _proj_kernel(x_ref, g_ref, sh_ref, sc_ref, w_ref, wab_ref, proj_ref, ab_ref, h_scr):
    @pl.when(pl.program_id(1) == 0)
    def _():
        h = _modulate(x_ref[...], g_ref[0], sh_ref[0], sc_ref[0]).astype(BF16)
        h_scr[...] = h
        ab_ref[...] = _dot(h, wab_ref[...].astype(BF16))

    proj_ref[...] = _dot(h_scr[...], w_ref[...])


def _even_project(lay, x, normg, modtab, w_in, w_ab, l):
    i_even = l // 2
    tm, tn = lay.tm, PROJ_TILE
    ncols = 3 * A_W + 4 * B_W
    return pl.pallas_call(
        _proj_kernel,
        out_shape=(jax.ShapeDtypeStruct((lay.n, ncols), F32),
                   jax.ShapeDtypeStruct((lay.n, LANES), F32)),
        grid=(lay.n // tm, ncols // tn),
        in_specs=[
            pl.BlockSpec((tm, D_MODEL), lambda i, j: (i, 0)),
            _row_spec(l, 0),
            lay.mod_spec(l, 0, tm),
            lay.mod_spec(l, 1, tm),
            pl.BlockSpec((None, D_MODEL, tn), lambda i, j: (i_even, 0, j)),
            pl.BlockSpec((D_MODEL, LANES), lambda i, j: (0, 0)),
        ],
        out_specs=(pl.BlockSpec((tm, tn), lambda i, j: (i, j)),
                   pl.BlockSpec((tm, LANES), lambda i, j: (i, 0))),
        scratch_shapes=[pltpu.VMEM((tm, D_MODEL), BF16)],
        compiler_params=_params(("arbitrary", "arbitrary")),
        name="even_project",
    )(x, normg, modtab, modtab, w_in, w_ab)


def _attend_heads(q, keys, values, bias_of):
    scale = DH_A ** -0.5
    per = LANES // DH_A
    lane = lax.broadcasted_iota(jnp.int32, (1, LANES), 1)
    own = [(lane >= s * DH_A) & (lane < (s + 1) * DH_A) for s in range(per)]
    groups = range(A_W // LANES)
    blocks = range(len(keys))
    gsl = lambda g: slice(g * LANES, (g + 1) * LANES)
    k16 = [[k[:, gsl(g)].astype(BF16) for k in keys] for g in groups]
    v16 = [[v[:, gsl(g)].astype(BF16) for v in values] for g in groups]
    heads = [(g, s) for g in groups for s in range(per)]
    qh = [jnp.where(own[s], q[:, gsl(g)], 0.0).astype(BF16) for g, s in heads]
    scores = []
    for h, (g, s) in enumerate(heads):
        row = []
        for i in blocks:
            sc = _dot_nt(qh[h], k16[g][i]) * scale
            b = bias_of(h, i)
            row.append(sc if b is None else sc + b)
        scores.append(row)
    peak = [functools.reduce(jnp.maximum, [jnp.max(sc, axis=-1, keepdims=True) for sc in row]) for row in scores]
    probs = [[jnp.exp(sc - m) for sc in row] for row, m in zip(scores, peak)]
    denom = [functools.reduce(jnp.add, [jnp.sum(p, axis=-1, keepdims=True) for p in row]) for row in probs]
    acc = [functools.reduce(jnp.add, [_dot(p.astype(BF16), v16[g][i]) for i, p in enumerate(row)])
           for row, (g, s) in zip(probs, heads)]
    outs = []
    for g in groups:
        o = jnp.zeros_like(acc[0])
        for s in range(per):
            h = g * per + s
            o = jnp.where(own[s], acc[h] / denom[h], o)
        outs.append(o)
    return jnp.concatenate(outs, axis=1)


def _ctx_attn_kernel(q_ref, k_ref, v_ref, o_ref):
    o_ref[...] = _attend_heads(q_ref[...], [k_ref[...]], [v_ref[...]], lambda h, i: None)


def _na_attn_kernel(q_ref, k_ref, v_ref, kc_ref, vc_ref, bias_ref, o_ref, *, rows, kh):
    r = pl.program_id(1)
    row_lo = jnp.clip(r - kh // 2, 0, rows - kh)
    start = pl.multiple_of(row_lo * GRID_W, GRID_W)
    kl = k_ref[pl.ds(start, kh * GRID_W), :]
    vl = v_ref[pl.ds(start, kh * GRID_W), :]
    o_ref[...] = _attend_heads(q_ref[...], [kl, kc_ref[...]], [vl, vc_ref[...]],
                               lambda h, i: bias_ref[h, 0] if i == 0 else None)


def _na_bias_table(rpb, rows, kh):
    r = np.arange(rows)
    row_idx = np.clip(r - kh // 2, 0, rows - kh)[:, None] + np.arange(kh)[None, :]
    dr = row_idx - r[:, None] + (NA_KH - 1)
    qcol = np.arange(GRID_W)
    kcol = np.arange(GRID_W)
    col_lo = np.clip(qcol - NA_KW // 2, 0, GRID_W - NA_KW)
    valid = (kcol[None, :] >= col_lo[:, None]) & (kcol[None, :] < col_lo[:, None] + NA_KW)
    dc = np.clip(kcol[None, :] - qcol[:, None], 1 - NA_KW, NA_KW - 1) + (NA_KW - 1)
    onehot = (dc[None, :, :] == np.arange(2 * NA_KW - 1)[:, None, None]).astype(np.float32)
    picked = jnp.einsum("hrjc,cqk->hrqjk", rpb[:, dr].astype(F32), onehot, precision=lax.Precision.HIGHEST)
    bias = jnp.where(valid[None, None, :, None, :], picked, NEG_INF)
    return bias.reshape(H_A, rows, GRID_W, kh * GRID_W)


def _attention(lay, proj, cache_k, cache_v, bias, i_even):
    ctx = pl.pallas_call(
        _ctx_attn_kernel,
        out_shape=jax.ShapeDtypeStruct((lay.nc, A_W), F32),
        grid=(lay.batch,),
        in_specs=[pl.BlockSpec((lay.seq, A_W), lambda b: (b, 0)),
                  pl.BlockSpec((lay.seq, A_W), lambda b: (b, 1)),
                  pl.BlockSpec((lay.seq, A_W), lambda b: (b, 2))],
        out_specs=pl.BlockSpec((lay.seq, A_W), lambda b: (b, 0)),
        compiler_params=_params(("arbitrary",)),
        name="ctx_attention",
    )(proj, proj, proj)

    rows = lay.dec_seq // GRID_W
    kh = min(NA_KH, rows)
    past = cache_k.shape[2]
    q0 = lay.nc // GRID_W
    b0 = lay.nc // lay.dec_seq
    lat = pl.pallas_call(
        functools.partial(_na_attn_kernel, rows=rows, kh=kh),
        out_shape=jax.ShapeDtypeStruct((lay.nl, A_W), F32),
        grid=(lay.dec_batch, rows),
        in_specs=[pl.BlockSpec((GRID_W, A_W), lambda b, r: (q0 + b * rows + r, 0)),
                  pl.BlockSpec((lay.dec_seq, A_W), lambda b, r: (b0 + b, 1)),
                  pl.BlockSpec((lay.dec_seq, A_W), lambda b, r: (b0 + b, 2)),
                  pl.BlockSpec((None, None, past, A_W), lambda b, r: (b, i_even, 0, 0)),
                  pl.BlockSpec((None, None, past, A_W), lambda b, r: (b, i_even, 0, 0)),
                  pl.BlockSpec((H_A, 1, GRID_W, kh * GRID_W), lambda b, r: (0, r, 0, 0))],
        out_specs=pl.BlockSpec((GRID_W, A_W), lambda b, r: (b * rows + r, 0)),
        compiler_params=_params(("arbitrary", "arbitrary")),
        name="na_attention",
    )(proj, proj, proj, cache_k, cache_v, bias)
    return ctx, lat


def _seq_position(first, nc, seq, dec_seq):
    is_lat = first >= nc
    return jnp.where(is_lat, (first - nc) % dec_seq, first % seq), jnp.where(is_lat, dec_seq, seq)


def _delta_prep_kernel(xp_ref, xc_ref, xn_ref, w_ref, q_ref, k_ref, v_ref, *, nc, seq, dec_seq):
    off, t_len = _seq_position(pl.program_id(0) * SEG, nc, seq, dec_seq)
    x = xc_ref[...]
    row = lax.broadcasted_iota(jnp.int32, (SEG, 1), 0)
    before = jnp.where(off > 0, xp_ref[7:8, :], 0.0)
    after = jnp.where(off + SEG < t_len, xn_ref[0:1, :], 0.0)
    prev = jnp.where(row == 0, before, pltpu.roll(x, 1, 0))
    nxt = jnp.where(row == SEG - 1, after, pltpu.roll(x, SEG - 1, 0))
    w = w_ref[...]
    y = _silu(prev * w[0:1] + x * w[1:2] + nxt * w[2:3])

    def l2norm(a):
        return a * lax.rsqrt(jnp.sum(a * a, axis=-1, keepdims=True) + EPS)

    heads = lambda part: [y[:, (part * H_B + h) * DK:(part * H_B + h + 1) * DK] for h in range(H_B)]
    q_ref[...] = jnp.concatenate([l2norm(a) * (DK ** -0.5) for a in heads(0)], axis=1)
    k_ref[...] = jnp.concatenate([l2norm(a) for a in heads(1)], axis=1)
    v_ref[...] = y[:, 2 * B_W:]


def _delta_prep(lay, proj, conv_w, i_even):
    width = 3 * B_W
    cblk = 3 * A_W // width
    assert cblk * width == 3 * A_W
    per = SEG // 8
    n8 = lay.n // 8
    ospec = pl.BlockSpec((SEG, B_W), lambda i: (i, 0))
    shape = jax.ShapeDtypeStruct((lay.n, B_W), F32)
    return pl.pallas_call(
        functools.partial(_delta_prep_kernel, nc=lay.nc, seq=lay.seq, dec_seq=lay.dec_seq),
        out_shape=(shape, shape, shape),
        grid=(lay.n // SEG,),
        in_specs=[pl.BlockSpec((8, width), lambda i: (jnp.maximum(i * per - 1, 0), cblk)),
                  pl.BlockSpec((SEG, width), lambda i: (i, cblk)),
                  pl.BlockSpec((8, width), lambda i: (jnp.minimum((i + 1) * per, n8 - 1), cblk)),
                  pl.BlockSpec((None, CONV_K, width), lambda i: (i_even, 0, 0))],
        out_specs=(ospec, ospec, ospec),
        compiler_params=_params(("arbitrary",)),
        name="delta_prep",
    )(proj, proj, proj, conv_w)


GATE_GC, GATE_BETA, GATE_GT = 0, 2 * H_B, 4 * H_B


def _delta_gates_kernel(ab_ref, alog_ref, dtb_ref, g_ref, gt_ref):
    n = SEG
    ab = ab_ref[...]
    lane = lax.broadcasted_iota(jnp.int32, (1, LANES), 1)
    ri = lax.broadcasted_iota(jnp.int32, (n, n), 0)
    ci = lax.broadcasted_iota(jnp.int32, (n, n), 1)
    shift = int(math.log2(CHUNK))
    same = jnp.right_shift(ri, shift) == jnp.right_shift(ci, shift)
    g_all = -jnp.exp(alog_ref[...]) * jax.nn.softplus(ab + dtb_ref[...])
    prefix = _dot_f32x3(_mask01(same & (ci <= ri)), g_all)
    total = _dot_f32x3(_mask01(same), g_all)
    gc = jnp.where(lane < H_B, prefix, total - prefix + g_all)
    table = jnp.where(lane < GATE_BETA, gc,
                      jnp.where(lane < GATE_GT, jax.nn.sigmoid(ab), pltpu.roll(total, GATE_GT, 1)))
    g_ref[...] = table
    gt_ref[...] = table.T


def _delta_gates(ab, alog, dtb):
    n = ab.shape[0]
    row = pl.BlockSpec((1, LANES), lambda i: (0, 0))
    return pl.pallas_call(
        _delta_gates_kernel,
        out_shape=(jax.ShapeDtypeStruct((n, LANES), F32), jax.ShapeDtypeStruct((LANES, n), F32)),
        grid=(n // SEG,),
        in_specs=[pl.BlockSpec((SEG, LANES), lambda i: (i, 0)), row, row],
        out_specs=(pl.BlockSpec((SEG, LANES), lambda i: (i, 0)), pl.BlockSpec((LANES, SEG), lambda i: (0, i))),
        compiler_params=_params(("arbitrary",)),
        name="delta_gates",
    )(ab, alog, dtb)


def _delta_masks(d):
    n = SEG
    ri = lax.broadcasted_iota(jnp.int32, (n, n), 0)
    ci = lax.broadcasted_iota(jnp.int32, (n, n), 1)
    shift = int(math.log2(CHUNK))
    same = jnp.right_shift(ri, shift) == jnp.right_shift(ci, shift)
    incl = same & ((ci <= ri) if d == 0 else (ci >= ri))
    strict = same & ((ci < ri) if d == 0 else (ci > ri))
    levels = []
    for level in range(shift):
        bi, bj = jnp.right_shift(ri, level), jnp.right_shift(ci, level)
        siblings = jnp.right_shift(bi, 1) == jnp.right_shift(bj, 1)
        levels.append(siblings & ((bi > bj) if d == 0 else (bi < bj)))
    return incl, strict, jnp.where(ri == ci, 1.0, 0.0), levels


def _delta_chains(chains):
    n = SEG
    n_chunks = n // CHUNK
    each = lambda fn, *lists: [fn(*args) for args in zip(*lists)]
    ds, masks, qs, ks, vs, betas, gcs, gts, gc_rows, gt_rows, states = (list(t) for t in zip(*chains))
    incls, stricts = [m[0] for m in masks], [m[1] for m in masks]
    n_levels = len(masks[0][3])

    decays = each(lambda m, gc, gr: jnp.where(m, jnp.exp(jnp.where(m, gc - gr, 0.0)), 0.0), incls, gcs, gc_rows)
    kbs = each(lambda k, b: k * b, ks, betas)
    k16s = [k.astype(BF16) for k in ks]
    grams = each(lambda kb, k16: _dot_nt(kb.astype(BF16), k16), kbs, k16s)
    lmats = each(lambda m, g, dec: jnp.where(m, g * dec, 0.0), stricts, grams, decays)

    xs = [m[2] for m in masks]
    for level in range(n_levels):
        x16s = [x.astype(BF16) for x in xs]
        cs = each(lambda m, lm: jnp.where(m[3][level], lm, 0.0).astype(BF16), masks, lmats)
        xcs = each(lambda x16, c: _dot(x16, c).astype(BF16), x16s, cs)
        xs = each(lambda x, xc, x16: x - _dot(xc, x16), xs, xcs, x16s)

    egs = [jnp.exp(gc) for gc in gcs]
    rhss = each(lambda v, b, kb, eg: jnp.concatenate([v * b, kb * eg], axis=1).astype(BF16), vs, betas, kbs, egs)
    uws = each(lambda x, rhs: _dot(x.astype(BF16), rhs), xs, rhss)
    qks = each(lambda q, k16: _dot_nt(q.astype(BF16), k16), qs, k16s)
    attns = each(lambda m, qk, dec: jnp.where(m, qk * dec, 0.0).astype(BF16), incls, qks, decays)
    qgs = each(lambda q, eg: (q * eg).astype(BF16), qs, egs)
    kd_ts = each(lambda k, gt_r, gc_r: (k.T * jnp.exp(gt_r - gc_r)).astype(BF16), ks, gt_rows, gc_rows)
    gls = [jnp.exp(gt) for gt in gts]

    v_new = [[None] * n_chunks for _ in chains]
    o_state = [[None] * n_chunks for _ in chains]
    for step in range(n_chunks):
        for i, d in enumerate(ds):
            c = step if d == 0 else n_chunks - 1 - step
            rs = slice(c * CHUNK, (c + 1) * CHUNK)
            s16 = states[i].astype(BF16)
            vn = uws[i][rs, :DV] - _dot(uws[i][rs, DV:].astype(BF16), s16)
            o_state[i][c] = _dot(qgs[i][rs], s16)
            v_new[i][c] = vn
            pieces = [jnp.zeros((c * CHUNK, DV), BF16), vn.astype(BF16),
                      jnp.zeros((n - (c + 1) * CHUNK, DV), BF16)]
            padded = jnp.concatenate([p for p in pieces if p.shape[0]], axis=0)
            states[i] = states[i] * gls[i][c * CHUNK:c * CHUNK + 1, :] + _dot(kd_ts[i], padded)
    outs = each(lambda os, a, vn: jnp.concatenate(os, axis=0) + _dot(a, jnp.concatenate(vn, axis=0).astype(BF16)),
                o_state, attns, v_new)
    return list(zip(outs, states))


def _delta_kernel(plan_ref, *refs):
    dir_refs = (refs[0:5], refs[5:10])
    s0_ref, of_ref, ob_ref, sfin_ref, s_scr = refs[10:]
    g = pl.program_id(0)
    first, last, has_s0 = plan_ref[2, g] == 1, plan_ref[3, g] == 1, plan_ref[4, g] == 1

    @pl.when(first)
    def _():
        s_scr[...] = jnp.where(has_s0, s0_ref[...], 0.0)

    chains = []
    for d, (q_ref, k_ref, v_ref, g_ref, gt_ref) in enumerate(dir_refs):
        masks = _delta_masks(d)
        q, k, v, gates, gates_t = q_ref[...], k_ref[...], v_ref[...], g_ref[...], gt_ref[...]
        for h in range(H_B):
            sl = slice(h * DK, (h + 1) * DK)
            col = d * H_B + h
            pick = lambda base: gates[:, base + col:base + col + 1]
            pick_t = lambda base: gates_t[base + col:base + col + 1, :]
            chains.append((d, masks, q[:, sl], k[:, sl], v[:, sl], pick(GATE_BETA), pick(GATE_GC),
                           pick(GATE_GT), pick_t(GATE_GC), pick_t(GATE_GT), s_scr[d, h]))
    results = _delta_chains(chains)
    for d, o_ref in enumerate((of_ref, ob_ref)):
        o_ref[...] = jnp.concatenate([results[d * H_B + h][0] for h in range(H_B)], axis=1)
        for h in range(H_B):
            s_scr[d, h] = results[d * H_B + h][1]

    @pl.when(last & jnp.logical_not(has_s0))
    def _():
        sfin_ref[...] = s_scr[...]


def _delta_plan(lay):
    rows = []
    for kind, n_seq, t in ((0, lay.batch, lay.seq), (1, lay.dec_batch, lay.dec_seq)):
        nseg = t // SEG
        base = 0 if kind == 0 else lay.nc // SEG
        for b in range(n_seq):
            for s in range(nseg):
                rows.append((base + b * nseg + s, base + b * nseg + nseg - 1 - s, int(s == 0), int(s == nseg - 1),
                             kind, b if kind else 0, b if kind == 0 else lay.batch - 1))
    return jnp.asarray(np.array(rows, np.int32).T)


def _delta_net(lay, q, k, v, gates, gates_t, s0):
    plan = _delta_plan(lay)
    in_specs, args = [], []
    for d in (0, 1):
        in_specs += [pl.BlockSpec((SEG, B_W), lambda g, plan, d=d: (plan[d, g], 0))] * 3
        in_specs += [pl.BlockSpec((SEG, LANES), lambda g, plan, d=d: (plan[d, g], 0)),
                     pl.BlockSpec((LANES, SEG), lambda g, plan, d=d: (0, plan[d, g]))]
        args += [q, k, v, gates, gates_t]
    state = lambda row: pl.BlockSpec((None, 2, H_B, DK, DV), lambda g, plan: (plan[row, g], 0, 0, 0, 0))
    oshape = jax.ShapeDtypeStruct((lay.n, B_W), F32)
    grid_spec = pltpu.PrefetchScalarGridSpec(
        num_scalar_prefetch=1,
        grid=(lay.n // SEG,),
        in_specs=in_specs + [state(5)],
        out_specs=(pl.BlockSpec((SEG, B_W), lambda g, plan: (plan[0, g], 0)),
                   pl.BlockSpec((SEG, B_W), lambda g, plan: (plan[1, g], 0)), state(6)),
        scratch_shapes=[pltpu.VMEM((2, H_B, DK, DV), F32)],
    )
    return pl.pallas_call(
        _delta_kernel,
        out_shape=(oshape, oshape, jax.ShapeDtypeStruct((lay.batch, 2, H_B, DK, DV), F32)),
        grid_spec=grid_spec,
        compiler_params=_params(("arbitrary",)),
        name="delta_net",
    )(plan, *args, s0)


def _even_out_kernel(oac_ref, oal_ref, of_ref, ob_ref, z_ref, gain_ref, w_ref, x_ref, gt_ref,
                     g2_ref, sh2_ref, sc2_ref, xo_ref, h2_ref, *, ctx_tiles):
    ob = of_ref[...] + ob_ref[...]
    z = z_ref[...]
    oa = jnp.where(pl.program_id(0) < ctx_tiles, oac_ref[...], oal_ref[...])
    parts = [oa.astype(BF16)]
    for h in range(H_B):
        sl = slice(h * DV, (h + 1) * DV)
        o_h = ob[:, sl]
        y = o_h * lax.rsqrt(jnp.mean(o_h * o_h, axis=-1, keepdims=True) + EPS) * gain_ref[...] * _silu(z[:, sl])
        parts.append(y.astype(BF16))
    mix = _dot(jnp.concatenate(parts, axis=1), w_ref[...])
    xn = x_ref[...] + gt_ref[0] * mix
    xo_ref[...] = xn
    h2_ref[...] = _modulate(xn, g2_ref[0], sh2_ref[0], sc2_ref[0]).astype(BF16)


def _even_output(lay, oa_ctx, oa_lat, o_f, o_b, proj, o_gain, w_out, x, normg, modtab, l):
    i_even = l // 2
    tm = lay.tm
    ctx_tiles = lay.nc // tm
    zcol = (3 * A_W + 3 * B_W) // B_W
    half = lambda: pl.BlockSpec((tm, B_W), lambda i: (i, 0))
    full = lambda: pl.BlockSpec((tm, D_MODEL), lambda i: (i, 0))
    return pl.pallas_call(
        functools.partial(_even_out_kernel, ctx_tiles=ctx_tiles),
        out_shape=(jax.ShapeDtypeStruct((lay.n, D_MODEL), F32),
                   jax.ShapeDtypeStruct((lay.n, D_MODEL), BF16)),
        grid=(lay.n // tm,),
        in_specs=[pl.BlockSpec((tm, A_W), lambda i: (jnp.minimum(i, ctx_tiles - 1), 0)),
                  pl.BlockSpec((tm, A_W), lambda i: (jnp.maximum(i - ctx_tiles, 0), 0)),
                  half(), half(),
                  pl.BlockSpec((tm, B_W), lambda i: (i, zcol)),
                  pl.BlockSpec((None, 1, DV), lambda i: (i_even, 0, 0)),
                  pl.BlockSpec((None, D_MODEL, D_MODEL), lambda i: (i_even, 0, 0)),
                  full(),
                  lay.mod_spec(l, 2, tm),
                  _row_spec(l, 1), lay.mod_spec(l, 3, tm), lay.mod_spec(l, 4, tm)],
        out_specs=(full(), full()),
        compiler_params=_params(("arbitrary",)),
        name="even_output",
    )(oa_ctx, oa_lat, o_f, o_b, proj, o_gain, w_out, x, modtab, normg, modtab, modtab)


def _ffn_kernel(h_ref, wg_ref, wu_ref, wd_ref, x_ref, gt_ref, o_ref, acc_ref):
    j = pl.program_id(1)

    @pl.when(j == 0)
    def _():
        acc_ref[...] = jnp.zeros_like(acc_ref)

    h = h_ref[...]
    a = (_silu(_dot(h, wg_ref[...])) * _dot(h, wu_ref[...])).astype(BF16)
    acc_ref[...] += _dot(a, wd_ref[...])

    @pl.when(j == pl.num_programs(1) - 1)
    def _():
        o_ref[...] = x_ref[...] + gt_ref[0] * acc_ref[...]


def _dense_ffn(lay, h2, x, ffn_gate, ffn_up, ffn_down, modtab, l):
    i_even = l // 2
    tm, tf = lay.tm, FF_TILE
    return pl.pallas_call(
        _ffn_kernel,
        out_shape=jax.ShapeDtypeStruct((lay.n, D_MODEL), F32),
        grid=(lay.n // tm, D_FF // tf),
        in_specs=[pl.BlockSpec((tm, D_MODEL), lambda i, j: (i, 0)),
                  pl.BlockSpec((None, D_MODEL, tf), lambda i, j: (i_even, 0, j)),
                  pl.BlockSpec((None, D_MODEL, tf), lambda i, j: (i_even, 0, j)),
                  pl.BlockSpec((None, tf, D_MODEL), lambda i, j: (i_even, j, 0)),
                  pl.BlockSpec((tm, D_MODEL), lambda i, j: (i, 0)),
                  lay.mod_spec(l, 5, tm)],
        out_specs=pl.BlockSpec((tm, D_MODEL), lambda i, j: (i, 0)),
        scratch_shapes=[pltpu.VMEM((tm, D_MODEL), F32)],
        compiler_params=_params(("arbitrary", "arbitrary")),
        name="dense_ffn",
    )(h2, ffn_gate, ffn_up, ffn_down, x, modtab)


def _pool_kernel(xp_ref, xc_ref, xn_ref, g1_ref, sh1_ref, sc1_ref, gt1_ref, pw_ref, ps_ref,
                 g2_ref, sh2_ref, sc2_ref, wr_ref, xo_ref, h2_ref, ridx_ref, rw_ref, cnt_ref, count_scr,
                 *, nc, seq, dec_seq):
    i = pl.program_id(0)
    off, t_len = _seq_position(i * SEG, nc, seq, dec_seq)

    g1, sh1, sc1 = g1_ref[0], sh1_ref[0], sc1_ref[0]
    x = xc_ref[...]
    h_cur = _modulate(x, g1, sh1, sc1)
    h_cat = jnp.concatenate([_modulate(xp_ref[...], g1, sh1, sc1), h_cur,
                             _modulate(xn_ref[...], g1, sh1, sc1)], axis=0)

    t = off + lax.broadcasted_iota(jnp.int32, (SEG, 1), 0)
    p = off - POOL_HALO + lax.broadcasted_iota(jnp.int32, (1, SEG + 2 * POOL_HALO), 1)
    mixes = []
    for gi, w in enumerate(POOL_WINDOWS):
        sl = slice(gi * POOL_G, (gi + 1) * POOL_G)
        lo = jnp.maximum(t - w // 2, 0)
        hi = jnp.minimum(t + (w - w // 2), t_len)
        band = _mask01((p >= lo) & (p < hi))
        hg = h_cat[:, sl]
        hg_hi = hg.astype(BF16)
        hg_lo = (hg - hg_hi.astype(F32)).astype(BF16)
        window_sum = _dot(band, hg_hi) + _dot(band, hg_lo)
        y = (window_sum / (hi - lo).astype(F32) - h_cur[:, sl]).astype(BF16)
        mixes.append(_dot(y, pw_ref[gi].astype(BF16)))
    mix = jnp.concatenate(mixes, axis=1) * ps_ref[...]
    xn = x + gt1_ref[0] * mix
    xo_ref[...] = xn
    h2 = _modulate(xn, g2_ref[0], sh2_ref[0], sc2_ref[0])
    h2_ref[...] = h2

    logits = _dot_f32ish(h2, wr_ref[...])
    lane = lax.broadcasted_iota(jnp.int32, logits.shape, 1)
    lane_f = lane.astype(F32)
    lg = jnp.where(lane < N_EXP, logits, -jnp.inf)
    m1 = jnp.max(lg, axis=-1, keepdims=True)
    i1 = jnp.min(jnp.where(lg == m1, lane_f, float(LANES)), axis=-1, keepdims=True)
    lg2 = jnp.where(lane_f == i1, -jnp.inf, lg)
    m2 = jnp.max(lg2, axis=-1, keepdims=True)
    i2 = jnp.min(jnp.where(lg2 == m2, lane_f, float(LANES)), axis=-1, keepdims=True)
    e = jnp.exp(m2 - m1)
    w1 = 1.0 / (1.0 + e)
    w2 = e / (1.0 + e)
    rw_ref[...] = jnp.where(lane == 0, w1, jnp.where(lane == 1, w2, 0.0))

    @pl.when(i == 0)
    def _():
        count_scr[...] = jnp.zeros_like(count_scr)

    hit1, hit2 = lane_f == i1, lane_f == i2
    picks = jnp.where(hit1 | hit2, 1.0, 0.0)
    ri = lax.broadcasted_iota(jnp.int32, (SEG, SEG), 0)
    ci = lax.broadcasted_iota(jnp.int32, (SEG, SEG), 1)
    before = _dot(_mask01(ci < ri), picks.astype(BF16)) + count_scr[...]
    r1 = jnp.sum(jnp.where(hit1, before, 0.0), axis=-1, keepdims=True)
    r2 = jnp.sum(jnp.where(hit2, before, 0.0), axis=-1, keepdims=True)
    count_scr[...] += jnp.sum(picks, axis=0, keepdims=True)
    route = jnp.where(lane == 0, i1, jnp.where(lane == 1, i2, jnp.where(lane == 2, r1, jnp.where(lane == 3, r2, 0.0))))
    ridx_ref[...] = route.astype(jnp.int32)
    cnt_ref[...] = jnp.broadcast_to(count_scr[...], cnt_ref.shape)


def _pool_and_route(lay, x, normg, modtab, pool_w, pool_scale, w_router, l):
    i_odd = l // 2
    nblk = lay.n // SEG
    blk = lambda imap: pl.BlockSpec((SEG, D_MODEL), imap)
    halo = lambda imap: pl.BlockSpec((POOL_HALO, D_MODEL), imap)
    per = SEG // POOL_HALO
    lane_blk = pl.BlockSpec((SEG, LANES), lambda i: (i, 0))
    return pl.pallas_call(
        functools.partial(_pool_kernel, nc=lay.nc, seq=lay.seq, dec_seq=lay.dec_seq),
        out_shape=(jax.ShapeDtypeStruct((lay.n, D_MODEL), F32),
                   jax.ShapeDtypeStruct((lay.n, D_MODEL), F32),
                   jax.ShapeDtypeStruct((lay.n, LANES), jnp.int32),
                   jax.ShapeDtypeStruct((lay.n, LANES), F32),
                   jax.ShapeDtypeStruct((8, LANES), F32)),
        grid=(nblk,),
        in_specs=[halo(lambda i: (jnp.maximum(i * per - 1, 0), 0)),
                  blk(lambda i: (i, 0)),
                  halo(lambda i: (jnp.minimum((i + 1) * per, nblk * per - 1), 0)),
                  _row_spec(l, 0), lay.mod_spec(l, 0, SEG), lay.mod_spec(l, 1, SEG), lay.mod_spec(l, 2, SEG),
                  pl.BlockSpec((None, len(POOL_WINDOWS), POOL_G, POOL_G), lambda i: (i_odd, 0, 0, 0)),
                  pl.BlockSpec((None, 1, D_MODEL), lambda i: (i_odd, 0, 0)),
                  _row_spec(l, 1), lay.mod_spec(l, 3, SEG), lay.mod_spec(l, 4, SEG),
                  pl.BlockSpec((None, D_MODEL, LANES), lambda i: (i_odd, 0, 0))],
        out_specs=(blk(lambda i: (i, 0)), blk(lambda i: (i, 0)), lane_blk, lane_blk,
                   pl.BlockSpec((8, LANES), lambda i: (0, 0))),
        scratch_shapes=[pltpu.VMEM((1, LANES), F32)],
        compiler_params=_params(("arbitrary",)),
        name="pool_route",
    )(x, x, x, normg, modtab, modtab, modtab, pool_w, pool_scale, normg, modtab, modtab, w_router)


def _dispatch_kernel(dest_ref, h_ref, zeros_hbm, xs_hbm, sem, *, tokens):
    del zeros_hbm

    def row_copy(t, k):
        return pltpu.make_async_copy(h_ref.at[pl.ds(t, 1)], xs_hbm.at[pl.ds(dest_ref[0, 2 * t + k], 1)], sem)

    def start(t, carry):
        row_copy(t, 0).start()
        row_copy(t, 1).start()
        return carry

    def wait(t, carry):
        one_row = pltpu.make_async_copy(h_ref.at[pl.ds(0, 1)], xs_hbm.at[pl.ds(0, 1)], sem)
        one_row.wait()
        one_row.wait()
        return carry

    lax.fori_loop(0, tokens, start, 0, unroll=8)
    lax.fori_loop(0, tokens, wait, 0, unroll=8)


def _moe_dispatch(h2, dest, p_max):
    n = h2.shape[0]
    tokens = math.gcd(n, 1024)
    zeros = jnp.zeros((p_max, D_MODEL), F32)
    return pl.pallas_call(
        functools.partial(_dispatch_kernel, tokens=tokens),
        out_shape=jax.ShapeDtypeStruct((p_max, D_MODEL), F32),
        grid=(n // tokens,),
        in_specs=[pl.BlockSpec((None, 1, 2 * tokens), lambda i: (i, 0, 0), memory_space=pltpu.SMEM),
                  pl.BlockSpec((tokens, D_MODEL), lambda i: (i, 0)),
                  pl.BlockSpec(memory_space=pl.ANY)],
        out_specs=pl.BlockSpec(memory_space=pl.ANY),
        scratch_shapes=[pltpu.SemaphoreType.DMA],
        input_output_aliases={2: 0},
        compiler_params=_params(("arbitrary",)),
        name="moe_dispatch",
    )(dest.reshape(n // tokens, 1, 2 * tokens), h2, zeros)


def _moe_kernel(ge_ref, gs_ref, gn_ref, xs_hbm, wg_ref, wu_ref, wd_ref, ys_hbm, xbuf, x16, acc, sem_in, sem_out):
    del ge_ref
    g = pl.program_id(0)
    j = pl.program_id(1)
    last_j = pl.num_programs(1) - 1
    nsub = gn_ref[g]
    row0 = gs_ref[g]

    def in_copy(s):
        rows = pl.ds(pl.multiple_of(row0 + s * MOE_SUB, MOE_SUB), MOE_SUB)
        return pltpu.make_async_copy(xs_hbm.at[rows], xbuf.at[pl.ds(pl.multiple_of(s * MOE_SUB, MOE_SUB), MOE_SUB)],
                                     sem_in.at[s])

    def out_copy(first, n_rows):
        rows = pl.ds(pl.multiple_of(row0 + first, MOE_SUB), n_rows)
        return pltpu.make_async_copy(acc.at[pl.ds(pl.multiple_of(first, MOE_SUB), n_rows)], ys_hbm.at[rows], sem_out)

    def loop(count, fn):
        def body(s, carry):
            fn(s)
            return carry
        lax.fori_loop(0, count, body, 0)

    @pl.when(j == 0)
    def _():
        loop(nsub, lambda s: in_copy(s).start())

    wg = wg_ref[...].astype(BF16)
    wu = wu_ref[...].astype(BF16)
    wd = wd_ref[...].astype(BF16)

    def tile(first_sub, n_sub):
        n_rows = n_sub * MOE_SUB
        first = pl.multiple_of(first_sub * MOE_SUB, MOE_SUB)
        rows = pl.ds(first, n_rows)

        @pl.when(j == 0)
        def _():
            for k in range(n_sub):
                in_copy(first_sub + k).wait()
            x16[rows, :] = xbuf[rows, :].astype(BF16)

        xs = x16[rows, :]
        a = (_silu(_dot(xs, wg)) * _dot(xs, wu)).astype(BF16)
        contrib = _dot(a, wd)

        @pl.when(j == 0)
        def _():
            acc[rows, :] = contrib

        @pl.when(j > 0)
        def _():
            acc[rows, :] += contrib

        @pl.when(j == last_j)
        def _():
            out_copy(first, n_rows).start()

    n_pairs = lax.shift_right_logical(nsub, 1)
    odd = nsub - 2 * n_pairs
    loop(n_pairs, lambda p: tile(2 * p, 2))

    @pl.when(odd == 1)
    def _():
        tile(nsub - 1, 1)

    @pl.when(j == last_j)
    def _():
        loop(n_pairs, lambda p: out_copy(2 * p * MOE_SUB, 2 * MOE_SUB).wait())

        @pl.when(odd == 1)
        def _():
            out_copy((nsub - 1) * MOE_SUB, MOE_SUB).wait()


def _moe_plan(ridx, counts, n_tok):
    n_pairs = 2 * n_tok
    group_rows = MOE_SUB * MOE_GROUP
    g_max = -(-n_pairs // group_rows) + N_EXP
    counts = counts[0, :N_EXP].astype(jnp.int32)
    nsub_e = (counts + MOE_SUB - 1) // MOE_SUB
    region = nsub_e * MOE_SUB
    start_e = jnp.cumsum(region) - region
    expert = ridx[:, 0:2]
    is_e = expert[:, :, None] == jnp.arange(N_EXP, dtype=jnp.int32)[None, None, :]
    dest = jnp.sum(jnp.where(is_e, start_e[None, None, :], 0), axis=-1) + ridx[:, 2:4]

    ngrp_e = (nsub_e + MOE_GROUP - 1) // MOE_GROUP
    gend = jnp.cumsum(ngrp_e)
    gstart = gend - ngrp_e
    total = gend[-1]
    gid = jnp.arange(g_max, dtype=jnp.int32)
    ge = jnp.minimum(jnp.searchsorted(gend, gid, side="right").astype(jnp.int32), N_EXP - 1)
    kk = gid - gstart[ge]
    live = gid < total
    last_e = jnp.minimum(jnp.searchsorted(gend, total - 1, side="right").astype(jnp.int32), N_EXP - 1)
    g_expert = jnp.where(live, ge, last_e)
    g_row = jnp.where(live, start_e[ge] + kk * group_rows, 0)
    g_nsub = jnp.where(live, jnp.clip(nsub_e[ge] - kk * MOE_GROUP, 0, MOE_GROUP), 0)
    return dest, g_expert.astype(jnp.int32), g_row.astype(jnp.int32), g_nsub.astype(jnp.int32)


def _moe_experts(xs, g_expert, g_row, g_nsub, moe_gate, moe_up, moe_down, i_odd):
    p_max = xs.shape[0]
    g_max = g_expert.shape[0]
    tf = FF_TILE
    group_rows = MOE_SUB * MOE_GROUP
    grid_spec = pltpu.PrefetchScalarGridSpec(
        num_scalar_prefetch=3,
        grid=(g_max, D_FF // tf),
        in_specs=[pl.BlockSpec(memory_space=pl.ANY),
                  pl.BlockSpec((None, None, D_MODEL, tf), lambda g, j, ge, gs, gn: (i_odd, ge[g], 0, j)),
                  pl.BlockSpec((None, None, D_MODEL, tf), lambda g, j, ge, gs, gn: (i_odd, ge[g], 0, j)),
                  pl.BlockSpec((None, None, tf, D_MODEL), lambda g, j, ge, gs, gn: (i_odd, ge[g], j, 0))],
        out_specs=pl.BlockSpec(memory_space=pl.ANY),
        scratch_shapes=[pltpu.VMEM((group_rows, D_MODEL), F32),
                        pltpu.VMEM((group_rows, D_MODEL), BF16),
                        pltpu.VMEM((group_rows, D_MODEL), F32),
                        pltpu.SemaphoreType.DMA((MOE_GROUP,)), pltpu.SemaphoreType.DMA],
    )
    return pl.pallas_call(
        _moe_kernel,
        out_shape=jax.ShapeDtypeStruct((p_max, D_MODEL), F32),
        grid_spec=grid_spec,
        input_output_aliases={3: 0},
        compiler_params=_params(("arbitrary", "arbitrary")),
        name="moe_experts",
    )(g_expert, g_row, g_nsub, xs, moe_gate, moe_up, moe_down)


def _combine_kernel(dest_ref, ys_hbm, x_ref, rw_ref, gt_ref, fg_ref, o_ref, ybuf, sem, *, tokens, final):
    def row_copy(t, k):
        return pltpu.make_async_copy(ys_hbm.at[pl.ds(dest_ref[0, 2 * t + k], 1)], ybuf.at[k, pl.ds(t, 1)], sem)

    def start(t, carry):
        row_copy(t, 0).start()
        row_copy(t, 1).start()
        return carry

    def wait(t, carry):
        one_row = pltpu.make_async_copy(ys_hbm.at[pl.ds(0, 1)], ybuf.at[0, pl.ds(0, 1)], sem)
        one_row.wait()
        one_row.wait()
        return carry

    lax.fori_loop(0, tokens, start, 0, unroll=8)
    lax.fori_loop(0, tokens, wait, 0, unroll=8)
    rw = rw_ref[...]
    y = rw[:, 0:1] * ybuf[0] + rw[:, 1:2] * ybuf[1]
    xn = x_ref[...] + gt_ref[0] * y
    if final:
        xn = xn * lax.rsqrt(jnp.mean(xn * xn, axis=-1, keepdims=True) + EPS) * fg_ref[...]
    o_ref[...] = xn


def _moe_combine(lay, x, ys, dest, rw, modtab, final_g, l, final):
    tm = lay.tm
    full = lambda: pl.BlockSpec((tm, D_MODEL), lambda i: (i, 0))
    return pl.pallas_call(
        functools.partial(_combine_kernel, tokens=tm, final=final),
        out_shape=jax.ShapeDtypeStruct((lay.n, D_MODEL), F32),
        grid=(lay.n // tm,),
        in_specs=[pl.BlockSpec((None, 1, 2 * tm), lambda i: (i, 0, 0), memory_space=pltpu.SMEM),
                  pl.BlockSpec(memory_space=pl.ANY),
                  full(),
                  pl.BlockSpec((tm, LANES), lambda i: (i, 0)),
                  lay.mod_spec(l, 5, tm),
                  pl.BlockSpec((1, D_MODEL), lambda i: (0, 0))],
        out_specs=full(),
        scratch_shapes=[pltpu.VMEM((2, tm, D_MODEL), F32), pltpu.SemaphoreType.DMA],
        compiler_params=_params(("arbitrary",)),
        name="moe_combine",
    )(dest.reshape(lay.n // tm, 1, 2 * tm), ys, x, rw, modtab, final_g)


def kernel(x_prompt, x_sample, cache_k_ctx, cache_v_ctx, state_delta, c, c_ctx, w_mod, b_mod, norm_g, final_g,
           w_in, conv_w, a_log, dt_bias, rpb, o_gain, w_out, ffn_gate, ffn_up, ffn_down, pool_w, pool_scale,
           w_router, moe_gate, moe_up, moe_down):
    batch, seq, d = x_prompt.shape
    dec_batch, dec_seq, _ = x_sample.shape
    depth = w_mod.shape[0]
    n_even = w_in.shape[0]
    past = cache_k_ctx.shape[2]
    assert d == D_MODEL and depth % 2 == 0
    lay = _Layout(batch, seq, dec_batch, dec_seq)

    cond = jnp.zeros((lay.mod_rows, d), F32).at[0].set(c_ctx).at[1:1 + dec_batch].set(c)
    mod = _adaln(cond, w_mod, b_mod)
    modtab = mod.reshape(depth, lay.mod_rows, 6, d).transpose(0, 2, 1, 3).reshape(depth * 6 * lay.mod_rows, 1, d)
    normg = norm_g.reshape(depth * 2, 1, d)

    x = jnp.concatenate([x_prompt.reshape(lay.nc, d), x_sample.reshape(lay.nl, d)], axis=0)
    cache_k = cache_k_ctx.reshape(dec_batch, n_even, past, A_W)
    cache_v = cache_v_ctx.reshape(dec_batch, n_even, past, A_W)
    rows = dec_seq // GRID_W
    kh = min(NA_KH, rows)
    n_main = 3 * A_W + 4 * B_W
    pad128 = lambda a: jnp.pad(a.reshape(1, -1), ((0, 0), (0, LANES - a.size)))
    w_router_p = jnp.pad(w_router, ((0, 0), (0, 0), (0, LANES - N_EXP)))
    w_in16, w_out16 = _to_bf16(w_in), _to_bf16(w_out)
    ffn_gate, ffn_up, ffn_down = _to_bf16(ffn_gate), _to_bf16(ffn_up), _to_bf16(ffn_down)

    k_list, v_list, s_list = [], [], []
    for l in range(depth):
        i = l // 2
        if l % 2 == 0:
            w_ab = jnp.pad(w_in[i][:, n_main:], ((0, 0), (0, LANES - 4 * H_B)))
            proj, ab = _even_project(lay, x, normg, modtab, w_in16, w_ab, l)
            bias = _na_bias_table(rpb[i], rows, kh)
            oa_ctx, oa_lat = _attention(lay, proj, cache_k, cache_v, bias, i)
            gates, gates_t = _delta_gates(ab, pad128(a_log[i]), pad128(dt_bias[i]))
            qd, kd, vd = _delta_prep(lay, proj, conv_w, i)
            o_f, o_b, s_fin = _delta_net(lay, qd, kd, vd, gates, gates_t, state_delta[:, i])
            x, h2 = _even_output(lay, oa_ctx, oa_lat, o_f, o_b, proj, o_gain.reshape(n_even, 1, DV), w_out16, x,
                                 normg, modtab, l)
            x = _dense_ffn(lay, h2, x, ffn_gate, ffn_up, ffn_down, modtab, l)
            k_list.append(proj[:lay.nc, A_W:2 * A_W].reshape(batch, seq, H_A, DH_A))
            v_list.append(proj[:lay.nc, 2 * A_W:3 * A_W].reshape(batch, seq, H_A, DH_A))
            s_list.append(s_fin)
        else:
            x, h2, ridx, rw, counts = _pool_and_route(lay, x, normg, modtab, pool_w,
                                                      pool_scale.reshape(-1, 1, d), w_router_p, l)
            dest, g_expert, g_row, g_nsub = _moe_plan(ridx, counts, lay.n)
            xs = _moe_dispatch(h2, dest, 2 * lay.n + N_EXP * MOE_SUB)
            ys = _moe_experts(xs, g_expert, g_row, g_nsub, moe_gate, moe_up, moe_down, i)
            x = _moe_combine(lay, x, ys, dest, rw, modtab, final_g.reshape(1, d), l, final=(l == depth - 1))
    y_prompt = x[:lay.nc].reshape(batch, seq, d)
    y_sample = x[lay.nc:].reshape(dec_batch, dec_seq, d)
    return (y_prompt, y_sample, jnp.stack(k_list, axis=1), jnp.stack(v_list, axis=1), jnp.stack(s_list, axis=1))
```

```python
import functools
import math

import numpy as np
import jax
import jax.numpy as jnp
from jax import lax
from jax.experimental import pallas as pl
from jax.experimental.pallas import tpu as pltpu

F32 = jnp.float32
BF16 = jnp.bfloat16

D_MODEL = 1024
GRID_W = 64
DH_A = 64
H_A = 8
A_W = H_A * DH_A
NA_KH = 8
NA_KW = 16
DK = 128
DV = 128
H_B = 4
B_W = H_B * DK
CONV_K = 3
CHUNK = 64
POOL_WINDOWS = (2, 4, 8, 16)
POOL_G = D_MODEL // len(POOL_WINDOWS)
D_FF = 7 * D_MODEL // 2
N_EXP = 8
EPS = 1e-6
NEG_INF = -1e30

LANES = 128
SEG = 256
POOL_HALO = 64
FF_TILE = 512
PROJ_TILE = 896
MOE_SUB = 256
MOE_GROUP = 8
VMEM_LIMIT = 56 * 2 ** 20


def _params(sem, vmem=VMEM_LIMIT):
    return pltpu.CompilerParams(dimension_semantics=sem, vmem_limit_bytes=vmem)


def _silu(x):
    return x * jax.nn.sigmoid(x)


def _dot(a, b):
    return jnp.dot(a, b, preferred_element_type=F32)


def _dot_nt(a, b):
    return lax.dot_general(a, b, (((1,), (1,)), ((), ())), preferred_element_type=F32)


def _dot_f32x3(a01, x):
    x1 = x.astype(BF16)
    r1 = x - x1.astype(F32)
    x2 = r1.astype(BF16)
    x3 = (r1 - x2.astype(F32)).astype(BF16)
    return _dot(a01, x1) + _dot(a01, x2) + _dot(a01, x3)


def _dot_f32ish(a, b):
    a_hi, b_hi = a.astype(BF16), b.astype(BF16)
    a_lo, b_lo = (a - a_hi.astype(F32)).astype(BF16), (b - b_hi.astype(F32)).astype(BF16)
    return _dot(a_hi, b_hi) + (_dot(a_hi, b_lo) + _dot(a_lo, b_hi))


def _mask01(mask):
    return jnp.where(mask, 1.0, 0.0).astype(BF16)


def _modulate(x, g, shift, scale):
    y = x * lax.rsqrt(jnp.mean(x * x, axis=-1, keepdims=True) + EPS)
    return (y * g) * (1.0 + scale) + shift


def _cast_kernel(w_ref, o_ref):
    o_ref[...] = w_ref[...].astype(BF16)


def _to_bf16(w):
    layers, rows, cols = w.shape
    tr = 256
    assert rows % tr == 0
    spec = pl.BlockSpec((None, tr, cols), lambda l, r: (l, r, 0))
    return pl.pallas_call(
        _cast_kernel,
        out_shape=jax.ShapeDtypeStruct(w.shape, BF16),
        grid=(layers, rows // tr),
        in_specs=[spec],
        out_specs=spec,
        compiler_params=_params(("arbitrary", "arbitrary")),
        name="to_bf16",
    )(w)


def _adaln_kernel(cond_ref, w_ref, b_ref, o_ref):
    s = _silu(cond_ref[...]).astype(BF16)
    o_ref[0] = _dot(s, w_ref[0].astype(BF16)) + b_ref[0]


def _adaln(cond, w_mod, b_mod):
    depth, d, six_d = w_mod.shape
    r = cond.shape[0]
    tn = six_d // 4
    return pl.pallas_call(
        _adaln_kernel,
        out_shape=jax.ShapeDtypeStruct((depth, r, six_d), F32),
        grid=(depth, six_d // tn),
        in_specs=[
            pl.BlockSpec((r, d), lambda l, j: (0, 0)),
            pl.BlockSpec((1, d, tn), lambda l, j: (l, 0, j)),
            pl.BlockSpec((1, 1, tn), lambda l, j: (l, 0, j)),
        ],
        out_specs=pl.BlockSpec((1, r, tn), lambda l, j: (l, 0, j)),
        compiler_params=_params(("arbitrary", "arbitrary")),
        name="adaln",
    )(cond, w_mod, b_mod.reshape(depth, 1, six_d))


class _Layout:
    def __init__(self, batch, seq, dec_batch, dec_seq):
        self.batch, self.seq, self.dec_batch, self.dec_seq = batch, seq, dec_batch, dec_seq
        self.nc = batch * seq
        self.nl = dec_batch * dec_seq
        self.n = self.nc + self.nl
        self.mod_rows = -(-(1 + dec_batch) // 8) * 8
        assert seq % SEG == 0 and dec_seq % SEG == 0 and dec_seq % GRID_W == 0
        self.tm = math.gcd(self.nc, 1024)
        assert self.tm % SEG == 0 and dec_seq % self.tm == 0 and self.nc % dec_seq == 0

    def mod_row(self, first_row):
        return jnp.maximum((first_row - self.nc) // self.dec_seq + 1, 0)

    def mod_spec(self, l, k, rows_per_step):
        base = (l * 6 + k) * self.mod_rows
        return pl.BlockSpec((1, 1, D_MODEL),
                            lambda i, *_: (base + self.mod_row(i * rows_per_step), 0, 0))


def _row_spec(l, k):
    return pl.BlockSpec((1, 1, D_MODEL), lambda i, *_: (l * 2 + k, 0, 0))


def _proj_kernel(x_ref, g_ref, sh_ref, sc_ref, w_ref, wab_ref, proj_ref, ab_ref, h_scr):
    @pl.when(pl.program_id(1) == 0)
    def _():
        h = _modulate(x_ref[...], g_ref[0], sh_ref[0], sc_ref[0]).astype(BF16)
        h_scr[...] = h
        ab_ref[...] = _dot(h, wab_ref[...].astype(BF16))

    proj_ref[...] = _dot(h_scr[...], w_ref[...])


def _even_project(lay, x, normg, modtab, w_in, w_ab, l):
    i_even = l // 2
    tm, tn = lay.tm, PROJ_TILE
    ncols = 3 * A_W + 4 * B_W
    return pl.pallas_call(
        _proj_kernel,
        out_shape=(jax.ShapeDtypeStruct((lay.n, ncols), F32),
                   jax.ShapeDtypeStruct((lay.n, LANES), F32)),
        grid=(lay.n // tm, ncols // tn),
        in_specs=[
            pl.BlockSpec((tm, D_MODEL), lambda i, j: (i, 0)),
            _row_spec(l, 0),
            lay.mod_spec(l, 0, tm),
            lay.mod_spec(l, 1, tm),
            pl.BlockSpec((None, D_MODEL, tn), lambda i, j: (i_even, 0, j)),
            pl.BlockSpec((D_MODEL, LANES), lambda i, j: (0, 0)),
        ],
        out_specs=(pl.BlockSpec((tm, tn), lambda i, j: (i, j)),
                   pl.BlockSpec((tm, LANES), lambda i, j: (i, 0))),
        scratch_shapes=[pltpu.VMEM((tm, D_MODEL), BF16)],
        compiler_params=_params(("arbitrary", "arbitrary")),
        name="even_project",
    )(x, normg, modtab, modtab, w_in, w_ab)


def _attend_heads(q, keys, values, bias_of):
    scale = DH_A ** -0.5
    per = LANES // DH_A
    lane = lax.broadcasted_iota(jnp.int32, (1, LANES), 1)
    own = [(lane >= s * DH_A) & (lane < (s + 1) * DH_A) for s in range(per)]
    groups = range(A_W // LANES)
    blocks = range(len(keys))
    gsl = lambda g: slice(g * LANES, (g + 1) * LANES)
    k16 = [[k[:, gsl(g)].astype(BF16) for k in keys] for g in groups]
    v16 = [[v[:, gsl(g)].astype(BF16) for v in values] for g in groups]
    heads = [(g, s) for g in groups for s in range(per)]
    qh = [jnp.where(own[s], q[:, gsl(g)], 0.0).astype(BF16) for g, s in heads]
    scores = []
    for h, (g, s) in enumerate(heads):
        row = []
        for i in blocks:
            sc = _dot_nt(qh[h], k16[g][i]) * scale
            b = bias_of(h, i)
            row.append(sc if b is None else sc + b)
        scores.append(row)
    peak = [functools.reduce(jnp.maximum, [jnp.max(sc, axis=-1, keepdims=True) for sc in row]) for row in scores]
    probs = [[jnp.exp(sc - m) for sc in row] for row, m in zip(scores, peak)]
    denom = [functools.reduce(jnp.add, [jnp.sum(p, axis=-1, keepdims=True) for p in row]) for row in probs]
    acc = [functools.reduce(jnp.add, [_dot(p.astype(BF16), v16[g][i]) for i, p in enumerate(row)])
           for row, (g, s) in zip(probs, heads)]
    outs = []
    for g in groups:
        o = jnp.zeros_like(acc[0])
        for s in range(per):
            h = g * per + s
            o = jnp.where(own[s], acc[h] / denom[h], o)
        outs.append(o)
    return jnp.concatenate(outs, axis=1)


def _ctx_attn_kernel(q_ref, k_ref, v_ref, o_ref):
    o_ref[...] = _attend_heads(q_ref[...], [k_ref[...]], [v_ref[...]], lambda h, i: None)


def _na_attn_kernel(q_ref, k_ref, v_ref, kc_ref, vc_ref, bias_ref, o_ref, *, rows, kh):
    r = pl.program_id(1)
    row_lo = jnp.clip(r - kh // 2, 0, rows - kh)
    start = pl.multiple_of(row_lo * GRID_W, GRID_W)
    kl = k_ref[pl.ds(start, kh * GRID_W), :]
    vl = v_ref[pl.ds(start, kh * GRID_W), :]
    o_ref[...] = _attend_heads(q_ref[...], [kl, kc_ref[...]], [vl, vc_ref[...]],
                               lambda h, i: bias_ref[h, 0] if i == 0 else None)


def _na_bias_table(rpb, rows, kh):
    r = np.arange(rows)
    row_idx = np.clip(r - kh // 2, 0, rows - kh)[:, None] + np.arange(kh)[None, :]
    dr = row_idx - r[:, None] + (NA_KH - 1)
    qcol = np.arange(GRID_W)
    kcol = np.arange(GRID_W)
    col_lo = np.clip(qcol - NA_KW // 2, 0, GRID_W - NA_KW)
    valid = (kcol[None, :] >= col_lo[:, None]) & (kcol[None, :] < col_lo[:, None] + NA_KW)
    dc = np.clip(kcol[None, :] - qcol[:, None], 1 - NA_KW, NA_KW - 1) + (NA_KW - 1)
    onehot = (dc[None, :, :] == np.arange(2 * NA_KW - 1)[:, None, None]).astype(np.float32)
    picked = jnp.einsum("hrjc,cqk->hrqjk", rpb[:, dr].astype(F32), onehot, precision=lax.Precision.HIGHEST)
    bias = jnp.where(valid[None, None, :, None, :], picked, NEG_INF)
    return bias.reshape(H_A, rows, GRID_W, kh * GRID_W)


def _attention(lay, proj, cache_k, cache_v, bias, i_even):
    ctx = pl.pallas_call(
        _ctx_attn_kernel,
        out_shape=jax.ShapeDtypeStruct((lay.nc, A_W), F32),
        grid=(lay.batch,),
        in_specs=[pl.BlockSpec((lay.seq, A_W), lambda b: (b, 0)),
                  pl.BlockSpec((lay.seq, A_W), lambda b: (b, 1)),
                  pl.BlockSpec((lay.seq, A_W), lambda b: (b, 2))],
        out_specs=pl.BlockSpec((lay.seq, A_W), lambda b: (b, 0)),
        compiler_params=_params(("arbitrary",)),
        name="ctx_attention",
    )(proj, proj, proj)

    rows = lay.dec_seq // GRID_W
    kh = min(NA_KH, rows)
    past = cache_k.shape[2]
    q0 = lay.nc // GRID_W
    b0 = lay.nc // lay.dec_seq
    lat = pl.pallas_call(
        functools.partial(_na_attn_kernel, rows=rows, kh=kh),
        out_shape=jax.ShapeDtypeStruct((lay.nl, A_W), F32),
        grid=(lay.dec_batch, rows),
        in_specs=[pl.BlockSpec((GRID_W, A_W), lambda b, r: (q0 + b * rows + r, 0)),
                  pl.BlockSpec((lay.dec_seq, A_W), lambda b, r: (b0 + b, 1)),
                  pl.BlockSpec((lay.dec_seq, A_W), lambda b, r: (b0 + b, 2)),
                  pl.BlockSpec((None, None, past, A_W), lambda b, r: (b, i_even, 0, 0)),
                  pl.BlockSpec((None, None, past, A_W), lambda b, r: (b, i_even, 0, 0)),
                  pl.BlockSpec((H_A, 1, GRID_W, kh * GRID_W), lambda b, r: (0, r, 0, 0))],
        out_specs=pl.BlockSpec((GRID_W, A_W), lambda b, r: (b * rows + r, 0)),
        compiler_params=_params(("arbitrary", "arbitrary")),
        name="na_attention",
    )(proj, proj, proj, cache_k, cache_v, bias)
    return ctx, lat


def _seq_position(first, nc, seq, dec_seq):
    is_lat = first >= nc
    return jnp.where(is_lat, (first - nc) % dec_seq, first % seq), jnp.where(is_lat, dec_seq, seq)


def _delta_prep_kernel(xp_ref, xc_ref, xn_ref, w_ref, q_ref, k_ref, v_ref, *, nc, seq, dec_seq):
    off, t_len = _seq_position(pl.program_id(0) * SEG, nc, seq, dec_seq)
    x = xc_ref[...]
    row = lax.broadcasted_iota(jnp.int32, (SEG, 1), 0)
    before = jnp.where(off > 0, xp_ref[7:8, :], 0.0)
    after = jnp.where(off + SEG < t_len, xn_ref[0:1, :], 0.0)
    prev = jnp.where(row == 0, before, pltpu.roll(x, 1, 0))
    nxt = jnp.where(row == SEG - 1, after, pltpu.roll(x, SEG - 1, 0))
    w = w_ref[...]
    y = _silu(prev * w[0:1] + x * w[1:2] + nxt * w[2:3])

    def l2norm(a):
        return a * lax.rsqrt(jnp.sum(a * a, axis=-1, keepdims=True) + EPS)

    heads = lambda part: [y[:, (part * H_B + h) * DK:(part * H_B + h + 1) * DK] for h in range(H_B)]
    q_ref[...] = jnp.concatenate([l2norm(a) * (DK ** -0.5) for a in heads(0)], axis=1)
    k_ref[...] = jnp.concatenate([l2norm(a) for a in heads(1)], axis=1)
    v_ref[...] = y[:, 2 * B_W:]


def _delta_prep(lay, proj, conv_w, i_even):
    width = 3 * B_W
    cblk = 3 * A_W // width
    assert cblk * width == 3 * A_W
    per = SEG // 8
    n8 = lay.n // 8
    ospec = pl.BlockSpec((SEG, B_W), lambda i: (i, 0))
    shape = jax.ShapeDtypeStruct((lay.n, B_W), F32)
    return pl.pallas_call(
        functools.partial(_delta_prep_kernel, nc=lay.nc, seq=lay.seq, dec_seq=lay.dec_seq),
        out_shape=(shape, shape, shape),
        grid=(lay.n // SEG,),
        in_specs=[pl.BlockSpec((8, width), lambda i: (jnp.maximum(i * per - 1, 0), cblk)),
                  pl.BlockSpec((SEG, width), lambda i: (i, cblk)),
                  pl.BlockSpec((8, width), lambda i: (jnp.minimum((i + 1) * per, n8 - 1), cblk)),
                  pl.BlockSpec((None, CONV_K, width), lambda i: (i_even, 0, 0))],
        out_specs=(ospec, ospec, ospec),
        compiler_params=_params(("arbitrary",)),
        name="delta_prep",
    )(proj, proj, proj, conv_w)


GATE_GC, GATE_BETA, GATE_GT = 0, 2 * H_B, 4 * H_B


def _delta_gates_kernel(ab_ref, alog_ref, dtb_ref, g_ref, gt_ref):
    n = SEG
    ab = ab_ref[...]
    lane = lax.broadcasted_iota(jnp.int32, (1, LANES), 1)
    ri = lax.broadcasted_iota(jnp.int32, (n, n), 0)
    ci = lax.broadcasted_iota(jnp.int32, (n, n), 1)
    shift = int(math.log2(CHUNK))
    same = jnp.right_shift(ri, shift) == jnp.right_shift(ci, shift)
    g_all = -jnp.exp(alog_ref[...]) * jax.nn.softplus(ab + dtb_ref[...])
    prefix = _dot_f32x3(_mask01(same & (ci <= ri)), g_all)
    total = _dot_f32x3(_mask01(same), g_all)
    gc = jnp.where(lane < H_B, prefix, total - prefix + g_all)
    table = jnp.where(lane < GATE_BETA, gc,
                      jnp.where(lane < GATE_GT, jax.nn.sigmoid(ab), pltpu.roll(total, GATE_GT, 1)))
    g_ref[...] = table
    gt_ref[...] = table.T


def _delta_gates(ab, alog, dtb):
    n = ab.shape[0]
    row = pl.BlockSpec((1, LANES), lambda i: (0, 0))
    return pl.pallas_call(
        _delta_gates_kernel,
        out_shape=(jax.ShapeDtypeStruct((n, LANES), F32), jax.ShapeDtypeStruct((LANES, n), F32)),
        grid=(n // SEG,),
        in_specs=[pl.BlockSpec((SEG, LANES), lambda i: (i, 0)), row, row],
        out_specs=(pl.BlockSpec((SEG, LANES), lambda i: (i, 0)), pl.BlockSpec((LANES, SEG), lambda i: (0, i))),
        compiler_params=_params(("arbitrary",)),
        name="delta_gates",
    )(ab, alog, dtb)


def _delta_masks(d):
    n = SEG
    ri = lax.broadcasted_iota(jnp.int32, (n, n), 0)
    ci = lax.broadcasted_iota(jnp.int32, (n, n), 1)
    shift = int(math.log2(CHUNK))
    same = jnp.right_shift(ri, shift) == jnp.right_shift(ci, shift)
    incl = same & ((ci <= ri) if d == 0 else (ci >= ri))
    strict = same & ((ci < ri) if d == 0 else (ci > ri))
    levels = []
    for level in range(shift):
        bi, bj = jnp.right_shift(ri, level), jnp.right_shift(ci, level)
        siblings = jnp.right_shift(bi, 1) == jnp.right_shift(bj, 1)
        levels.append(siblings & ((bi > bj) if d == 0 else (bi < bj)))
    return incl, strict, jnp.where(ri == ci, 1.0, 0.0), levels


def _delta_chains(chains):
    n = SEG
    n_chunks = n // CHUNK
    each = lambda fn, *lists: [fn(*args) for args in zip(*lists)]
    ds, masks, qs, ks, vs, betas, gcs, gts, gc_rows, gt_rows, states = (list(t) for t in zip(*chains))
    incls, stricts = [m[0] for m in masks], [m[1] for m in masks]
    n_levels = len(masks[0][3])

    decays = each(lambda m, gc, gr: jnp.where(m, jnp.exp(jnp.where(m, gc - gr, 0.0)), 0.0), incls, gcs, gc_rows)
    kbs = each(lambda k, b: k * b, ks, betas)
    k16s = [k.astype(BF16) for k in ks]
    grams = each(lambda kb, k16: _dot_nt(kb.astype(BF16), k16), kbs, k16s)
    lmats = each(lambda m, g, dec: jnp.where(m, g * dec, 0.0), stricts, grams, decays)

    xs = each(lambda m, lm: m[2] - jnp.where(m[3][0], lm, 0.0), masks, lmats)
    for level in range(1, n_levels):
        x16s = [x.astype(BF16) for x in xs]
        cs = each(lambda m, lm: jnp.where(m[3][level], lm, 0.0).astype(BF16), masks, lmats)
        xcs = each(lambda x16, c: _dot(x16, c).astype(BF16), x16s, cs)
        xs = each(lambda x, xc, x16: x - _dot(xc, x16), xs, xcs, x16s)

    egs = [jnp.exp(gc) for gc in gcs]
    rhss = each(lambda v, b, kb, eg: jnp.concatenate([v * b, kb * eg], axis=1).astype(BF16), vs, betas, kbs, egs)
    uws = each(lambda x, rhs: _dot(x.astype(BF16), rhs), xs, rhss)
    qks = each(lambda q, k16: _dot_nt(q.astype(BF16), k16), qs, k16s)
    attns = each(lambda m, qk, dec: jnp.where(m, qk * dec, 0.0).astype(BF16), incls, qks, decays)
    qgs = each(lambda q, eg: (q * eg).astype(BF16), qs, egs)
    kd_ts = each(lambda k, gt_r, gc_r: (k.T * jnp.exp(gt_r - gc_r)).astype(BF16), ks, gt_rows, gc_rows)
    gls = [jnp.exp(gt) for gt in gts]

    v_new = [[None] * n_chunks for _ in chains]
    o_state = [[None] * n_chunks for _ in chains]
    for step in range(n_chunks):
        for i, d in enumerate(ds):
            c = step if d == 0 else n_chunks - 1 - step
            rs = slice(c * CHUNK, (c + 1) * CHUNK)
            s16 = states[i].astype(BF16)
            vn = uws[i][rs, :DV] - _dot(uws[i][rs, DV:].astype(BF16), s16)
            o_state[i][c] = _dot(qgs[i][rs], s16)
            v_new[i][c] = vn
            pieces = [jnp.zeros((c * CHUNK, DV), BF16), vn.astype(BF16),
                      jnp.zeros((n - (c + 1) * CHUNK, DV), BF16)]
            padded = jnp.concatenate([p for p in pieces if p.shape[0]], axis=0)
            states[i] = states[i] * gls[i][c * CHUNK:c * CHUNK + 1, :] + _dot(kd_ts[i], padded)
    outs = each(lambda os, a, vn: jnp.concatenate(os, axis=0) + _dot(a, jnp.concatenate(vn, axis=0).astype(BF16)),
                o_state, attns, v_new)
    return list(zip(outs, states))


def _delta_kernel(plan_ref, *refs):
    dir_refs = (refs[0:5], refs[5:10])
    s0_ref, of_ref, ob_ref, sfin_ref, s_scr = refs[10:]
    g = pl.program_id(0)
    first, last, has_s0 = plan_ref[2, g] == 1, plan_ref[3, g] == 1, plan_ref[4, g] == 1

    @pl.when(first)
    def _():
        s_scr[...] = jnp.where(has_s0, s0_ref[...], 0.0)

    chains = []
    for d, (q_ref, k_ref, v_ref, g_ref, gt_ref) in enumerate(dir_refs):
        masks = _delta_masks(d)
        q, k, v, gates, gates_t = q_ref[...], k_ref[...], v_ref[...], g_ref[...], gt_ref[...]
        for h in range(H_B):
            sl = slice(h * DK, (h + 1) * DK)
            col = d * H_B + h
            pick = lambda base: gates[:, base + col:base + col + 1]
            pick_t = lambda base: gates_t[base + col:base + col + 1, :]
            chains.append((d, masks, q[:, sl], k[:, sl], v[:, sl], pick(GATE_BETA), pick(GATE_GC),
                           pick(GATE_GT), pick_t(GATE_GC), pick_t(GATE_GT), s_scr[d, h]))
    results = _delta_chains(chains)
    for d, o_ref in enumerate((of_ref, ob_ref)):
        o_ref[...] = jnp.concatenate([results[d * H_B + h][0] for h in range(H_B)], axis=1)
        for h in range(H_B):
            s_scr[d, h] = results[d * H_B + h][1]

    @pl.when(last & jnp.logical_not(has_s0))
    def _():
        sfin_ref[...] = s_scr[...]


def _delta_plan(lay):
    rows = []
    for kind, n_seq, t in ((0, lay.batch, lay.seq), (1, lay.dec_batch, lay.dec_seq)):
        nseg = t // SEG
        base = 0 if kind == 0 else lay.nc // SEG
        for b in range(n_seq):
            for s in range(nseg):
                rows.append((base + b * nseg + s, base + b * nseg + nseg - 1 - s, int(s == 0), int(s == nseg - 1),
                             kind, b if kind else 0, b if kind == 0 else lay.batch - 1))
    return jnp.asarray(np.array(rows, np.int32).T)


def _delta_net(lay, q, k, v, gates, gates_t, s0):
    plan = _delta_plan(lay)
    in_specs, args = [], []
    for d in (0, 1):
        in_specs += [pl.BlockSpec((SEG, B_W), lambda g, plan, d=d: (plan[d, g], 0))] * 3
        in_specs += [pl.BlockSpec((SEG, LANES), lambda g, plan, d=d: (plan[d, g], 0)),
                     pl.BlockSpec((LANES, SEG), lambda g, plan, d=d: (0, plan[d, g]))]
        args += [q, k, v, gates, gates_t]
    state = lambda row: pl.BlockSpec((None, 2, H_B, DK, DV), lambda g, plan: (plan[row, g], 0, 0, 0, 0))
    oshape = jax.ShapeDtypeStruct((lay.n, B_W), F32)
    grid_spec = pltpu.PrefetchScalarGridSpec(
        num_scalar_prefetch=1,
        grid=(lay.n // SEG,),
        in_specs=in_specs + [state(5)],
        out_specs=(pl.BlockSpec((SEG, B_W), lambda g, plan: (plan[0, g], 0)),
                   pl.BlockSpec((SEG, B_W), lambda g, plan: (plan[1, g], 0)), state(6)),
        scratch_shapes=[pltpu.VMEM((2, H_B, DK, DV), F32)],
    )
    return pl.pallas_call(
        _delta_kernel,
        out_shape=(oshape, oshape, jax.ShapeDtypeStruct((lay.batch, 2, H_B, DK, DV), F32)),
        grid_spec=grid_spec,
        compiler_params=_params(("arbitrary",)),
        name="delta_net",
    )(plan, *args, s0)


def _even_out_kernel(oac_ref, oal_ref, of_ref, ob_ref, z_ref, gain_ref, w_ref, x_ref, gt_ref,
                     g2_ref, sh2_ref, sc2_ref, xo_ref, h2_ref, *, ctx_tiles):
    ob = of_ref[...] + ob_ref[...]
    z = z_ref[...]
    oa = jnp.where(pl.program_id(0) < ctx_tiles, oac_ref[...], oal_ref[...])
    parts = [oa.astype(BF16)]
    for h in range(H_B):
        sl = slice(h * DV, (h + 1) * DV)
        o_h = ob[:, sl]
        y = o_h * lax.rsqrt(jnp.mean(o_h * o_h, axis=-1, keepdims=True) + EPS) * gain_ref[...] * _silu(z[:, sl])
        parts.append(y.astype(BF16))
    mix = _dot(jnp.concatenate(parts, axis=1), w_ref[...])
    xn = x_ref[...] + gt_ref[0] * mix
    xo_ref[...] = xn
    h2_ref[...] = _modulate(xn, g2_ref[0], sh2_ref[0], sc2_ref[0]).astype(BF16)


def _even_output(lay, oa_ctx, oa_lat, o_f, o_b, proj, o_gain, w_out, x, normg, modtab, l):
    i_even = l // 2
    tm = lay.tm
    ctx_tiles = lay.nc // tm
    zcol = (3 * A_W + 3 * B_W) // B_W
    half = lambda: pl.BlockSpec((tm, B_W), lambda i: (i, 0))
    full = lambda: pl.BlockSpec((tm, D_MODEL), lambda i: (i, 0))
    return pl.pallas_call(
        functools.partial(_even_out_kernel, ctx_tiles=ctx_tiles),
        out_shape=(jax.ShapeDtypeStruct((lay.n, D_MODEL), F32),
                   jax.ShapeDtypeStruct((lay.n, D_MODEL), BF16)),
        grid=(lay.n // tm,),
        in_specs=[pl.BlockSpec((tm, A_W), lambda i: (jnp.minimum(i, ctx_tiles - 1), 0)),
                  pl.BlockSpec((tm, A_W), lambda i: (jnp.maximum(i - ctx_tiles, 0), 0)),
                  half(), half(),
                  pl.BlockSpec((tm, B_W), lambda i: (i, zcol)),
                  pl.BlockSpec((None, 1, DV), lambda i: (i_even, 0, 0)),
                  pl.BlockSpec((None, D_MODEL, D_MODEL), lambda i: (i_even, 0, 0)),
                  full(),
                  lay.mod_spec(l, 2, tm),
                  _row_spec(l, 1), lay.mod_spec(l, 3, tm), lay.mod_spec(l, 4, tm)],
        out_specs=(full(), full()),
        compiler_params=_params(("arbitrary",)),
        name="even_output",
    )(oa_ctx, oa_lat, o_f, o_b, proj, o_gain, w_out, x, modtab, normg, modtab, modtab)


def _ffn_kernel(h_ref, wg_ref, wu_ref, wd_ref, x_ref, gt_ref, o_ref, acc_ref):
    j = pl.program_id(1)

    @pl.when(j == 0)
    def _():
        acc_ref[...] = jnp.zeros_like(acc_ref)

    h = h_ref[...]
    a = (_silu(_dot(h, wg_ref[...])) * _dot(h, wu_ref[...])).astype(BF16)
    acc_ref[...] += _dot(a, wd_ref[...])

    @pl.when(j == pl.num_programs(1) - 1)
    def _():
        o_ref[...] = x_ref[...] + gt_ref[0] * acc_ref[...]


def _dense_ffn(lay, h2, x, ffn_gate, ffn_up, ffn_down, modtab, l):
    i_even = l // 2
    tm, tf = lay.tm, FF_TILE
    return pl.pallas_call(
        _ffn_kernel,
        out_shape=jax.ShapeDtypeStruct((lay.n, D_MODEL), F32),
        grid=(lay.n // tm, D_FF // tf),
        in_specs=[pl.BlockSpec((tm, D_MODEL), lambda i, j: (i, 0)),
                  pl.BlockSpec((None, D_MODEL, tf), lambda i, j: (i_even, 0, j)),
                  pl.BlockSpec((None, D_MODEL, tf), lambda i, j: (i_even, 0, j)),
                  pl.BlockSpec((None, tf, D_MODEL), lambda i, j: (i_even, j, 0)),
                  pl.BlockSpec((tm, D_MODEL), lambda i, j: (i, 0)),
                  lay.mod_spec(l, 5, tm)],
        out_specs=pl.BlockSpec((tm, D_MODEL), lambda i, j: (i, 0)),
        scratch_shapes=[pltpu.VMEM((tm, D_MODEL), F32)],
        compiler_params=_params(("arbitrary", "arbitrary")),
        name="dense_ffn",
    )(h2, ffn_gate, ffn_up, ffn_down, x, modtab)


def _pool_kernel(xp_ref, xc_ref, xn_ref, g1_ref, sh1_ref, sc1_ref, gt1_ref, pw_ref, ps_ref,
                 g2_ref, sh2_ref, sc2_ref, wr_ref, xo_ref, h2_ref, ridx_ref, rw_ref, cnt_ref, count_scr,
                 *, nc, seq, dec_seq):
    i = pl.program_id(0)
    off, t_len = _seq_position(i * SEG, nc, seq, dec_seq)

    g1, sh1, sc1 = g1_ref[0], sh1_ref[0], sc1_ref[0]
    x = xc_ref[...]
    h_cur = _modulate(x, g1, sh1, sc1)
    h_cat = jnp.concatenate([_modulate(xp_ref[...], g1, sh1, sc1), h_cur,
                             _modulate(xn_ref[...], g1, sh1, sc1)], axis=0)

    t = off + lax.broadcasted_iota(jnp.int32, (SEG, 1), 0)
    p = off - POOL_HALO + lax.broadcasted_iota(jnp.int32, (1, SEG + 2 * POOL_HALO), 1)
    mixes = []
    for gi, w in enumerate(POOL_WINDOWS):
        sl = slice(gi * POOL_G, (gi + 1) * POOL_G)
        lo = jnp.maximum(t - w // 2, 0)
        hi = jnp.minimum(t + (w - w // 2), t_len)
        band = _mask01((p >= lo) & (p < hi))
        hg = h_cat[:, sl]
        hg_hi = hg.astype(BF16)
        hg_lo = (hg - hg_hi.astype(F32)).astype(BF16)
        window_sum = _dot(band, hg_hi) + _dot(band, hg_lo)
        y = (window_sum / (hi - lo).astype(F32) - h_cur[:, sl]).astype(BF16)
        mixes.append(_dot(y, pw_ref[gi].astype(BF16)))
    mix = jnp.concatenate(mixes, axis=1) * ps_ref[...]
    xn = x + gt1_ref[0] * mix
    xo_ref[...] = xn
    h2 = _modulate(xn, g2_ref[0], sh2_ref[0], sc2_ref[0])
    h2_ref[...] = h2

    logits = _dot_f32ish(h2, wr_ref[...])
    lane = lax.broadcasted_iota(jnp.int32, logits.shape, 1)
    lane_f = lane.astype(F32)
    lg = jnp.where(lane < N_EXP, logits, -jnp.inf)
    m1 = jnp.max(lg, axis=-1, keepdims=True)
    i1 = jnp.min(jnp.where(lg == m1, lane_f, float(LANES)), axis=-1, keepdims=True)
    lg2 = jnp.where(lane_f == i1, -jnp.inf, lg)
    m2 = jnp.max(lg2, axis=-1, keepdims=True)
    i2 = jnp.min(jnp.where(lg2 == m2, lane_f, float(LANES)), axis=-1, keepdims=True)
    e = jnp.exp(m2 - m1)
    w1 = 1.0 / (1.0 + e)
    w2 = e / (1.0 + e)
    rw_ref[...] = jnp.where(lane == 0, w1, jnp.where(lane == 1, w2, 0.0))

    @pl.when(i == 0)
    def _():
        count_scr[...] = jnp.zeros_like(count_scr)

    hit1, hit2 = lane_f == i1, lane_f == i2
    picks = jnp.where(hit1 | hit2, 1.0, 0.0)
    ri = lax.broadcasted_iota(jnp.int32, (SEG, SEG), 0)
    ci = lax.broadcasted_iota(jnp.int32, (SEG, SEG), 1)
    before = _dot(_mask01(ci < ri), picks.astype(BF16)) + count_scr[...]
    r1 = jnp.sum(jnp.where(hit1, before, 0.0), axis=-1, keepdims=True)
    r2 = jnp.sum(jnp.where(hit2, before, 0.0), axis=-1, keepdims=True)
    count_scr[...] += jnp.sum(picks, axis=0, keepdims=True)
    route = jnp.where(lane == 0, i1, jnp.where(lane == 1, i2, jnp.where(lane == 2, r1, jnp.where(lane == 3, r2, 0.0))))
    ridx_ref[...] = route.astype(jnp.int32)
    cnt_ref[...] = jnp.broadcast_to(count_scr[...], cnt_ref.shape)


def _pool_and_route(lay, x, normg, modtab, pool_w, pool_scale, w_router, l):
    i_odd = l // 2
    nblk = lay.n // SEG
    blk = lambda imap: pl.BlockSpec((SEG, D_MODEL), imap)
    halo = lambda imap: pl.BlockSpec((POOL_HALO, D_MODEL), imap)
    per = SEG // POOL_HALO
    lane_blk = pl.BlockSpec((SEG, LANES), lambda i: (i, 0))
    return pl.pallas_call(
        functools.partial(_pool_kernel, nc=lay.nc, seq=lay.seq, dec_seq=lay.dec_seq),
        out_shape=(jax.ShapeDtypeStruct((lay.n, D_MODEL), F32),
                   jax.ShapeDtypeStruct((lay.n, D_MODEL), F32),
                   jax.ShapeDtypeStruct((lay.n, LANES), jnp.int32),
                   jax.ShapeDtypeStruct((lay.n, LANES), F32),
                   jax.ShapeDtypeStruct((8, LANES), F32)),
        grid=(nblk,),
        in_specs=[halo(lambda i: (jnp.maximum(i * per - 1, 0), 0)),
                  blk(lambda i: (i, 0)),
                  halo(lambda i: (jnp.minimum((i + 1) * per, nblk * per - 1), 0)),
                  _row_spec(l, 0), lay.mod_spec(l, 0, SEG), lay.mod_spec(l, 1, SEG), lay.mod_spec(l, 2, SEG),
                  pl.BlockSpec((None, len(POOL_WINDOWS), POOL_G, POOL_G), lambda i: (i_odd, 0, 0, 0)),
                  pl.BlockSpec((None, 1, D_MODEL), lambda i: (i_odd, 0, 0)),
                  _row_spec(l, 1), lay.mod_spec(l, 3, SEG), lay.mod_spec(l, 4, SEG),
                  pl.BlockSpec((None, D_MODEL, LANES), lambda i: (i_odd, 0, 0))],
        out_specs=(blk(lambda i: (i, 0)), blk(lambda i: (i, 0)), lane_blk, lane_blk,
                   pl.BlockSpec((8, LANES), lambda i: (0, 0))),
        scratch_shapes=[pltpu.VMEM((1, LANES), F32)],
        compiler_params=_params(("arbitrary",)),
        name="pool_route",
    )(x, x, x, normg, modtab, modtab, modtab, pool_w, pool_scale, normg, modtab, modtab, w_router)


def _dispatch_kernel(dest_ref, h_ref, zeros_hbm, xs_hbm, sem, *, tokens):
    del zeros_hbm

    def row_copy(t, k):
        return pltpu.make_async_copy(h_ref.at[pl.ds(t, 1)], xs_hbm.at[pl.ds(dest_ref[0, 2 * t + k], 1)], sem)

    def start(t, carry):
        row_copy(t, 0).start()
        row_copy(t, 1).start()
        return carry

    def wait(t, carry):
        one_row = pltpu.make_async_copy(h_ref.at[pl.ds(0, 1)], xs_hbm.at[pl.ds(0, 1)], sem)
        one_row.wait()
        one_row.wait()
        return carry

    lax.fori_loop(0, tokens, start, 0, unroll=16)
    lax.fori_loop(0, tokens, wait, 0, unroll=16)


def _moe_dispatch(h2, dest, p_max):
    n = h2.shape[0]
    tokens = math.gcd(n, 1024)
    zeros = jnp.zeros((p_max, D_MODEL), F32)
    return pl.pallas_call(
        functools.partial(_dispatch_kernel, tokens=tokens),
        out_shape=jax.ShapeDtypeStruct((p_max, D_MODEL), F32),
        grid=(n // tokens,),
        in_specs=[pl.BlockSpec((None, 1, 2 * tokens), lambda i: (i, 0, 0), memory_space=pltpu.SMEM),
                  pl.BlockSpec((tokens, D_MODEL), lambda i: (i, 0)),
                  pl.BlockSpec(memory_space=pl.ANY)],
        out_specs=pl.BlockSpec(memory_space=pl.ANY),
        scratch_shapes=[pltpu.SemaphoreType.DMA],
        input_output_aliases={2: 0},
        compiler_params=_params(("arbitrary",)),
        name="moe_dispatch",
    )(dest.reshape(n // tokens, 1, 2 * tokens), h2, zeros)


def _moe_kernel(ge_ref, gs_ref, gn_ref, xs_hbm, wg_ref, wu_ref, wd_ref, ys_hbm, xbuf, x16, acc, sem_in, sem_out):
    del ge_ref
    g = pl.program_id(0)
    j = pl.program_id(1)
    last_j = pl.num_programs(1) - 1
    nsub = gn_ref[g]
    row0 = gs_ref[g]

    def in_copy(s):
        rows = pl.ds(pl.multiple_of(row0 + s * MOE_SUB, MOE_SUB), MOE_SUB)
        return pltpu.make_async_copy(xs_hbm.at[rows], xbuf.at[pl.ds(pl.multiple_of(s * MOE_SUB, MOE_SUB), MOE_SUB)],
                                     sem_in.at[s])

    def out_copy(first, n_rows):
        rows = pl.ds(pl.multiple_of(row0 + first, MOE_SUB), n_rows)
        return pltpu.make_async_copy(acc.at[pl.ds(pl.multiple_of(first, MOE_SUB), n_rows)], ys_hbm.at[rows], sem_out)

    def loop(count, fn):
        def body(s, carry):
            fn(s)
            return carry
        lax.fori_loop(0, count, body, 0)

    @pl.when(j == 0)
    def _():
        loop(nsub, lambda s: in_copy(s).start())

    wg = wg_ref[...].astype(BF16)
    wu = wu_ref[...].astype(BF16)
    wd = wd_ref[...].astype(BF16)

    def tile(first_sub, n_sub):
        n_rows = n_sub * MOE_SUB
        first = pl.multiple_of(first_sub * MOE_SUB, MOE_SUB)
        rows = pl.ds(first, n_rows)

        @pl.when(j == 0)
        def _():
            for k in range(n_sub):
                in_copy(first_sub + k).wait()
            x16[rows, :] = xbuf[rows, :].astype(BF16)

        xs = x16[rows, :]
        a = (_silu(_dot(xs, wg)) * _dot(xs, wu)).astype(BF16)
        contrib = _dot(a, wd)

        @pl.when(j == 0)
        def _():
            acc[rows, :] = contrib

        @pl.when(j > 0)
        def _():
            acc[rows, :] += contrib

        @pl.when(j == last_j)
        def _():
            out_copy(first, n_rows).start()

    n_quads = lax.shift_right_logical(nsub, 2)
    has_two = (nsub & 2) == 2
    has_one = (nsub & 1) == 1
    loop(n_quads, lambda p: tile(4 * p, 4))

    @pl.when(has_two)
    def _():
        tile(4 * n_quads, 2)

    @pl.when(has_one)
    def _():
        tile(nsub - 1, 1)

    @pl.when(j == last_j)
    def _():
        loop(n_quads, lambda p: out_copy(4 * p * MOE_SUB, 4 * MOE_SUB).wait())

        @pl.when(has_two)
        def _():
            out_copy(4 * n_quads * MOE_SUB, 2 * MOE_SUB).wait()

        @pl.when(has_one)
        def _():
            out_copy((nsub - 1) * MOE_SUB, MOE_SUB).wait()


def _moe_plan(ridx, counts, n_tok):
    n_pairs = 2 * n_tok
    group_rows = MOE_SUB * MOE_GROUP
    g_max = -(-n_pairs // group_rows) + N_EXP
    counts = counts[0, :N_EXP].astype(jnp.int32)
    nsub_e = (counts + MOE_SUB - 1) // MOE_SUB
    region = nsub_e * MOE_SUB
    start_e = jnp.cumsum(region) - region
    expert = ridx[:, 0:2]
    is_e = expert[:, :, None] == jnp.arange(N_EXP, dtype=jnp.int32)[None, None, :]
    dest = jnp.sum(jnp.where(is_e, start_e[None, None, :], 0), axis=-1) + ridx[:, 2:4]

    ngrp_e = (nsub_e + MOE_GROUP - 1) // MOE_GROUP
    gend = jnp.cumsum(ngrp_e)
    gstart = gend - ngrp_e
    total = gend[-1]
    gid = jnp.arange(g_max, dtype=jnp.int32)
    ge = jnp.minimum(jnp.searchsorted(gend, gid, side="right").astype(jnp.int32), N_EXP - 1)
    kk = gid - gstart[ge]
    live = gid < total
    last_e = jnp.minimum(jnp.searchsorted(gend, total - 1, side="right").astype(jnp.int32), N_EXP - 1)
    g_expert = jnp.where(live, ge, last_e)
    g_row = jnp.where(live, start_e[ge] + kk * group_rows, 0)
    g_nsub = jnp.where(live, jnp.clip(nsub_e[ge] - kk * MOE_GROUP, 0, MOE_GROUP), 0)
    return dest, g_expert.astype(jnp.int32), g_row.astype(jnp.int32), g_nsub.astype(jnp.int32)


def _moe_experts(xs, g_expert, g_row, g_nsub, moe_gate, moe_up, moe_down, i_odd):
    p_max = xs.shape[0]
    g_max = g_expert.shape[0]
    tf = FF_TILE
    n_j = D_FF // tf
    group_rows = MOE_SUB * MOE_GROUP
    tile_of = lambda g, j, gn: jnp.where(gn[g] > 0, j, n_j - 1)
    grid_spec = pltpu.PrefetchScalarGridSpec(
        num_scalar_prefetch=3,
        grid=(g_max, n_j),
        in_specs=[pl.BlockSpec(memory_space=pl.ANY),
                  pl.BlockSpec((None, None, D_MODEL, tf),
                               lambda g, j, ge, gs, gn: (i_odd, ge[g], 0, tile_of(g, j, gn))),
                  pl.BlockSpec((None, None, D_MODEL, tf),
                               lambda g, j, ge, gs, gn: (i_odd, ge[g], 0, tile_of(g, j, gn))),
                  pl.BlockSpec((None, None, tf, D_MODEL),
                               lambda g, j, ge, gs, gn: (i_odd, ge[g], tile_of(g, j, gn), 0))],
        out_specs=pl.BlockSpec(memory_space=pl.ANY),
        scratch_shapes=[pltpu.VMEM((group_rows, D_MODEL), F32),
                        pltpu.VMEM((group_rows, D_MODEL), BF16),
                        pltpu.VMEM((group_rows, D_MODEL), F32),
                        pltpu.SemaphoreType.DMA((MOE_GROUP,)), pltpu.SemaphoreType.DMA],
    )
    return pl.pallas_call(
        _moe_kernel,
        out_shape=jax.ShapeDtypeStruct((p_max, D_MODEL), F32),
        grid_spec=grid_spec,
        input_output_aliases={3: 0},
        compiler_params=_params(("arbitrary", "arbitrary")),
        name="moe_experts",
    )(g_expert, g_row, g_nsub, xs, moe_gate, moe_up, moe_down)


def _combine_kernel(dest_ref, ys_hbm, x_ref, rw_ref, gt_ref, fg_ref, *rest, tokens, ctx_tiles):
    out_refs, (ybuf, sem) = rest[:-2], rest[-2:]
    def row_copy(t, k):
        return pltpu.make_async_copy(ys_hbm.at[pl.ds(dest_ref[0, 2 * t + k], 1)], ybuf.at[k, pl.ds(t, 1)], sem)

    def start(t, carry):
        row_copy(t, 0).start()
        row_copy(t, 1).start()
        return carry

    def wait(t, carry):
        one_row = pltpu.make_async_copy(ys_hbm.at[pl.ds(0, 1)], ybuf.at[0, pl.ds(0, 1)], sem)
        one_row.wait()
        one_row.wait()
        return carry

    lax.fori_loop(0, tokens, start, 0, unroll=16)
    lax.fori_loop(0, tokens, wait, 0, unroll=16)
    rw = rw_ref[...]
    y = rw[:, 0:1] * ybuf[0] + rw[:, 1:2] * ybuf[1]
    xn = x_ref[...] + gt_ref[0] * y
    if ctx_tiles is None:
        out_refs[0][...] = xn
        return
    xn = xn * lax.rsqrt(jnp.mean(xn * xn, axis=-1, keepdims=True) + EPS) * fg_ref[...]
    is_ctx = pl.program_id(0) < ctx_tiles

    @pl.when(is_ctx)
    def _():
        out_refs[0][...] = xn

    @pl.when(jnp.logical_not(is_ctx))
    def _():
        out_refs[1][...] = xn


def _moe_combine(lay, x, ys, dest, rw, modtab, final_g, l, final):
    tm = lay.tm
    full = lambda: pl.BlockSpec((tm, D_MODEL), lambda i: (i, 0))
    ctx_tiles = lay.nc // tm if final else None
    if final:
        out_shape = (jax.ShapeDtypeStruct((lay.nc, D_MODEL), F32), jax.ShapeDtypeStruct((lay.nl, D_MODEL), F32))
        out_specs = (pl.BlockSpec((tm, D_MODEL), lambda i: (jnp.minimum(i, ctx_tiles - 1), 0)),
                     pl.BlockSpec((tm, D_MODEL), lambda i: (jnp.maximum(i - ctx_tiles, 0), 0)))
    else:
        out_shape, out_specs = jax.ShapeDtypeStruct((lay.n, D_MODEL), F32), full()
    return pl.pallas_call(
        functools.partial(_combine_kernel, tokens=tm, ctx_tiles=ctx_tiles),
        out_shape=out_shape,
        grid=(lay.n // tm,),
        in_specs=[pl.BlockSpec((None, 1, 2 * tm), lambda i: (i, 0, 0), memory_space=pltpu.SMEM),
                  pl.BlockSpec(memory_space=pl.ANY),
                  full(),
                  pl.BlockSpec((tm, LANES), lambda i: (i, 0)),
                  lay.mod_spec(l, 5, tm),
                  pl.BlockSpec((1, D_MODEL), lambda i: (0, 0))],
        out_specs=out_specs,
        scratch_shapes=[pltpu.VMEM((2, tm, D_MODEL), F32), pltpu.SemaphoreType.DMA],
        compiler_params=_params(("arbitrary",)),
        name="moe_combine",
    )(dest.reshape(lay.n // tm, 1, 2 * tm), ys, x, rw, modtab, final_g)


def kernel(x_prompt, x_sample, cache_k_ctx, cache_v_ctx, state_delta, c, c_ctx, w_mod, b_mod, norm_g, final_g,
           w_in, conv_w, a_log, dt_bias, rpb, o_gain, w_out, ffn_gate, ffn_up, ffn_down, pool_w, pool_scale,
           w_router, moe_gate, moe_up, moe_down):
    batch, seq, d = x_prompt.shape
    dec_batch, dec_seq, _ = x_sample.shape
    depth = w_mod.shape[0]
    n_even = w_in.shape[0]
    past = cache_k_ctx.shape[2]
    assert d == D_MODEL and depth % 2 == 0
    lay = _Layout(batch, seq, dec_batch, dec_seq)

    cond = jnp.zeros((lay.mod_rows, d), F32).at[0].set(c_ctx).at[1:1 + dec_batch].set(c)
    mod = _adaln(cond, w_mod, b_mod)
    modtab = mod.reshape(depth, lay.mod_rows, 6, d).transpose(0, 2, 1, 3).reshape(depth * 6 * lay.mod_rows, 1, d)
    normg = norm_g.reshape(depth * 2, 1, d)

    x = jnp.concatenate([x_prompt.reshape(lay.nc, d), x_sample.reshape(lay.nl, d)], axis=0)
    cache_k = cache_k_ctx.reshape(dec_batch, n_even, past, A_W)
    cache_v = cache_v_ctx.reshape(dec_batch, n_even, past, A_W)
    rows = dec_seq // GRID_W
    kh = min(NA_KH, rows)
    n_main = 3 * A_W + 4 * B_W
    pad128 = lambda a: jnp.pad(a.reshape(1, -1), ((0, 0), (0, LANES - a.size)))
    w_router_p = jnp.pad(w_router, ((0, 0), (0, 0), (0, LANES - N_EXP)))
    w_in16, w_out16 = _to_bf16(w_in), _to_bf16(w_out)
    ffn_gate, ffn_up, ffn_down = _to_bf16(ffn_gate), _to_bf16(ffn_up), _to_bf16(ffn_down)

    k_list, v_list, s_list = [], [], []
    for l in range(depth):
        i = l // 2
        if l % 2 == 0:
            w_ab = jnp.pad(w_in[i][:, n_main:], ((0, 0), (0, LANES - 4 * H_B)))
            proj, ab = _even_project(lay, x, normg, modtab, w_in16, w_ab, l)
            bias = _na_bias_table(rpb[i], rows, kh)
            oa_ctx, oa_lat = _attention(lay, proj, cache_k, cache_v, bias, i)
            gates, gates_t = _delta_gates(ab, pad128(a_log[i]), pad128(dt_bias[i]))
            qd, kd, vd = _delta_prep(lay, proj, conv_w, i)
            o_f, o_b, s_fin = _delta_net(lay, qd, kd, vd, gates, gates_t, state_delta[:, i])
            x, h2 = _even_output(lay, oa_ctx, oa_lat, o_f, o_b, proj, o_gain.reshape(n_even, 1, DV), w_out16, x,
                                 normg, modtab, l)
            x = _dense_ffn(lay, h2, x, ffn_gate, ffn_up, ffn_down, modtab, l)
            k_list.append(proj[:lay.nc, A_W:2 * A_W].reshape(batch, seq, H_A, DH_A))
            v_list.append(proj[:lay.nc, 2 * A_W:3 * A_W].reshape(batch, seq, H_A, DH_A))
            s_list.append(s_fin)
        else:
            x, h2, ridx, rw, counts = _pool_and_route(lay, x, normg, modtab, pool_w,
                                                      pool_scale.reshape(-1, 1, d), w_router_p, l)
            dest, g_expert, g_row, g_nsub = _moe_plan(ridx, counts, lay.n)
            xs = _moe_dispatch(h2, dest, 2 * lay.n + N_EXP * MOE_SUB)
            ys = _moe_experts(xs, g_expert, g_row, g_nsub, moe_gate, moe_up, moe_down, i)
            x = _moe_combine(lay, x, ys, dest, rw, modtab, final_g.reshape(1, d), l, final=(l == depth - 1))
    y_ctx, y_lat = x
    return (y_ctx.reshape(batch, seq, d), y_lat.reshape(dec_batch, dec_seq, d), jnp.stack(k_list, axis=1), jnp.stack(v_list, axis=1), jnp.stack(s_list, axis=1))
```

```python
import functools
import math

import numpy as np
import jax
import jax.numpy as jnp
from jax import lax
from jax.experimental import pallas as pl
from jax.experimental.pallas import tpu as pltpu

F32 = jnp.float32
BF16 = jnp.bfloat16

D_MODEL = 1024
GRID_W = 64
DH_A = 64
H_A = 8
A_W = H_A * DH_A
NA_KH = 8
NA_KW = 16
DK = 128
DV = 128
H_B = 4
B_W = H_B * DK
CONV_K = 3
CHUNK = 64
POOL_WINDOWS = (2, 4, 8, 16)
POOL_G = D_MODEL // len(POOL_WINDOWS)
D_FF = 7 * D_MODEL // 2
N_EXP = 8
EPS = 1e-6
NEG_INF = -1e30

LANES = 128
SEG = 256
POOL_HALO = 64
FF_TILE = 512
PROJ_TILE = 896
MOE_SUB = 256
MOE_GROUP = 8
VMEM_LIMIT = 56 * 2 ** 20


def _params(sem, vmem=VMEM_LIMIT):
    return pltpu.CompilerParams(dimension_semantics=sem, vmem_limit_bytes=vmem)


def _silu(x):
    return x * jax.nn.sigmoid(x)


def _dot(a, b):
    return jnp.dot(a, b, preferred_element_type=F32)


def _dot_nt(a, b):
    return lax.dot_general(a, b, (((1,), (1,)), ((), ())), preferred_element_type=F32)


def _dot_f32x3(a01, x):
    x1 = x.astype(BF16)
    r1 = x - x1.astype(F32)
    x2 = r1.astype(BF16)
    x3 = (r1 - x2.astype(F32)).astype(BF16)
    return _dot(a01, x1) + _dot(a01, x2) + _dot(a01, x3)


def _dot_f32ish(a, b):
    a_hi, b_hi = a.astype(BF16), b.astype(BF16)
    a_lo, b_lo = (a - a_hi.astype(F32)).astype(BF16), (b - b_hi.astype(F32)).astype(BF16)
    return _dot(a_hi, b_hi) + (_dot(a_hi, b_lo) + _dot(a_lo, b_hi))


def _mask01(mask):
    return jnp.where(mask, 1.0, 0.0).astype(BF16)


def _modulate(x, g, shift, scale):
    y = x * lax.rsqrt(jnp.mean(x * x, axis=-1, keepdims=True) + EPS)
    return (y * g) * (1.0 + scale) + shift


def _cast_kernel(w_ref, o_ref):
    o_ref[...] = w_ref[...].astype(BF16)


def _to_bf16(w):
    layers, rows, cols = w.shape
    tr = 256
    assert rows % tr == 0
    spec = pl.BlockSpec((None, tr, cols), lambda l, r: (l, r, 0))
    return pl.pallas_call(
        _cast_kernel,
        out_shape=jax.ShapeDtypeStruct(w.shape, BF16),
        grid=(layers, rows // tr),
        in_specs=[spec],
        out_specs=spec,
        compiler_params=_params(("arbitrary", "arbitrary")),
        name="to_bf16",
    )(w)


def _adaln_kernel(cond_ref, w_ref, b_ref, o_ref):
    s = _silu(cond_ref[...]).astype(BF16)
    o_ref[0] = _dot(s, w_ref[0].astype(BF16)) + b_ref[0]


def _adaln(cond, w_mod, b_mod):
    depth, d, six_d = w_mod.shape
    r = cond.shape[0]
    tn = six_d // 4
    return pl.pallas_call(
        _adaln_kernel,
        out_shape=jax.ShapeDtypeStruct((depth, r, six_d), F32),
        grid=(depth, six_d // tn),
        in_specs=[
            pl.BlockSpec((r, d), lambda l, j: (0, 0)),
            pl.BlockSpec((1, d, tn), lambda l, j: (l, 0, j)),
            pl.BlockSpec((1, 1, tn), lambda l, j: (l, 0, j)),
        ],
        out_specs=pl.BlockSpec((1, r, tn), lambda l, j: (l, 0, j)),
        compiler_params=_params(("arbitrary", "arbitrary")),
        name="adaln",
    )(cond, w_mod, b_mod.reshape(depth, 1, six_d))


class _Layout:
    def __init__(self, batch, seq, dec_batch, dec_seq):
        self.batch, self.seq, self.dec_batch, self.dec_seq = batch, seq, dec_batch, dec_seq
        self.nc = batch * seq
        self.nl = dec_batch * dec_seq
        self.n = self.nc + self.nl
        self.mod_rows = -(-(1 + dec_batch) // 8) * 8
        assert seq % SEG == 0 and dec_seq % SEG == 0 and dec_seq % GRID_W == 0
        self.tm = math.gcd(self.nc, 1024)
        assert self.tm % SEG == 0 and dec_seq % self.tm == 0 and self.nc % dec_seq == 0

    def mod_row(self, first_row):
        return jnp.maximum((first_row - self.nc) // self.dec_seq + 1, 0)

    def stream_specs(self, parts, tm):
        ct = self.nc // tm
        off = ct if len(parts) == 1 else 0
        specs = (pl.BlockSpec((tm, D_MODEL), lambda i, *_: (jnp.minimum(i, ct - 1), 0)),
                 pl.BlockSpec((tm, D_MODEL), lambda i, *_: (off + jnp.maximum(i - ct, 0), 0)))
        return specs, (parts[0], parts[-1])

    def mod_spec(self, l, k, rows_per_step):
        base = (l * 6 + k) * self.mod_rows
        return pl.BlockSpec((1, 1, D_MODEL),
                            lambda i, *_: (base + self.mod_row(i * rows_per_step), 0, 0))


def _stream_tile(xa_ref, xb_ref, ctx_tiles):
    return jnp.where(pl.program_id(0) < ctx_tiles, xa_ref[...], xb_ref[...])


def _row_spec(l, k):
    return pl.BlockSpec((1, 1, D_MODEL), lambda i, *_: (l * 2 + k, 0, 0))


def _proj_kernel(xa_ref, xb_ref, g_ref, sh_ref, sc_ref, w_ref, wab_ref, proj_ref, ab_ref, h_scr, *, ctx_tiles):
    @pl.when(pl.program_id(1) == 0)
    def _():
        h = _modulate(_stream_tile(xa_ref, xb_ref, ctx_tiles), g_ref[0], sh_ref[0], sc_ref[0]).astype(BF16)
        h_scr[...] = h
        ab_ref[...] = _dot(h, wab_ref[...].astype(BF16))

    proj_ref[...] = _dot(h_scr[...], w_ref[...])


def _even_project(lay, x_parts, normg, modtab, w_in, w_ab, l):
    i_even = l // 2
    tm, tn = lay.tm, PROJ_TILE
    ncols = 3 * A_W + 4 * B_W
    x_specs, x_args = lay.stream_specs(x_parts, tm)
    return pl.pallas_call(
        functools.partial(_proj_kernel, ctx_tiles=lay.nc // tm),
        out_shape=(jax.ShapeDtypeStruct((lay.n, ncols), F32),
                   jax.ShapeDtypeStruct((lay.n, LANES), F32)),
        grid=(lay.n // tm, ncols // tn),
        in_specs=[
            *x_specs,
            _row_spec(l, 0),
            lay.mod_spec(l, 0, tm),
            lay.mod_spec(l, 1, tm),
            pl.BlockSpec((None, D_MODEL, tn), lambda i, j: (i_even, 0, j)),
            pl.BlockSpec((D_MODEL, LANES), lambda i, j: (0, 0)),
        ],
        out_specs=(pl.BlockSpec((tm, tn), lambda i, j: (i, j)),
                   pl.BlockSpec((tm, LANES), lambda i, j: (i, 0))),
        scratch_shapes=[pltpu.VMEM((tm, D_MODEL), BF16)],
        compiler_params=_params(("arbitrary", "arbitrary")),
        name="even_project",
    )(*x_args, normg, modtab, modtab, w_in, w_ab)


def _attend_heads(q, keys, values, bias_of):
    scale = DH_A ** -0.5
    per = LANES // DH_A
    lane = lax.broadcasted_iota(jnp.int32, (1, LANES), 1)
    own = [(lane >= s * DH_A) & (lane < (s + 1) * DH_A) for s in range(per)]
    groups = range(A_W // LANES)
    blocks = range(len(keys))
    gsl = lambda g: slice(g * LANES, (g + 1) * LANES)
    k16 = [[k[:, gsl(g)].astype(BF16) for k in keys] for g in groups]
    v16 = [[v[:, gsl(g)].astype(BF16) for v in values] for g in groups]
    heads = [(g, s) for g in groups for s in range(per)]
    qh = [jnp.where(own[s], q[:, gsl(g)], 0.0).astype(BF16) for g, s in heads]
    scores = []
    for h, (g, s) in enumerate(heads):
        row = []
        for i in blocks:
            sc = _dot_nt(qh[h], k16[g][i]) * scale
            b = bias_of(h, i)
            row.append(sc if b is None else sc + b)
        scores.append(row)
    peak = [functools.reduce(jnp.maximum, [jnp.max(sc, axis=-1, keepdims=True) for sc in row]) for row in scores]
    probs = [[jnp.exp(sc - m) for sc in row] for row, m in zip(scores, peak)]
    denom = [functools.reduce(jnp.add, [jnp.sum(p, axis=-1, keepdims=True) for p in row]) for row in probs]
    acc = [functools.reduce(jnp.add, [_dot(p.astype(BF16), v16[g][i]) for i, p in enumerate(row)])
           for row, (g, s) in zip(probs, heads)]
    outs = []
    for g in groups:
        o = jnp.zeros_like(acc[0])
        for s in range(per):
            h = g * per + s
            o = jnp.where(own[s], acc[h] / denom[h], o)
        outs.append(o)
    return jnp.concatenate(outs, axis=1)


def _ctx_attn_kernel(q_ref, k_ref, v_ref, o_ref):
    o_ref[...] = _attend_heads(q_ref[...], [k_ref[...]], [v_ref[...]], lambda h, i: None)


def _na_attn_kernel(q_ref, k_ref, v_ref, kc_ref, vc_ref, bias_ref, o_ref, *, rows, kh):
    r = pl.program_id(1)
    row_lo = jnp.clip(r - kh // 2, 0, rows - kh)
    start = pl.multiple_of(row_lo * GRID_W, GRID_W)
    kl = k_ref[pl.ds(start, kh * GRID_W), :]
    vl = v_ref[pl.ds(start, kh * GRID_W), :]
    o_ref[...] = _attend_heads(q_ref[...], [kl, kc_ref[...]], [vl, vc_ref[...]],
                               lambda h, i: bias_ref[h, 0] if i == 0 else None)


def _na_bias_table(rpb, rows, kh):
    r = np.arange(rows)
    row_idx = np.clip(r - kh // 2, 0, rows - kh)[:, None] + np.arange(kh)[None, :]
    dr = row_idx - r[:, None] + (NA_KH - 1)
    qcol = np.arange(GRID_W)
    kcol = np.arange(GRID_W)
    col_lo = np.clip(qcol - NA_KW // 2, 0, GRID_W - NA_KW)
    valid = (kcol[None, :] >= col_lo[:, None]) & (kcol[None, :] < col_lo[:, None] + NA_KW)
    dc = np.clip(kcol[None, :] - qcol[:, None], 1 - NA_KW, NA_KW - 1) + (NA_KW - 1)
    onehot = (dc[None, :, :] == np.arange(2 * NA_KW - 1)[:, None, None]).astype(np.float32)
    picked = jnp.einsum("hrjc,cqk->hrqjk", rpb[:, dr].astype(F32), onehot, precision=lax.Precision.HIGHEST)
    bias = jnp.where(valid[None, None, :, None, :], picked, NEG_INF)
    return bias.reshape(H_A, rows, GRID_W, kh * GRID_W)


def _attention(lay, proj, cache_k, cache_v, bias, i_even):
    ctx = pl.pallas_call(
        _ctx_attn_kernel,
        out_shape=jax.ShapeDtypeStruct((lay.nc, A_W), F32),
        grid=(lay.batch,),
        in_specs=[pl.BlockSpec((lay.seq, A_W), lambda b: (b, 0)),
                  pl.BlockSpec((lay.seq, A_W), lambda b: (b, 1)),
                  pl.BlockSpec((lay.seq, A_W), lambda b: (b, 2))],
        out_specs=pl.BlockSpec((lay.seq, A_W), lambda b: (b, 0)),
        compiler_params=_params(("arbitrary",)),
        name="ctx_attention",
    )(proj, proj, proj)

    rows = lay.dec_seq // GRID_W
    kh = min(NA_KH, rows)
    past = cache_k.shape[2]
    q0 = lay.nc // GRID_W
    b0 = lay.nc // lay.dec_seq
    lat = pl.pallas_call(
        functools.partial(_na_attn_kernel, rows=rows, kh=kh),
        out_shape=jax.ShapeDtypeStruct((lay.nl, A_W), F32),
        grid=(lay.dec_batch, rows),
        in_specs=[pl.BlockSpec((GRID_W, A_W), lambda b, r: (q0 + b * rows + r, 0)),
                  pl.BlockSpec((lay.dec_seq, A_W), lambda b, r: (b0 + b, 1)),
                  pl.BlockSpec((lay.dec_seq, A_W), lambda b, r: (b0 + b, 2)),
                  pl.BlockSpec((None, None, past, A_W), lambda b, r: (b, i_even, 0, 0)),
                  pl.BlockSpec((None, None, past, A_W), lambda b, r: (b, i_even, 0, 0)),
                  pl.BlockSpec((H_A, 1, GRID_W, kh * GRID_W), lambda b, r: (0, r, 0, 0))],
        out_specs=pl.BlockSpec((GRID_W, A_W), lambda b, r: (b * rows + r, 0)),
        compiler_params=_params(("arbitrary", "arbitrary")),
        name="na_attention",
    )(proj, proj, proj, cache_k, cache_v, bias)
    return ctx, lat


def _seq_position(first, nc, seq, dec_seq):
    is_lat = first >= nc
    return jnp.where(is_lat, (first - nc) % dec_seq, first % seq), jnp.where(is_lat, dec_seq, seq)


GATE_GC, GATE_BETA, GATE_GT = 0, 2 * H_B, 4 * H_B


def _gate_table(ab, alog, dtb):
    n = SEG
    lane = lax.broadcasted_iota(jnp.int32, (1, LANES), 1)
    ri = lax.broadcasted_iota(jnp.int32, (n, n), 0)
    ci = lax.broadcasted_iota(jnp.int32, (n, n), 1)
    shift = int(math.log2(CHUNK))
    same = jnp.right_shift(ri, shift) == jnp.right_shift(ci, shift)
    g_all = -jnp.exp(alog) * jax.nn.softplus(ab + dtb)
    prefix = _dot_f32x3(_mask01(same & (ci <= ri)), g_all)
    total = _dot_f32x3(_mask01(same), g_all)
    gc = jnp.where(lane < H_B, prefix, total - prefix + g_all)
    return jnp.where(lane < GATE_BETA, gc,
                     jnp.where(lane < GATE_GT, jax.nn.sigmoid(ab), pltpu.roll(total, GATE_GT, 1)))


def _delta_prep_kernel(xp_ref, xc_ref, xn_ref, w_ref, ab_ref, alog_ref, dtb_ref, q_ref, k_ref, v_ref, g_ref, gt_ref,
                       *, nc, seq, dec_seq):
    table = _gate_table(ab_ref[...], alog_ref[...], dtb_ref[...])
    g_ref[...] = table
    gt_ref[...] = table.T
    off, t_len = _seq_position(pl.program_id(0) * SEG, nc, seq, dec_seq)
    x = xc_ref[...]
    row = lax.broadcasted_iota(jnp.int32, (SEG, 1), 0)
    before = jnp.where(off > 0, xp_ref[7:8, :], 0.0)
    after = jnp.where(off + SEG < t_len, xn_ref[0:1, :], 0.0)
    prev = jnp.where(row == 0, before, pltpu.roll(x, 1, 0))
    nxt = jnp.where(row == SEG - 1, after, pltpu.roll(x, SEG - 1, 0))
    w = w_ref[...]
    y = _silu(prev * w[0:1] + x * w[1:2] + nxt * w[2:3])

    def l2norm(a):
        return a * lax.rsqrt(jnp.sum(a * a, axis=-1, keepdims=True) + EPS)

    heads = lambda part: [y[:, (part * H_B + h) * DK:(part * H_B + h + 1) * DK] for h in range(H_B)]
    q_ref[...] = jnp.concatenate([l2norm(a) * (DK ** -0.5) for a in heads(0)], axis=1)
    k_ref[...] = jnp.concatenate([l2norm(a) for a in heads(1)], axis=1)
    v_ref[...] = y[:, 2 * B_W:]


def _delta_prep(lay, proj, conv_w, ab, alog, dtb, i_even):
    width = 3 * B_W
    cblk = 3 * A_W // width
    assert cblk * width == 3 * A_W
    per = SEG // 8
    n8 = lay.n // 8
    ospec = pl.BlockSpec((SEG, B_W), lambda i: (i, 0))
    shape = jax.ShapeDtypeStruct((lay.n, B_W), F32)
    row = pl.BlockSpec((1, LANES), lambda i: (0, 0))
    return pl.pallas_call(
        functools.partial(_delta_prep_kernel, nc=lay.nc, seq=lay.seq, dec_seq=lay.dec_seq),
        out_shape=(shape, shape, shape,
                   jax.ShapeDtypeStruct((lay.n, LANES), F32), jax.ShapeDtypeStruct((LANES, lay.n), F32)),
        grid=(lay.n // SEG,),
        in_specs=[pl.BlockSpec((8, width), lambda i: (jnp.maximum(i * per - 1, 0), cblk)),
                  pl.BlockSpec((SEG, width), lambda i: (i, cblk)),
                  pl.BlockSpec((8, width), lambda i: (jnp.minimum((i + 1) * per, n8 - 1), cblk)),
                  pl.BlockSpec((None, CONV_K, width), lambda i: (i_even, 0, 0)),
                  pl.BlockSpec((SEG, LANES), lambda i: (i, 0)), row, row],
        out_specs=(ospec, ospec, ospec,
                   pl.BlockSpec((SEG, LANES), lambda i: (i, 0)), pl.BlockSpec((LANES, SEG), lambda i: (0, i))),
        compiler_params=_params(("arbitrary",)),
        name="delta_prep",
    )(proj, proj, proj, conv_w, ab, alog, dtb)


def _delta_masks(d):
    n = SEG
    ri = lax.broadcasted_iota(jnp.int32, (n, n), 0)
    ci = lax.broadcasted_iota(jnp.int32, (n, n), 1)
    shift = int(math.log2(CHUNK))
    same = jnp.right_shift(ri, shift) == jnp.right_shift(ci, shift)
    incl = same & ((ci <= ri) if d == 0 else (ci >= ri))
    strict = same & ((ci < ri) if d == 0 else (ci > ri))
    levels = []
    for level in range(shift):
        bi, bj = jnp.right_shift(ri, level), jnp.right_shift(ci, level)
        siblings = jnp.right_shift(bi, 1) == jnp.right_shift(bj, 1)
        levels.append(siblings & ((bi > bj) if d == 0 else (bi < bj)))
    return incl, strict, jnp.where(ri == ci, 1.0, 0.0), levels


def _delta_chains(chains):
    n = SEG
    n_chunks = n // CHUNK
    each = lambda fn, *lists: [fn(*args) for args in zip(*lists)]
    ds, masks, qs, ks, vs, betas, gcs, gts, gc_rows, gt_rows, states = (list(t) for t in zip(*chains))
    incls, stricts = [m[0] for m in masks], [m[1] for m in masks]
    n_levels = len(masks[0][3])

    decays = each(lambda m, gc, gr: jnp.where(m, jnp.exp(jnp.where(m, gc - gr, 0.0)), 0.0), incls, gcs, gc_rows)
    kbs = each(lambda k, b: k * b, ks, betas)
    k16s = [k.astype(BF16) for k in ks]
    grams = each(lambda kb, k16: _dot_nt(kb.astype(BF16), k16), kbs, k16s)
    lmats = each(lambda m, g, dec: jnp.where(m, g * dec, 0.0), stricts, grams, decays)

    xs = each(lambda m, lm: m[2] - jnp.where(m[3][0], lm, 0.0), masks, lmats)
    for level in range(1, n_levels):
        x16s = [x.astype(BF16) for x in xs]
        cs = each(lambda m, lm: jnp.where(m[3][level], lm, 0.0).astype(BF16), masks, lmats)
        xcs = each(lambda x16, c: _dot(x16, c).astype(BF16), x16s, cs)
        xs = each(lambda x, xc, x16: x - _dot(xc, x16), xs, xcs, x16s)

    egs = [jnp.exp(gc) for gc in gcs]
    rhss = each(lambda v, b, kb, eg: jnp.concatenate([v * b, kb * eg], axis=1).astype(BF16), vs, betas, kbs, egs)
    uws = each(lambda x, rhs: _dot(x.astype(BF16), rhs), xs, rhss)
    qks = each(lambda q, k16: _dot_nt(q.astype(BF16), k16), qs, k16s)
    attns = each(lambda m, qk, dec: jnp.where(m, qk * dec, 0.0).astype(BF16), incls, qks, decays)
    qgs = each(lambda q, eg: (q * eg).astype(BF16), qs, egs)
    kd_ts = each(lambda k, gt_r, gc_r: (k.T * jnp.exp(gt_r - gc_r)).astype(BF16), ks, gt_rows, gc_rows)
    gls = [jnp.exp(gt) for gt in gts]

    v_new = [[None] * n_chunks for _ in chains]
    o_state = [[None] * n_chunks for _ in chains]
    for step in range(n_chunks):
        for i, d in enumerate(ds):
            c = step if d == 0 else n_chunks - 1 - step
            rs = slice(c * CHUNK, (c + 1) * CHUNK)
            s16 = states[i].astype(BF16)
            vn = uws[i][rs, :DV] - _dot(uws[i][rs, DV:].astype(BF16), s16)
            o_state[i][c] = _dot(qgs[i][rs], s16)
            v_new[i][c] = vn
            pieces = [jnp.zeros((c * CHUNK, DV), BF16), vn.astype(BF16),
                      jnp.zeros((n - (c + 1) * CHUNK, DV), BF16)]
            padded = jnp.concatenate([p for p in pieces if p.shape[0]], axis=0)
            states[i] = states[i] * gls[i][c * CHUNK:c * CHUNK + 1, :] + _dot(kd_ts[i], padded)
    outs = each(lambda os, a, vn: jnp.concatenate(os, axis=0) + _dot(a, jnp.concatenate(vn, axis=0).astype(BF16)),
                o_state, attns, v_new)
    return list(zip(outs, states))


def _delta_kernel(plan_ref, *refs):
    dir_refs = (refs[0:5], refs[5:10])
    s0_ref, of_ref, ob_ref, sfin_ref, s_scr = refs[10:]
    g = pl.program_id(0)
    first, last, has_s0 = plan_ref[2, g] == 1, plan_ref[3, g] == 1, plan_ref[4, g] == 1

    @pl.when(first)
    def _():
        s_scr[...] = jnp.where(has_s0, s0_ref[...], 0.0)

    chains = []
    for d, (q_ref, k_ref, v_ref, g_ref, gt_ref) in enumerate(dir_refs):
        masks = _delta_masks(d)
        q, k, v, gates, gates_t = q_ref[...], k_ref[...], v_ref[...], g_ref[...], gt_ref[...]
        for h in range(H_B):
            sl = slice(h * DK, (h + 1) * DK)
            col = d * H_B + h
            pick = lambda base: gates[:, base + col:base + col + 1]
            pick_t = lambda base: gates_t[base + col:base + col + 1, :]
            chains.append((d, masks, q[:, sl], k[:, sl], v[:, sl], pick(GATE_BETA), pick(GATE_GC),
                           pick(GATE_GT), pick_t(GATE_GC), pick_t(GATE_GT), s_scr[d, h]))
    results = _delta_chains(chains)
    for d, o_ref in enumerate((of_ref, ob_ref)):
        o_ref[...] = jnp.concatenate([results[d * H_B + h][0] for h in range(H_B)], axis=1)
        for h in range(H_B):
            s_scr[d, h] = results[d * H_B + h][1]

    @pl.when(last & jnp.logical_not(has_s0))
    def _():
        sfin_ref[...] = s_scr[...]


def _delta_plan(lay):
    rows = []
    for kind, n_seq, t in ((0, lay.batch, lay.seq), (1, lay.dec_batch, lay.dec_seq)):
        nseg = t // SEG
        base = 0 if kind == 0 else lay.nc // SEG
        for b in range(n_seq):
            for s in range(nseg):
                rows.append((base + b * nseg + s, base + b * nseg + nseg - 1 - s, int(s == 0), int(s == nseg - 1),
                             kind, b if kind else 0, b if kind == 0 else lay.batch - 1))
    return jnp.asarray(np.array(rows, np.int32).T)


def _delta_net(lay, q, k, v, gates, gates_t, s0):
    plan = _delta_plan(lay)
    in_specs, args = [], []
    for d in (0, 1):
        in_specs += [pl.BlockSpec((SEG, B_W), lambda g, plan, d=d: (plan[d, g], 0))] * 3
        in_specs += [pl.BlockSpec((SEG, LANES), lambda g, plan, d=d: (plan[d, g], 0)),
                     pl.BlockSpec((LANES, SEG), lambda g, plan, d=d: (0, plan[d, g]))]
        args += [q, k, v, gates, gates_t]
    state = lambda row: pl.BlockSpec((None, 2, H_B, DK, DV), lambda g, plan: (plan[row, g], 0, 0, 0, 0))
    oshape = jax.ShapeDtypeStruct((lay.n, B_W), F32)
    grid_spec = pltpu.PrefetchScalarGridSpec(
        num_scalar_prefetch=1,
        grid=(lay.n // SEG,),
        in_specs=in_specs + [state(5)],
        out_specs=(pl.BlockSpec((SEG, B_W), lambda g, plan: (plan[0, g], 0)),
                   pl.BlockSpec((SEG, B_W), lambda g, plan: (plan[1, g], 0)), state(6)),
        scratch_shapes=[pltpu.VMEM((2, H_B, DK, DV), F32)],
    )
    return pl.pallas_call(
        _delta_kernel,
        out_shape=(oshape, oshape, jax.ShapeDtypeStruct((lay.batch, 2, H_B, DK, DV), F32)),
        grid_spec=grid_spec,
        compiler_params=_params(("arbitrary",)),
        name="delta_net",
    )(plan, *args, s0)


def _even_out_kernel(oac_ref, oal_ref, of_ref, ob_ref, z_ref, gain_ref, w_ref, xa_ref, xb_ref, gt_ref,
                     g2_ref, sh2_ref, sc2_ref, xo_ref, h2_ref, *, ctx_tiles):
    ob = of_ref[...] + ob_ref[...]
    z = z_ref[...]
    oa = _stream_tile(oac_ref, oal_ref, ctx_tiles)
    parts = [oa.astype(BF16)]
    for h in range(H_B):
        sl = slice(h * DV, (h + 1) * DV)
        o_h = ob[:, sl]
        y = o_h * lax.rsqrt(jnp.mean(o_h * o_h, axis=-1, keepdims=True) + EPS) * gain_ref[...] * _silu(z[:, sl])
        parts.append(y.astype(BF16))
    mix = _dot(jnp.concatenate(parts, axis=1), w_ref[...])
    xn = _stream_tile(xa_ref, xb_ref, ctx_tiles) + gt_ref[0] * mix
    xo_ref[...] = xn
    h2_ref[...] = _modulate(xn, g2_ref[0], sh2_ref[0], sc2_ref[0]).astype(BF16)


def _even_output(lay, oa_ctx, oa_lat, o_f, o_b, proj, o_gain, w_out, x_parts, normg, modtab, l):
    i_even = l // 2
    tm = lay.tm
    ctx_tiles = lay.nc // tm
    x_specs, x_args = lay.stream_specs(x_parts, tm)
    zcol = (3 * A_W + 3 * B_W) // B_W
    half = lambda: pl.BlockSpec((tm, B_W), lambda i: (i, 0))
    full = lambda: pl.BlockSpec((tm, D_MODEL), lambda i: (i, 0))
    return pl.pallas_call(
        functools.partial(_even_out_kernel, ctx_tiles=ctx_tiles),
        out_shape=(jax.ShapeDtypeStruct((lay.n, D_MODEL), F32),
                   jax.ShapeDtypeStruct((lay.n, D_MODEL), BF16)),
        grid=(lay.n // tm,),
        in_specs=[pl.BlockSpec((tm, A_W), lambda i: (jnp.minimum(i, ctx_tiles - 1), 0)),
                  pl.BlockSpec((tm, A_W), lambda i: (jnp.maximum(i - ctx_tiles, 0), 0)),
                  half(), half(),
                  pl.BlockSpec((tm, B_W), lambda i: (i, zcol)),
                  pl.BlockSpec((None, 1, DV), lambda i: (i_even, 0, 0)),
                  pl.BlockSpec((None, D_MODEL, D_MODEL), lambda i: (i_even, 0, 0)),
                  *x_specs,
                  lay.mod_spec(l, 2, tm),
                  _row_spec(l, 1), lay.mod_spec(l, 3, tm), lay.mod_spec(l, 4, tm)],
        out_specs=(full(), full()),
        compiler_params=_params(("arbitrary",)),
        name="even_output",
    )(oa_ctx, oa_lat, o_f, o_b, proj, o_gain, w_out, *x_args, modtab, normg, modtab, modtab)


def _ffn_kernel(h_ref, wg_ref, wu_ref, wd_ref, x_ref, gt_ref, o_ref, acc_ref):
    j = pl.program_id(1)

    @pl.when(j == 0)
    def _():
        acc_ref[...] = jnp.zeros_like(acc_ref)

    h = h_ref[...]
    a = (_silu(_dot(h, wg_ref[...])) * _dot(h, wu_ref[...])).astype(BF16)
    acc_ref[...] += _dot(a, wd_ref[...])

    @pl.when(j == pl.num_programs(1) - 1)
    def _():
        o_ref[...] = x_ref[...] + gt_ref[0] * acc_ref[...]


def _dense_ffn(lay, h2, x, ffn_gate, ffn_up, ffn_down, modtab, l):
    i_even = l // 2
    tm, tf = lay.tm, FF_TILE
    return pl.pallas_call(
        _ffn_kernel,
        out_shape=jax.ShapeDtypeStruct((lay.n, D_MODEL), F32),
        grid=(lay.n // tm, D_FF // tf),
        in_specs=[pl.BlockSpec((tm, D_MODEL), lambda i, j: (i, 0)),
                  pl.BlockSpec((None, D_MODEL, tf), lambda i, j: (i_even, 0, j)),
                  pl.BlockSpec((None, D_MODEL, tf), lambda i, j: (i_even, 0, j)),
                  pl.BlockSpec((None, tf, D_MODEL), lambda i, j: (i_even, j, 0)),
                  pl.BlockSpec((tm, D_MODEL), lambda i, j: (i, 0)),
                  lay.mod_spec(l, 5, tm)],
        out_specs=pl.BlockSpec((tm, D_MODEL), lambda i, j: (i, 0)),
        scratch_shapes=[pltpu.VMEM((tm, D_MODEL), F32)],
        compiler_params=_params(("arbitrary", "arbitrary")),
        name="dense_ffn",
    )(h2, ffn_gate, ffn_up, ffn_down, x, modtab)


def _pool_kernel(xp_ref, xc_ref, xn_ref, g1_ref, sh1_ref, sc1_ref, gt1_ref, pw_ref, ps_ref,
                 g2_ref, sh2_ref, sc2_ref, wr_ref, xo_ref, h2_ref, ridx_ref, rw_ref, cnt_ref, count_scr,
                 *, nc, seq, dec_seq):
    i = pl.program_id(0)
    off, t_len = _seq_position(i * SEG, nc, seq, dec_seq)

    g1, sh1, sc1 = g1_ref[0], sh1_ref[0], sc1_ref[0]
    x = xc_ref[...]
    h_cur = _modulate(x, g1, sh1, sc1)
    h_cat = jnp.concatenate([_modulate(xp_ref[...], g1, sh1, sc1), h_cur,
                             _modulate(xn_ref[...], g1, sh1, sc1)], axis=0)

    t = off + lax.broadcasted_iota(jnp.int32, (SEG, 1), 0)
    p = off - POOL_HALO + lax.broadcasted_iota(jnp.int32, (1, SEG + 2 * POOL_HALO), 1)
    mixes = []
    for gi, w in enumerate(POOL_WINDOWS):
        sl = slice(gi * POOL_G, (gi + 1) * POOL_G)
        lo = jnp.maximum(t - w // 2, 0)
        hi = jnp.minimum(t + (w - w // 2), t_len)
        band = _mask01((p >= lo) & (p < hi))
        hg = h_cat[:, sl]
        hg_hi = hg.astype(BF16)
        hg_lo = (hg - hg_hi.astype(F32)).astype(BF16)
        window_sum = _dot(band, hg_hi) + _dot(band, hg_lo)
        y = (window_sum / (hi - lo).astype(F32) - h_cur[:, sl]).astype(BF16)
        mixes.append(_dot(y, pw_ref[gi].astype(BF16)))
    mix = jnp.concatenate(mixes, axis=1) * ps_ref[...]
    xn = x + gt1_ref[0] * mix
    xo_ref[...] = xn
    h2 = _modulate(xn, g2_ref[0], sh2_ref[0], sc2_ref[0])
    h2_ref[...] = h2

    logits = _dot_f32ish(h2, wr_ref[...])
    lane = lax.broadcasted_iota(jnp.int32, logits.shape, 1)
    lane_f = lane.astype(F32)
    lg = jnp.where(lane < N_EXP, logits, -jnp.inf)
    m1 = jnp.max(lg, axis=-1, keepdims=True)
    i1 = jnp.min(jnp.where(lg == m1, lane_f, float(LANES)), axis=-1, keepdims=True)
    lg2 = jnp.where(lane_f == i1, -jnp.inf, lg)
    m2 = jnp.max(lg2, axis=-1, keepdims=True)
    i2 = jnp.min(jnp.where(lg2 == m2, lane_f, float(LANES)), axis=-1, keepdims=True)
    e = jnp.exp(m2 - m1)
    w1 = 1.0 / (1.0 + e)
    w2 = e / (1.0 + e)
    rw_ref[...] = jnp.where(lane == 0, w1, jnp.where(lane == 1, w2, 0.0))

    @pl.when(i == 0)
    def _():
        count_scr[...] = jnp.zeros_like(count_scr)

    hit1, hit2 = lane_f == i1, lane_f == i2
    picks = jnp.where(hit1 | hit2, 1.0, 0.0)
    ri = lax.broadcasted_iota(jnp.int32, (SEG, SEG), 0)
    ci = lax.broadcasted_iota(jnp.int32, (SEG, SEG), 1)
    before = _dot(_mask01(ci < ri), picks.astype(BF16)) + count_scr[...]
    r1 = jnp.sum(jnp.where(hit1, before, 0.0), axis=-1, keepdims=True)
    r2 = jnp.sum(jnp.where(hit2, before, 0.0), axis=-1, keepdims=True)
    count_scr[...] += jnp.sum(picks, axis=0, keepdims=True)
    route = jnp.where(lane == 0, i1, jnp.where(lane == 1, i2, jnp.where(lane == 2, r1, jnp.where(lane == 3, r2, 0.0))))
    ridx_ref[...] = route.astype(jnp.int32)
    cnt_ref[...] = jnp.broadcast_to(count_scr[...], cnt_ref.shape)


def _pool_and_route(lay, x, normg, modtab, pool_w, pool_scale, w_router, l):
    i_odd = l // 2
    nblk = lay.n // SEG
    blk = lambda imap: pl.BlockSpec((SEG, D_MODEL), imap)
    halo = lambda imap: pl.BlockSpec((POOL_HALO, D_MODEL), imap)
    per = SEG // POOL_HALO
    lane_blk = pl.BlockSpec((SEG, LANES), lambda i: (i, 0))
    return pl.pallas_call(
        functools.partial(_pool_kernel, nc=lay.nc, seq=lay.seq, dec_seq=lay.dec_seq),
        out_shape=(jax.ShapeDtypeStruct((lay.n, D_MODEL), F32),
                   jax.ShapeDtypeStruct((lay.n, D_MODEL), F32),
                   jax.ShapeDtypeStruct((lay.n, LANES), jnp.int32),
                   jax.ShapeDtypeStruct((lay.n, LANES), F32),
                   jax.ShapeDtypeStruct((8, LANES), F32)),
        grid=(nblk,),
        in_specs=[halo(lambda i: (jnp.maximum(i * per - 1, 0), 0)),
                  blk(lambda i: (i, 0)),
                  halo(lambda i: (jnp.minimum((i + 1) * per, nblk * per - 1), 0)),
                  _row_spec(l, 0), lay.mod_spec(l, 0, SEG), lay.mod_spec(l, 1, SEG), lay.mod_spec(l, 2, SEG),
                  pl.BlockSpec((None, len(POOL_WINDOWS), POOL_G, POOL_G), lambda i: (i_odd, 0, 0, 0)),
                  pl.BlockSpec((None, 1, D_MODEL), lambda i: (i_odd, 0, 0)),
                  _row_spec(l, 1), lay.mod_spec(l, 3, SEG), lay.mod_spec(l, 4, SEG),
                  pl.BlockSpec((None, D_MODEL, LANES), lambda i: (i_odd, 0, 0))],
        out_specs=(blk(lambda i: (i, 0)), blk(lambda i: (i, 0)), lane_blk, lane_blk,
                   pl.BlockSpec((8, LANES), lambda i: (0, 0))),
        scratch_shapes=[pltpu.VMEM((1, LANES), F32)],
        compiler_params=_params(("arbitrary",)),
        name="pool_route",
    )(x, x, x, normg, modtab, modtab, modtab, pool_w, pool_scale, normg, modtab, modtab, w_router)


def _dispatch_kernel(pad_ref, dest_ref, h_ref, xs_hbm, zero_row, sem, *, tokens):
    def row_copy(t, k):
        return pltpu.make_async_copy(h_ref.at[pl.ds(t, 1)], xs_hbm.at[pl.ds(dest_ref[0, 2 * t + k], 1)], sem)

    def wait_rows(count):
        one_row = pltpu.make_async_copy(h_ref.at[pl.ds(0, 1)], xs_hbm.at[pl.ds(0, 1)], sem)
        for _ in range(count):
            one_row.wait()

    def start(t, carry):
        row_copy(t, 0).start()
        row_copy(t, 1).start()
        return carry

    def wait(t, carry):
        wait_rows(2)
        return carry

    @pl.when(pl.program_id(0) == 0)
    def _():
        zero_row[...] = jnp.zeros_like(zero_row)
        for e in range(N_EXP):
            def zero_start(r, carry, e=e):
                pltpu.make_async_copy(zero_row.at[pl.ds(0, 1)], xs_hbm.at[pl.ds(pad_ref[0, e] + r, 1)], sem).start()
                return carry

            def zero_wait(r, carry):
                wait_rows(1)
                return carry

            lax.fori_loop(0, pad_ref[1, e], zero_start, 0)
            lax.fori_loop(0, pad_ref[1, e], zero_wait, 0)

    lax.fori_loop(0, tokens, start, 0, unroll=16)
    lax.fori_loop(0, tokens, wait, 0, unroll=16)


def _moe_dispatch(h2, dest, pad, p_max):
    n = h2.shape[0]
    tokens = math.gcd(n, 1024)
    grid_spec = pltpu.PrefetchScalarGridSpec(
        num_scalar_prefetch=1,
        grid=(n // tokens,),
        in_specs=[pl.BlockSpec((None, 1, 2 * tokens), lambda i, pad: (i, 0, 0), memory_space=pltpu.SMEM),
                  pl.BlockSpec((tokens, D_MODEL), lambda i, pad: (i, 0))],
        out_specs=pl.BlockSpec(memory_space=pl.ANY),
        scratch_shapes=[pltpu.VMEM((8, D_MODEL), F32), pltpu.SemaphoreType.DMA],
    )
    return pl.pallas_call(
        functools.partial(_dispatch_kernel, tokens=tokens),
        out_shape=jax.ShapeDtypeStruct((p_max, D_MODEL), F32),
        grid_spec=grid_spec,
        compiler_params=_params(("arbitrary",)),
        name="moe_dispatch",
    )(pad, dest.reshape(n // tokens, 1, 2 * tokens), h2)


def _moe_kernel(ge_ref, gs_ref, gn_ref, xs_hbm, wg_ref, wu_ref, wd_ref, ys_hbm, xbuf, x16, acc, sem_in, sem_out):
    del ge_ref
    g = pl.program_id(0)
    j = pl.program_id(1)
    last_j = pl.num_programs(1) - 1
    nsub = gn_ref[g]
    row0 = gs_ref[g]

    def in_copy(first_row, s):
        rows = pl.ds(pl.multiple_of(first_row + s * MOE_SUB, MOE_SUB), MOE_SUB)
        return pltpu.make_async_copy(xs_hbm.at[rows], xbuf.at[pl.ds(pl.multiple_of(s * MOE_SUB, MOE_SUB), MOE_SUB)],
                                     sem_in.at[s])

    def out_copy(first, n_rows):
        rows = pl.ds(pl.multiple_of(row0 + first, MOE_SUB), n_rows)
        return pltpu.make_async_copy(acc.at[pl.ds(pl.multiple_of(first, MOE_SUB), n_rows)], ys_hbm.at[rows], sem_out)

    def loop(count, fn):
        def body(s, carry):
            fn(s)
            return carry
        lax.fori_loop(0, count, body, 0)

    @pl.when((g == 0) & (j == 0))
    def _():
        loop(nsub, lambda s: in_copy(row0, s).start())

    nxt = jnp.minimum(g + 1, pl.num_programs(0) - 1)

    @pl.when((j == 1) & (g + 1 < pl.num_programs(0)))
    def _():
        loop(gn_ref[nxt], lambda s: in_copy(gs_ref[nxt], s).start())

    wg = wg_ref[...].astype(BF16)
    wu = wu_ref[...].astype(BF16)
    wd = wd_ref[...].astype(BF16)

    def tile(first_sub, n_sub):
        n_rows = n_sub * MOE_SUB
        first = pl.multiple_of(first_sub * MOE_SUB, MOE_SUB)
        rows = pl.ds(first, n_rows)

        @pl.when(j == 0)
        def _():
            for k in range(n_sub):
                in_copy(row0, first_sub + k).wait()
            x16[rows, :] = xbuf[rows, :].astype(BF16)

        xs = x16[rows, :]
        a = (_silu(_dot(xs, wg)) * _dot(xs, wu)).astype(BF16)
        contrib = _dot(a, wd)

        @pl.when(j == 0)
        def _():
            acc[rows, :] = contrib

        @pl.when(j > 0)
        def _():
            acc[rows, :] += contrib

        @pl.when(j == last_j)
        def _():
            out_copy(first, n_rows).start()

    n_quads = lax.shift_right_logical(nsub, 2)
    has_two = (nsub & 2) == 2
    has_one = (nsub & 1) == 1
    loop(n_quads, lambda p: tile(4 * p, 4))

    @pl.when(has_two)
    def _():
        tile(4 * n_quads, 2)

    @pl.when(has_one)
    def _():
        tile(nsub - 1, 1)

    @pl.when(j == last_j)
    def _():
        loop(n_quads, lambda p: out_copy(4 * p * MOE_SUB, 4 * MOE_SUB).wait())

        @pl.when(has_two)
        def _():
            out_copy(4 * n_quads * MOE_SUB, 2 * MOE_SUB).wait()

        @pl.when(has_one)
        def _():
            out_copy((nsub - 1) * MOE_SUB, MOE_SUB).wait()


def _moe_plan(ridx, counts, n_tok):
    n_pairs = 2 * n_tok
    group_rows = MOE_SUB * MOE_GROUP
    g_max = -(-n_pairs // group_rows) + N_EXP
    counts = counts[0, :N_EXP].astype(jnp.int32)
    nsub_e = (counts + MOE_SUB - 1) // MOE_SUB
    region = nsub_e * MOE_SUB
    start_e = jnp.cumsum(region) - region
    expert = ridx[:, 0:2]
    is_e = expert[:, :, None] == jnp.arange(N_EXP, dtype=jnp.int32)[None, None, :]
    dest = jnp.sum(jnp.where(is_e, start_e[None, None, :], 0), axis=-1) + ridx[:, 2:4]

    ngrp_e = (nsub_e + MOE_GROUP - 1) // MOE_GROUP
    gend = jnp.cumsum(ngrp_e)
    gstart = gend - ngrp_e
    total = gend[-1]
    gid = jnp.arange(g_max, dtype=jnp.int32)
    ge = jnp.minimum(jnp.searchsorted(gend, gid, side="right").astype(jnp.int32), N_EXP - 1)
    kk = gid - gstart[ge]
    live = gid < total
    last_e = jnp.minimum(jnp.searchsorted(gend, total - 1, side="right").astype(jnp.int32), N_EXP - 1)
    g_expert = jnp.where(live, ge, last_e)
    g_row = jnp.where(live, start_e[ge] + kk * group_rows, 0)
    g_nsub = jnp.where(live, jnp.clip(nsub_e[ge] - kk * MOE_GROUP, 0, MOE_GROUP), 0)
    pad = jnp.stack([start_e + counts, region - counts]).astype(jnp.int32)
    return dest, pad, g_expert.astype(jnp.int32), g_row.astype(jnp.int32), g_nsub.astype(jnp.int32)


def _moe_experts(xs, g_expert, g_row, g_nsub, moe_gate, moe_up, moe_down, i_odd):
    p_max = xs.shape[0]
    g_max = g_expert.shape[0]
    tf = FF_TILE
    n_j = D_FF // tf
    group_rows = MOE_SUB * MOE_GROUP
    tile_of = lambda g, j, gn: jnp.where(gn[g] > 0, j, n_j - 1)
    grid_spec = pltpu.PrefetchScalarGridSpec(
        num_scalar_prefetch=3,
        grid=(g_max, n_j),
        in_specs=[pl.BlockSpec(memory_space=pl.ANY),
                  pl.BlockSpec((None, None, D_MODEL, tf),
                               lambda g, j, ge, gs, gn: (i_odd, ge[g], 0, tile_of(g, j, gn))),
                  pl.BlockSpec((None, None, D_MODEL, tf),
                               lambda g, j, ge, gs, gn: (i_odd, ge[g], 0, tile_of(g, j, gn))),
                  pl.BlockSpec((None, None, tf, D_MODEL),
                               lambda g, j, ge, gs, gn: (i_odd, ge[g], tile_of(g, j, gn), 0))],
        out_specs=pl.BlockSpec(memory_space=pl.ANY),
        scratch_shapes=[pltpu.VMEM((group_rows, D_MODEL), F32),
                        pltpu.VMEM((group_rows, D_MODEL), BF16),
                        pltpu.VMEM((group_rows, D_MODEL), F32),
                        pltpu.SemaphoreType.DMA((MOE_GROUP,)), pltpu.SemaphoreType.DMA],
    )
    return pl.pallas_call(
        _moe_kernel,
        out_shape=jax.ShapeDtypeStruct((p_max, D_MODEL), F32),
        grid_spec=grid_spec,
        input_output_aliases={3: 0},
        compiler_params=_params(("arbitrary", "arbitrary")),
        name="moe_experts",
    )(g_expert, g_row, g_nsub, xs, moe_gate, moe_up, moe_down)


def _combine_kernel(dest_ref, ys_hbm, x_ref, rw_ref, gt_ref, fg_ref, *rest, tokens, ctx_tiles):
    out_refs, (ybuf, sem) = rest[:-2], rest[-2:]
    def row_copy(t, k):
        return pltpu.make_async_copy(ys_hbm.at[pl.ds(dest_ref[0, 2 * t + k], 1)], ybuf.at[k, pl.ds(t, 1)], sem)

    def start(t, carry):
        row_copy(t, 0).start()
        row_copy(t, 1).start()
        return carry

    def wait(t, carry):
        one_row = pltpu.make_async_copy(ys_hbm.at[pl.ds(0, 1)], ybuf.at[0, pl.ds(0, 1)], sem)
        one_row.wait()
        one_row.wait()
        return carry

    lax.fori_loop(0, tokens, start, 0, unroll=16)
    lax.fori_loop(0, tokens, wait, 0, unroll=16)
    rw = rw_ref[...]
    y = rw[:, 0:1] * ybuf[0] + rw[:, 1:2] * ybuf[1]
    xn = x_ref[...] + gt_ref[0] * y
    if ctx_tiles is None:
        out_refs[0][...] = xn
        return
    xn = xn * lax.rsqrt(jnp.mean(xn * xn, axis=-1, keepdims=True) + EPS) * fg_ref[...]
    is_ctx = pl.program_id(0) < ctx_tiles

    @pl.when(is_ctx)
    def _():
        out_refs[0][...] = xn

    @pl.when(jnp.logical_not(is_ctx))
    def _():
        out_refs[1][...] = xn


def _moe_combine(lay, x, ys, dest, rw, modtab, final_g, l, final):
    tm = lay.tm
    full = lambda: pl.BlockSpec((tm, D_MODEL), lambda i: (i, 0))
    ctx_tiles = lay.nc // tm if final else None
    if final:
        out_shape = (jax.ShapeDtypeStruct((lay.nc, D_MODEL), F32), jax.ShapeDtypeStruct((lay.nl, D_MODEL), F32))
        out_specs = (pl.BlockSpec((tm, D_MODEL), lambda i: (jnp.minimum(i, ctx_tiles - 1), 0)),
                     pl.BlockSpec((tm, D_MODEL), lambda i: (jnp.maximum(i - ctx_tiles, 0), 0)))
    else:
        out_shape, out_specs = jax.ShapeDtypeStruct((lay.n, D_MODEL), F32), full()
    return pl.pallas_call(
        functools.partial(_combine_kernel, tokens=tm, ctx_tiles=ctx_tiles),
        out_shape=out_shape,
        grid=(lay.n // tm,),
        in_specs=[pl.BlockSpec((None, 1, 2 * tm), lambda i: (i, 0, 0), memory_space=pltpu.SMEM),
                  pl.BlockSpec(memory_space=pl.ANY),
                  full(),
                  pl.BlockSpec((tm, LANES), lambda i: (i, 0)),
                  lay.mod_spec(l, 5, tm),
                  pl.BlockSpec((1, D_MODEL), lambda i: (0, 0))],
        out_specs=out_specs,
        scratch_shapes=[pltpu.VMEM((2, tm, D_MODEL), F32), pltpu.SemaphoreType.DMA],
        compiler_params=_params(("arbitrary",)),
        name="moe_combine",
    )(dest.reshape(lay.n // tm, 1, 2 * tm), ys, x, rw, modtab, final_g)


def kernel(x_prompt, x_sample, cache_k_ctx, cache_v_ctx, state_delta, c, c_ctx, w_mod, b_mod, norm_g, final_g,
           w_in, conv_w, a_log, dt_bias, rpb, o_gain, w_out, ffn_gate, ffn_up, ffn_down, pool_w, pool_scale,
           w_router, moe_gate, moe_up, moe_down):
    batch, seq, d = x_prompt.shape
    dec_batch, dec_seq, _ = x_sample.shape
    depth = w_mod.shape[0]
    n_even = w_in.shape[0]
    past = cache_k_ctx.shape[2]
    assert d == D_MODEL and depth % 2 == 0
    lay = _Layout(batch, seq, dec_batch, dec_seq)

    cond = jnp.zeros((lay.mod_rows, d), F32).at[0].set(c_ctx).at[1:1 + dec_batch].set(c)
    mod = _adaln(cond, w_mod, b_mod)
    modtab = mod.reshape(depth, lay.mod_rows, 6, d).transpose(0, 2, 1, 3).reshape(depth * 6 * lay.mod_rows, 1, d)
    normg = norm_g.reshape(depth * 2, 1, d)

    x_parts = (x_prompt.reshape(lay.nc, d), x_sample.reshape(lay.nl, d))
    cache_k = cache_k_ctx.reshape(dec_batch, n_even, past, A_W)
    cache_v = cache_v_ctx.reshape(dec_batch, n_even, past, A_W)
    rows = dec_seq // GRID_W
    kh = min(NA_KH, rows)
    n_main = 3 * A_W + 4 * B_W
    pad128 = lambda a: jnp.pad(a.reshape(1, -1), ((0, 0), (0, LANES - a.size)))
    w_router_p = jnp.pad(w_router, ((0, 0), (0, 0), (0, LANES - N_EXP)))
    w_in16, w_out16 = _to_bf16(w_in), _to_bf16(w_out)
    ffn_gate, ffn_up, ffn_down = _to_bf16(ffn_gate), _to_bf16(ffn_up), _to_bf16(ffn_down)

    k_list, v_list, s_list = [], [], []
    for l in range(depth):
        i = l // 2
        if l % 2 == 0:
            w_ab = jnp.pad(w_in[i][:, n_main:], ((0, 0), (0, LANES - 4 * H_B)))
            proj, ab = _even_project(lay, x_parts, normg, modtab, w_in16, w_ab, l)
            bias = _na_bias_table(rpb[i], rows, kh)
            oa_ctx, oa_lat = _attention(lay, proj, cache_k, cache_v, bias, i)
            qd, kd, vd, gates, gates_t = _delta_prep(lay, proj, conv_w, ab, pad128(a_log[i]), pad128(dt_bias[i]), i)
            o_f, o_b, s_fin = _delta_net(lay, qd, kd, vd, gates, gates_t, state_delta[:, i])
            x, h2 = _even_output(lay, oa_ctx, oa_lat, o_f, o_b, proj, o_gain.reshape(n_even, 1, DV), w_out16, x_parts,
                                 normg, modtab, l)
            x = _dense_ffn(lay, h2, x, ffn_gate, ffn_up, ffn_down, modtab, l)
            k_list.append(proj[:lay.nc, A_W:2 * A_W].reshape(batch, seq, H_A, DH_A))
            v_list.append(proj[:lay.nc, 2 * A_W:3 * A_W].reshape(batch, seq, H_A, DH_A))
            s_list.append(s_fin)
        else:
            x, h2, ridx, rw, counts = _pool_and_route(lay, x, normg, modtab, pool_w,
                                                      pool_scale.reshape(-1, 1, d), w_router_p, l)
            dest, pad, g_expert, g_row, g_nsub = _moe_plan(ridx, counts, lay.n)
            xs = _moe_dispatch(h2, dest, pad, 2 * lay.n + N_EXP * MOE_SUB)
            ys = _moe_experts(xs, g_expert, g_row, g_nsub, moe_gate, moe_up, moe_down, i)
            x = _moe_combine(lay, x, ys, dest, rw, modtab, final_g.reshape(1, d), l, final=(l == depth - 1))
            x_parts = (x,)
    y_ctx, y_lat = x
    return (y_ctx.reshape(batch, seq, d), y_lat.reshape(dec_batch, dec_seq, d), jnp.stack(k_list, axis=1), jnp.stack(v_list, axis=1), jnp.stack(s_list, axis=1))
```

```python
import functools
import math

import numpy as np
import jax
import jax.numpy as jnp
from jax import lax
from jax.experimental import pallas as pl
from jax.experimental.pallas import tpu as pltpu

F32 = jnp.float32
BF16 = jnp.bfloat16

D_MODEL = 1024
GRID_W = 64
DH_A = 64
H_A = 8
A_W = H_A * DH_A
NA_KH = 8
NA_KW = 16
DK = 128
DV = 128
H_B = 4
B_W = H_B * DK
CONV_K = 3
CHUNK = 64
POOL_WINDOWS = (2, 4, 8, 16)
POOL_G = D_MODEL // len(POOL_WINDOWS)
D_FF = 7 * D_MODEL // 2
N_EXP = 8
EPS = 1e-6
NEG_INF = -1e30

LANES = 128
SEG = 256
POOL_HALO = 64
FF_TILE = 512
PROJ_TILE = 1792
MOE_SUB = 256
MOE_GROUP = 8
VMEM_LIMIT = 56 * 2 ** 20


def _params(sem, vmem=VMEM_LIMIT):
    return pltpu.CompilerParams(dimension_semantics=sem, vmem_limit_bytes=vmem)


def _silu(x):
    return x * jax.nn.sigmoid(x)


def _dot(a, b):
    return jnp.dot(a, b, preferred_element_type=F32)


def _dot_nt(a, b):
    return lax.dot_general(a, b, (((1,), (1,)), ((), ())), preferred_element_type=F32)


def _dot_f32x3(a01, x):
    x1 = x.astype(BF16)
    r1 = x - x1.astype(F32)
    x2 = r1.astype(BF16)
    x3 = (r1 - x2.astype(F32)).astype(BF16)
    return _dot(a01, x1) + _dot(a01, x2) + _dot(a01, x3)


def _dot_f32ish(a, b):
    a_hi, b_hi = a.astype(BF16), b.astype(BF16)
    a_lo, b_lo = (a - a_hi.astype(F32)).astype(BF16), (b - b_hi.astype(F32)).astype(BF16)
    return _dot(a_hi, b_hi) + (_dot(a_hi, b_lo) + _dot(a_lo, b_hi))


def _mask01(mask):
    return jnp.where(mask, 1.0, 0.0).astype(BF16)


def _modulate(x, g, shift, scale):
    y = x * lax.rsqrt(jnp.mean(x * x, axis=-1, keepdims=True) + EPS)
    return (y * g) * (1.0 + scale) + shift


def _cast_kernel(w_ref, o_ref):
    o_ref[...] = w_ref[...].astype(BF16)


def _to_bf16(w):
    layers, rows, cols = w.shape
    tr = 256
    assert rows % tr == 0
    spec = pl.BlockSpec((None, tr, cols), lambda l, r: (l, r, 0))
    return pl.pallas_call(
        _cast_kernel,
        out_shape=jax.ShapeDtypeStruct(w.shape, BF16),
        grid=(layers, rows // tr),
        in_specs=[spec],
        out_specs=spec,
        compiler_params=_params(("arbitrary", "arbitrary")),
        name="to_bf16",
    )(w)


def _adaln_kernel(cond_ref, w_ref, b_ref, o_ref):
    s = _silu(cond_ref[...]).astype(BF16)
    o_ref[0] = _dot(s, w_ref[0].astype(BF16)) + b_ref[0]


def _adaln(cond, w_mod, b_mod):
    depth, d, six_d = w_mod.shape
    r = cond.shape[0]
    tn = six_d // 4
    return pl.pallas_call(
        _adaln_kernel,
        out_shape=jax.ShapeDtypeStruct((depth, r, six_d), F32),
        grid=(depth, six_d // tn),
        in_specs=[
            pl.BlockSpec((r, d), lambda l, j: (0, 0)),
            pl.BlockSpec((1, d, tn), lambda l, j: (l, 0, j)),
            pl.BlockSpec((1, 1, tn), lambda l, j: (l, 0, j)),
        ],
        out_specs=pl.BlockSpec((1, r, tn), lambda l, j: (l, 0, j)),
        compiler_params=_params(("arbitrary", "arbitrary")),
        name="adaln",
    )(cond, w_mod, b_mod.reshape(depth, 1, six_d))


class _Layout:
    def __init__(self, batch, seq, dec_batch, dec_seq):
        self.batch, self.seq, self.dec_batch, self.dec_seq = batch, seq, dec_batch, dec_seq
        self.nc = batch * seq
        self.nl = dec_batch * dec_seq
        self.n = self.nc + self.nl
        self.mod_rows = -(-(1 + dec_batch) // 8) * 8
        assert seq % SEG == 0 and dec_seq % SEG == 0 and dec_seq % GRID_W == 0
        self.tm = math.gcd(self.nc, 1024)
        assert self.tm % SEG == 0 and dec_seq % self.tm == 0 and self.nc % dec_seq == 0

    def mod_row(self, first_row):
        return jnp.maximum((first_row - self.nc) // self.dec_seq + 1, 0)

    def stream_specs(self, parts, tm):
        ct = self.nc // tm
        off = ct if len(parts) == 1 else 0
        specs = (pl.BlockSpec((tm, D_MODEL), lambda i, *_: (jnp.minimum(i, ct - 1), 0)),
                 pl.BlockSpec((tm, D_MODEL), lambda i, *_: (off + jnp.maximum(i - ct, 0), 0)))
        return specs, (parts[0], parts[-1])

    def mod_spec(self, l, k, rows_per_step):
        base = (l * 6 + k) * self.mod_rows
        return pl.BlockSpec((1, 1, D_MODEL),
                            lambda i, *_: (base + self.mod_row(i * rows_per_step), 0, 0))


def _stream_tile(xa_ref, xb_ref, ctx_tiles):
    return jnp.where(pl.program_id(0) < ctx_tiles, xa_ref[...], xb_ref[...])


def _row_spec(l, k):
    return pl.BlockSpec((1, 1, D_MODEL), lambda i, *_: (l * 2 + k, 0, 0))


def _proj_kernel(xa_ref, xb_ref, g_ref, sh_ref, sc_ref, w_ref, wab_ref, proj_ref, ab_ref, h_scr, *, ctx_tiles):
    @pl.when(pl.program_id(1) == 0)
    def _():
        h = _modulate(_stream_tile(xa_ref, xb_ref, ctx_tiles), g_ref[0], sh_ref[0], sc_ref[0]).astype(BF16)
        h_scr[...] = h
        ab_ref[...] = _dot(h, wab_ref[...].astype(BF16))

    proj_ref[...] = _dot(h_scr[...], w_ref[...])


def _even_project(lay, x_parts, normg, modtab, w_in, w_ab, l):
    i_even = l // 2
    tm, tn = lay.tm, PROJ_TILE
    ncols = 3 * A_W + 4 * B_W
    x_specs, x_args = lay.stream_specs(x_parts, tm)
    return pl.pallas_call(
        functools.partial(_proj_kernel, ctx_tiles=lay.nc // tm),
        out_shape=(jax.ShapeDtypeStruct((lay.n, ncols), F32),
                   jax.ShapeDtypeStruct((lay.n, LANES), F32)),
        grid=(lay.n // tm, ncols // tn),
        in_specs=[
            *x_specs,
            _row_spec(l, 0),
            lay.mod_spec(l, 0, tm),
            lay.mod_spec(l, 1, tm),
            pl.BlockSpec((None, D_MODEL, tn), lambda i, j: (i_even, 0, j)),
            pl.BlockSpec((D_MODEL, LANES), lambda i, j: (0, 0)),
        ],
        out_specs=(pl.BlockSpec((tm, tn), lambda i, j: (i, j)),
                   pl.BlockSpec((tm, LANES), lambda i, j: (i, 0))),
        scratch_shapes=[pltpu.VMEM((tm, D_MODEL), BF16)],
        compiler_params=_params(("arbitrary", "arbitrary")),
        name="even_project",
    )(*x_args, normg, modtab, modtab, w_in, w_ab)


def _attend_heads(q, keys, values, bias_of):
    scale = DH_A ** -0.5
    per = LANES // DH_A
    lane = lax.broadcasted_iota(jnp.int32, (1, LANES), 1)
    own = [(lane >= s * DH_A) & (lane < (s + 1) * DH_A) for s in range(per)]
    groups = range(A_W // LANES)
    blocks = range(len(keys))
    gsl = lambda g: slice(g * LANES, (g + 1) * LANES)
    k16 = [[k[:, gsl(g)].astype(BF16) for k in keys] for g in groups]
    v16 = [[v[:, gsl(g)].astype(BF16) for v in values] for g in groups]
    heads = [(g, s) for g in groups for s in range(per)]
    qh = [jnp.where(own[s], q[:, gsl(g)], 0.0).astype(BF16) for g, s in heads]
    scores = []
    for h, (g, s) in enumerate(heads):
        row = []
        for i in blocks:
            sc = _dot_nt(qh[h], k16[g][i]) * scale
            b = bias_of(h, i)
            row.append(sc if b is None else sc + b)
        scores.append(row)
    peak = [functools.reduce(jnp.maximum, [jnp.max(sc, axis=-1, keepdims=True) for sc in row]) for row in scores]
    probs = [[jnp.exp(sc - m) for sc in row] for row, m in zip(scores, peak)]
    denom = [functools.reduce(jnp.add, [jnp.sum(p, axis=-1, keepdims=True) for p in row]) for row in probs]
    acc = [functools.reduce(jnp.add, [_dot(p.astype(BF16), v16[g][i]) for i, p in enumerate(row)])
           for row, (g, s) in zip(probs, heads)]
    outs = []
    for g in groups:
        o = jnp.zeros_like(acc[0])
        for s in range(per):
            h = g * per + s
            o = jnp.where(own[s], acc[h] / denom[h], o)
        outs.append(o)
    return jnp.concatenate(outs, axis=1)


def _ctx_attn_kernel(q_ref, k_ref, v_ref, o_ref):
    o_ref[...] = _attend_heads(q_ref[...], [k_ref[...]], [v_ref[...]], lambda h, i: None)


def _na_attn_kernel(q_ref, k_ref, v_ref, kc_ref, vc_ref, bias_ref, o_ref, k16, v16, kc16, vc16, *, rows, kh):
    r = pl.program_id(1)

    @pl.when(r == 0)
    def _():
        k16[...] = k_ref[...].astype(BF16)
        v16[...] = v_ref[...].astype(BF16)
        kc16[...] = kc_ref[...].astype(BF16)
        vc16[...] = vc_ref[...].astype(BF16)

    row_lo = jnp.clip(r - kh // 2, 0, rows - kh)
    start = pl.multiple_of(row_lo * GRID_W, GRID_W)
    kl = k16[pl.ds(start, kh * GRID_W), :]
    vl = v16[pl.ds(start, kh * GRID_W), :]
    o_ref[...] = _attend_heads(q_ref[...], [kl, kc16[...]], [vl, vc16[...]],
                               lambda h, i: bias_ref[h, 0] if i == 0 else None)


def _na_bias_table(rpb, rows, kh):
    r = np.arange(rows)
    row_idx = np.clip(r - kh // 2, 0, rows - kh)[:, None] + np.arange(kh)[None, :]
    dr = row_idx - r[:, None] + (NA_KH - 1)
    qcol = np.arange(GRID_W)
    kcol = np.arange(GRID_W)
    col_lo = np.clip(qcol - NA_KW // 2, 0, GRID_W - NA_KW)
    valid = (kcol[None, :] >= col_lo[:, None]) & (kcol[None, :] < col_lo[:, None] + NA_KW)
    dc = np.clip(kcol[None, :] - qcol[:, None], 1 - NA_KW, NA_KW - 1) + (NA_KW - 1)
    onehot = (dc[None, :, :] == np.arange(2 * NA_KW - 1)[:, None, None]).astype(np.float32)
    picked = jnp.einsum("hrjc,cqk->hrqjk", rpb[:, dr].astype(F32), onehot, precision=lax.Precision.HIGHEST)
    bias = jnp.where(valid[None, None, :, None, :], picked, NEG_INF)
    return bias.reshape(H_A, rows, GRID_W, kh * GRID_W)


def _attention(lay, proj, cache_k, cache_v, bias, i_even):
    ctx = pl.pallas_call(
        _ctx_attn_kernel,
        out_shape=jax.ShapeDtypeStruct((lay.nc, A_W), F32),
        grid=(lay.batch,),
        in_specs=[pl.BlockSpec((lay.seq, A_W), lambda b: (b, 0)),
                  pl.BlockSpec((lay.seq, A_W), lambda b: (b, 1)),
                  pl.BlockSpec((lay.seq, A_W), lambda b: (b, 2))],
        out_specs=pl.BlockSpec((lay.seq, A_W), lambda b: (b, 0)),
        compiler_params=_params(("arbitrary",)),
        name="ctx_attention",
    )(proj, proj, proj)

    rows = lay.dec_seq // GRID_W
    kh = min(NA_KH, rows)
    past = cache_k.shape[2]
    q0 = lay.nc // GRID_W
    b0 = lay.nc // lay.dec_seq
    lat = pl.pallas_call(
        functools.partial(_na_attn_kernel, rows=rows, kh=kh),
        out_shape=jax.ShapeDtypeStruct((lay.nl, A_W), F32),
        grid=(lay.dec_batch, rows),
        in_specs=[pl.BlockSpec((GRID_W, A_W), lambda b, r: (q0 + b * rows + r, 0)),
                  pl.BlockSpec((lay.dec_seq, A_W), lambda b, r: (b0 + b, 1)),
                  pl.BlockSpec((lay.dec_seq, A_W), lambda b, r: (b0 + b, 2)),
                  pl.BlockSpec((None, None, past, A_W), lambda b, r: (b, i_even, 0, 0)),
                  pl.BlockSpec((None, None, past, A_W), lambda b, r: (b, i_even, 0, 0)),
                  pl.BlockSpec((H_A, 1, GRID_W, kh * GRID_W), lambda b, r: (0, r, 0, 0))],
        out_specs=pl.BlockSpec((GRID_W, A_W), lambda b, r: (b * rows + r, 0)),
        scratch_shapes=[pltpu.VMEM((lay.dec_seq, A_W), BF16), pltpu.VMEM((lay.dec_seq, A_W), BF16),
                        pltpu.VMEM((past, A_W), BF16), pltpu.VMEM((past, A_W), BF16)],
        compiler_params=_params(("arbitrary", "arbitrary")),
        name="na_attention",
    )(proj, proj, proj, cache_k, cache_v, bias)
    return ctx, lat


def _seq_position(first, nc, seq, dec_seq):
    is_lat = first >= nc
    return jnp.where(is_lat, (first - nc) % dec_seq, first % seq), jnp.where(is_lat, dec_seq, seq)


GATE_GC, GATE_BETA, GATE_GT = 0, 2 * H_B, 4 * H_B


def _gate_table(ab, alog, dtb):
    n = SEG
    lane = lax.broadcasted_iota(jnp.int32, (1, LANES), 1)
    ri = lax.broadcasted_iota(jnp.int32, (n, n), 0)
    ci = lax.broadcasted_iota(jnp.int32, (n, n), 1)
    shift = int(math.log2(CHUNK))
    same = jnp.right_shift(ri, shift) == jnp.right_shift(ci, shift)
    g_all = -jnp.exp(alog) * jax.nn.softplus(ab + dtb)
    prefix = _dot_f32x3(_mask01(same & (ci <= ri)), g_all)
    total = _dot_f32x3(_mask01(same), g_all)
    gc = jnp.where(lane < H_B, prefix, total - prefix + g_all)
    return jnp.where(lane < GATE_BETA, gc,
                     jnp.where(lane < GATE_GT, jax.nn.sigmoid(ab), pltpu.roll(total, GATE_GT, 1)))


def _delta_prep_kernel(xp_ref, xc_ref, xn_ref, w_ref, ab_ref, alog_ref, dtb_ref, q_ref, k_ref, v_ref, g_ref, gt_ref,
                       *, nc, seq, dec_seq):
    table = _gate_table(ab_ref[...], alog_ref[...], dtb_ref[...])
    g_ref[...] = table
    gt_ref[...] = table.T
    off, t_len = _seq_position(pl.program_id(0) * SEG, nc, seq, dec_seq)
    x = xc_ref[...]
    row = lax.broadcasted_iota(jnp.int32, (SEG, 1), 0)
    before = jnp.where(off > 0, xp_ref[7:8, :], 0.0)
    after = jnp.where(off + SEG < t_len, xn_ref[0:1, :], 0.0)
    prev = jnp.where(row == 0, before, pltpu.roll(x, 1, 0))
    nxt = jnp.where(row == SEG - 1, after, pltpu.roll(x, SEG - 1, 0))
    w = w_ref[...]
    y = _silu(prev * w[0:1] + x * w[1:2] + nxt * w[2:3])

    def l2norm(a):
        return a * lax.rsqrt(jnp.sum(a * a, axis=-1, keepdims=True) + EPS)

    heads = lambda part: [y[:, (part * H_B + h) * DK:(part * H_B + h + 1) * DK] for h in range(H_B)]
    q_ref[...] = jnp.concatenate([l2norm(a) * (DK ** -0.5) for a in heads(0)], axis=1)
    k_ref[...] = jnp.concatenate([l2norm(a) for a in heads(1)], axis=1)
    v_ref[...] = y[:, 2 * B_W:]


def _delta_prep(lay, proj, conv_w, ab, alog, dtb, i_even):
    width = 3 * B_W
    cblk = 3 * A_W // width
    assert cblk * width == 3 * A_W
    per = SEG // 8
    n8 = lay.n // 8
    ospec = pl.BlockSpec((SEG, B_W), lambda i: (i, 0))
    shape = jax.ShapeDtypeStruct((lay.n, B_W), F32)
    row = pl.BlockSpec((1, LANES), lambda i: (0, 0))
    return pl.pallas_call(
        functools.partial(_delta_prep_kernel, nc=lay.nc, seq=lay.seq, dec_seq=lay.dec_seq),
        out_shape=(shape, shape, shape,
                   jax.ShapeDtypeStruct((lay.n, LANES), F32), jax.ShapeDtypeStruct((LANES, lay.n), F32)),
        grid=(lay.n // SEG,),
        in_specs=[pl.BlockSpec((8, width), lambda i: (jnp.maximum(i * per - 1, 0), cblk)),
                  pl.BlockSpec((SEG, width), lambda i: (i, cblk)),
                  pl.BlockSpec((8, width), lambda i: (jnp.minimum((i + 1) * per, n8 - 1), cblk)),
                  pl.BlockSpec((None, CONV_K, width), lambda i: (i_even, 0, 0)),
                  pl.BlockSpec((SEG, LANES), lambda i: (i, 0)), row, row],
        out_specs=(ospec, ospec, ospec,
                   pl.BlockSpec((SEG, LANES), lambda i: (i, 0)), pl.BlockSpec((LANES, SEG), lambda i: (0, i))),
        compiler_params=_params(("arbitrary",)),
        name="delta_prep",
    )(proj, proj, proj, conv_w, ab, alog, dtb)


def _delta_masks(d):
    n = SEG
    ri = lax.broadcasted_iota(jnp.int32, (n, n), 0)
    ci = lax.broadcasted_iota(jnp.int32, (n, n), 1)
    shift = int(math.log2(CHUNK))
    same = jnp.right_shift(ri, shift) == jnp.right_shift(ci, shift)
    incl = same & ((ci <= ri) if d == 0 else (ci >= ri))
    strict = same & ((ci < ri) if d == 0 else (ci > ri))
    levels = []
    for level in range(shift):
        bi, bj = jnp.right_shift(ri, level), jnp.right_shift(ci, level)
        siblings = jnp.right_shift(bi, 1) == jnp.right_shift(bj, 1)
        levels.append(siblings & ((bi > bj) if d == 0 else (bi < bj)))
    return incl, strict, jnp.where(ri == ci, 1.0, 0.0), levels


def _delta_chains(chains):
    n = SEG
    n_chunks = n // CHUNK
    each = lambda fn, *lists: [fn(*args) for args in zip(*lists)]
    ds, masks, qs, ks, vs, betas, gcs, gts, gc_rows, gt_rows, states = (list(t) for t in zip(*chains))
    incls, stricts = [m[0] for m in masks], [m[1] for m in masks]
    n_levels = len(masks[0][3])

    decays = each(lambda m, gc, gr: jnp.where(m, jnp.exp(jnp.where(m, gc - gr, 0.0)), 0.0), incls, gcs, gc_rows)
    kbs = each(lambda k, b: k * b, ks, betas)
    k16s = [k.astype(BF16) for k in ks]
    grams = each(lambda kb, k16: _dot_nt(kb.astype(BF16), k16), kbs, k16s)
    lmats = each(lambda m, g, dec: jnp.where(m, g * dec, 0.0), stricts, grams, decays)

    xs = each(lambda m, lm: m[2] - jnp.where(m[3][0], lm, 0.0), masks, lmats)
    for level in range(1, n_levels):
        x16s = [x.astype(BF16) for x in xs]
        cs = each(lambda m, lm: jnp.where(m[3][level], lm, 0.0).astype(BF16), masks, lmats)
        xcs = each(lambda x16, c: _dot(x16, c).astype(BF16), x16s, cs)
        xs = each(lambda x, xc, x16: x - _dot(xc, x16), xs, xcs, x16s)

    egs = [jnp.exp(gc) for gc in gcs]
    rhss = each(lambda v, b, kb, eg: jnp.concatenate([v * b, kb * eg], axis=1).astype(BF16), vs, betas, kbs, egs)
    uws = each(lambda x, rhs: _dot(x.astype(BF16), rhs), xs, rhss)
    qks = each(lambda q, k16: _dot_nt(q.astype(BF16), k16), qs, k16s)
    attns = each(lambda m, qk, dec: jnp.where(m, qk * dec, 0.0).astype(BF16), incls, qks, decays)
    qgs = each(lambda q, eg: (q * eg).astype(BF16), qs, egs)
    kd_ts = each(lambda k, gt_r, gc_r: (k.T * jnp.exp(gt_r - gc_r)).astype(BF16), ks, gt_rows, gc_rows)
    gls = [jnp.exp(gt) for gt in gts]

    v_new = [[None] * n_chunks for _ in chains]
    o_state = [[None] * n_chunks for _ in chains]
    for step in range(n_chunks):
        for i, d in enumerate(ds):
            c = step if d == 0 else n_chunks - 1 - step
            rs = slice(c * CHUNK, (c + 1) * CHUNK)
            s16 = states[i].astype(BF16)
            vn = uws[i][rs, :DV] - _dot(uws[i][rs, DV:].astype(BF16), s16)
            o_state[i][c] = _dot(qgs[i][rs], s16)
            v_new[i][c] = vn
            pieces = [jnp.zeros((c * CHUNK, DV), BF16), vn.astype(BF16),
                      jnp.zeros((n - (c + 1) * CHUNK, DV), BF16)]
            padded = jnp.concatenate([p for p in pieces if p.shape[0]], axis=0)
            states[i] = states[i] * gls[i][c * CHUNK:c * CHUNK + 1, :] + _dot(kd_ts[i], padded)
    outs = each(lambda os, a, vn: jnp.concatenate(os, axis=0) + _dot(a, jnp.concatenate(vn, axis=0).astype(BF16)),
                o_state, attns, v_new)
    return list(zip(outs, states))


def _delta_kernel(plan_ref, *refs):
    dir_refs = (refs[0:5], refs[5:10])
    s0_ref, of_ref, ob_ref, sfin_ref, s_scr = refs[10:]
    g = pl.program_id(0)
    first, last, has_s0 = plan_ref[2, g] == 1, plan_ref[3, g] == 1, plan_ref[4, g] == 1

    @pl.when(first)
    def _():
        s_scr[...] = jnp.where(has_s0, s0_ref[...], 0.0)

    chains = []
    for d, (q_ref, k_ref, v_ref, g_ref, gt_ref) in enumerate(dir_refs):
        masks = _delta_masks(d)
        q, k, v, gates, gates_t = q_ref[...], k_ref[...], v_ref[...], g_ref[...], gt_ref[...]
        for h in range(H_B):
            sl = slice(h * DK, (h + 1) * DK)
            col = d * H_B + h
            pick = lambda base: gates[:, base + col:base + col + 1]
            pick_t = lambda base: gates_t[base + col:base + col + 1, :]
            chains.append((d, masks, q[:, sl], k[:, sl], v[:, sl], pick(GATE_BETA), pick(GATE_GC),
                           pick(GATE_GT), pick_t(GATE_GC), pick_t(GATE_GT), s_scr[d, h]))
    results = _delta_chains(chains)
    for d, o_ref in enumerate((of_ref, ob_ref)):
        o_ref[...] = jnp.concatenate([results[d * H_B + h][0] for h in range(H_B)], axis=1)
        for h in range(H_B):
            s_scr[d, h] = results[d * H_B + h][1]

    @pl.when(last & jnp.logical_not(has_s0))
    def _():
        sfin_ref[...] = s_scr[...]


def _delta_plan(lay):
    rows = []
    for kind, n_seq, t in ((0, lay.batch, lay.seq), (1, lay.dec_batch, lay.dec_seq)):
        nseg = t // SEG
        base = 0 if kind == 0 else lay.nc // SEG
        for b in range(n_seq):
            for s in range(nseg):
                rows.append((base + b * nseg + s, base + b * nseg + nseg - 1 - s, int(s == 0), int(s == nseg - 1),
                             kind, b if kind else 0, b if kind == 0 else lay.batch - 1))
    return jnp.asarray(np.array(rows, np.int32).T)


def _delta_net(lay, q, k, v, gates, gates_t, s0):
    plan = _delta_plan(lay)
    in_specs, args = [], []
    for d in (0, 1):
        in_specs += [pl.BlockSpec((SEG, B_W), lambda g, plan, d=d: (plan[d, g], 0))] * 3
        in_specs += [pl.BlockSpec((SEG, LANES), lambda g, plan, d=d: (plan[d, g], 0)),
                     pl.BlockSpec((LANES, SEG), lambda g, plan, d=d: (0, plan[d, g]))]
        args += [q, k, v, gates, gates_t]
    state = lambda row: pl.BlockSpec((None, 2, H_B, DK, DV), lambda g, plan: (plan[row, g], 0, 0, 0, 0))
    oshape = jax.ShapeDtypeStruct((lay.n, B_W), F32)
    grid_spec = pltpu.PrefetchScalarGridSpec(
        num_scalar_prefetch=1,
        grid=(lay.n // SEG,),
        in_specs=in_specs + [state(5)],
        out_specs=(pl.BlockSpec((SEG, B_W), lambda g, plan: (plan[0, g], 0)),
                   pl.BlockSpec((SEG, B_W), lambda g, plan: (plan[1, g], 0)), state(6)),
        scratch_shapes=[pltpu.VMEM((2, H_B, DK, DV), F32)],
    )
    return pl.pallas_call(
        _delta_kernel,
        out_shape=(oshape, oshape, jax.ShapeDtypeStruct((lay.batch, 2, H_B, DK, DV), F32)),
        grid_spec=grid_spec,
        compiler_params=_params(("arbitrary",)),
        name="delta_net",
    )(plan, *args, s0)


def _even_out_kernel(oac_ref, oal_ref, of_ref, ob_ref, z_ref, gain_ref, w_ref, xa_ref, xb_ref, gt_ref,
                     g2_ref, sh2_ref, sc2_ref, xo_ref, h2_ref, *, ctx_tiles):
    ob = of_ref[...] + ob_ref[...]
    z = z_ref[...]
    oa = _stream_tile(oac_ref, oal_ref, ctx_tiles)
    parts = [oa.astype(BF16)]
    for h in range(H_B):
        sl = slice(h * DV, (h + 1) * DV)
        o_h = ob[:, sl]
        y = o_h * lax.rsqrt(jnp.mean(o_h * o_h, axis=-1, keepdims=True) + EPS) * gain_ref[...] * _silu(z[:, sl])
        parts.append(y.astype(BF16))
    mix = _dot(jnp.concatenate(parts, axis=1), w_ref[...])
    xn = _stream_tile(xa_ref, xb_ref, ctx_tiles) + gt_ref[0] * mix
    xo_ref[...] = xn
    h2_ref[...] = _modulate(xn, g2_ref[0], sh2_ref[0], sc2_ref[0]).astype(BF16)


def _even_output(lay, oa_ctx, oa_lat, o_f, o_b, proj, o_gain, w_out, x_parts, normg, modtab, l):
    i_even = l // 2
    tm = lay.tm
    ctx_tiles = lay.nc // tm
    x_specs, x_args = lay.stream_specs(x_parts, tm)
    zcol = (3 * A_W + 3 * B_W) // B_W
    half = lambda: pl.BlockSpec((tm, B_W), lambda i: (i, 0))
    full = lambda: pl.BlockSpec((tm, D_MODEL), lambda i: (i, 0))
    return pl.pallas_call(
        functools.partial(_even_out_kernel, ctx_tiles=ctx_tiles),
        out_shape=(jax.ShapeDtypeStruct((lay.n, D_MODEL), F32),
                   jax.ShapeDtypeStruct((lay.n, D_MODEL), BF16)),
        grid=(lay.n // tm,),
        in_specs=[pl.BlockSpec((tm, A_W), lambda i: (jnp.minimum(i, ctx_tiles - 1), 0)),
                  pl.BlockSpec((tm, A_W), lambda i: (jnp.maximum(i - ctx_tiles, 0), 0)),
                  half(), half(),
                  pl.BlockSpec((tm, B_W), lambda i: (i, zcol)),
                  pl.BlockSpec((None, 1, DV), lambda i: (i_even, 0, 0)),
                  pl.BlockSpec((None, D_MODEL, D_MODEL), lambda i: (i_even, 0, 0)),
                  *x_specs,
                  lay.mod_spec(l, 2, tm),
                  _row_spec(l, 1), lay.mod_spec(l, 3, tm), lay.mod_spec(l, 4, tm)],
        out_specs=(full(), full()),
        compiler_params=_params(("arbitrary",)),
        name="even_output",
    )(oa_ctx, oa_lat, o_f, o_b, proj, o_gain, w_out, *x_args, modtab, normg, modtab, modtab)


def _ffn_kernel(h_ref, wg_ref, wu_ref, wd_ref, x_ref, gt_ref, o_ref, acc_ref):
    j = pl.program_id(1)

    @pl.when(j == 0)
    def _():
        acc_ref[...] = jnp.zeros_like(acc_ref)

    h = h_ref[...]
    a = (_silu(_dot(h, wg_ref[...])) * _dot(h, wu_ref[...])).astype(BF16)
    acc_ref[...] += _dot(a, wd_ref[...])

    @pl.when(j == pl.num_programs(1) - 1)
    def _():
        o_ref[...] = x_ref[...] + gt_ref[0] * acc_ref[...]


def _dense_ffn(lay, h2, x, ffn_gate, ffn_up, ffn_down, modtab, l):
    i_even = l // 2
    tm, tf = lay.tm, FF_TILE
    return pl.pallas_call(
        _ffn_kernel,
        out_shape=jax.ShapeDtypeStruct((lay.n, D_MODEL), F32),
        grid=(lay.n // tm, D_FF // tf),
        in_specs=[pl.BlockSpec((tm, D_MODEL), lambda i, j: (i, 0)),
                  pl.BlockSpec((None, D_MODEL, tf), lambda i, j: (i_even, 0, j)),
                  pl.BlockSpec((None, D_MODEL, tf), lambda i, j: (i_even, 0, j)),
                  pl.BlockSpec((None, tf, D_MODEL), lambda i, j: (i_even, j, 0)),
                  pl.BlockSpec((tm, D_MODEL), lambda i, j: (i, 0)),
                  lay.mod_spec(l, 5, tm)],
        out_specs=pl.BlockSpec((tm, D_MODEL), lambda i, j: (i, 0)),
        scratch_shapes=[pltpu.VMEM((tm, D_MODEL), F32)],
        compiler_params=_params(("arbitrary", "arbitrary")),
        name="dense_ffn",
    )(h2, ffn_gate, ffn_up, ffn_down, x, modtab)


def _pool_kernel(xp_ref, xc_ref, xn_ref, g1_ref, sh1_ref, sc1_ref, gt1_ref, pw_ref, ps_ref,
                 g2_ref, sh2_ref, sc2_ref, wr_ref, xo_ref, h2_ref, ridx_ref, rw_ref, cnt_ref, count_scr,
                 *, nc, seq, dec_seq):
    i = pl.program_id(0)
    off, t_len = _seq_position(i * SEG, nc, seq, dec_seq)

    g1, sh1, sc1 = g1_ref[0], sh1_ref[0], sc1_ref[0]
    x = xc_ref[...]
    h_cur = _modulate(x, g1, sh1, sc1)
    h_cat = jnp.concatenate([_modulate(xp_ref[...], g1, sh1, sc1), h_cur,
                             _modulate(xn_ref[...], g1, sh1, sc1)], axis=0)

    t = off + lax.broadcasted_iota(jnp.int32, (SEG, 1), 0)
    p = off - POOL_HALO + lax.broadcasted_iota(jnp.int32, (1, SEG + 2 * POOL_HALO), 1)
    mixes = []
    for gi, w in enumerate(POOL_WINDOWS):
        sl = slice(gi * POOL_G, (gi + 1) * POOL_G)
        lo = jnp.maximum(t - w // 2, 0)
        hi = jnp.minimum(t + (w - w // 2), t_len)
        band = _mask01((p >= lo) & (p < hi))
        hg = h_cat[:, sl]
        hg_hi = hg.astype(BF16)
        hg_lo = (hg - hg_hi.astype(F32)).astype(BF16)
        window_sum = _dot(band, hg_hi) + _dot(band, hg_lo)
        y = (window_sum / (hi - lo).astype(F32) - h_cur[:, sl]).astype(BF16)
        mixes.append(_dot(y, pw_ref[gi].astype(BF16)))
    mix = jnp.concatenate(mixes, axis=1) * ps_ref[...]
    xn = x + gt1_ref[0] * mix
    xo_ref[...] = xn
    h2 = _modulate(xn, g2_ref[0], sh2_ref[0], sc2_ref[0])
    h2_ref[...] = h2

    logits = _dot_f32ish(h2, wr_ref[...])
    lane = lax.broadcasted_iota(jnp.int32, logits.shape, 1)
    lane_f = lane.astype(F32)
    lg = jnp.where(lane < N_EXP, logits, -jnp.inf)
    m1 = jnp.max(lg, axis=-1, keepdims=True)
    i1 = jnp.min(jnp.where(lg == m1, lane_f, float(LANES)), axis=-1, keepdims=True)
    lg2 = jnp.where(lane_f == i1, -jnp.inf, lg)
    m2 = jnp.max(lg2, axis=-1, keepdims=True)
    i2 = jnp.min(jnp.where(lg2 == m2, lane_f, float(LANES)), axis=-1, keepdims=True)
    e = jnp.exp(m2 - m1)
    w1 = 1.0 / (1.0 + e)
    w2 = e / (1.0 + e)
    rw_ref[...] = jnp.where(lane == 0, w1, jnp.where(lane == 1, w2, 0.0))

    @pl.when(i == 0)
    def _():
        count_scr[...] = jnp.zeros_like(count_scr)

    hit1, hit2 = lane_f == i1, lane_f == i2
    picks = jnp.where(hit1 | hit2, 1.0, 0.0)
    ri = lax.broadcasted_iota(jnp.int32, (SEG, SEG), 0)
    ci = lax.broadcasted_iota(jnp.int32, (SEG, SEG), 1)
    before = _dot(_mask01(ci < ri), picks.astype(BF16)) + count_scr[...]
    r1 = jnp.sum(jnp.where(hit1, before, 0.0), axis=-1, keepdims=True)
    r2 = jnp.sum(jnp.where(hit2, before, 0.0), axis=-1, keepdims=True)
    count_scr[...] += jnp.sum(picks, axis=0, keepdims=True)
    route = jnp.where(lane == 0, i1, jnp.where(lane == 1, i2, jnp.where(lane == 2, r1, jnp.where(lane == 3, r2, 0.0))))
    ridx_ref[...] = route.astype(jnp.int32)
    cnt_ref[...] = jnp.broadcast_to(count_scr[...], cnt_ref.shape)


def _pool_and_route(lay, x, normg, modtab, pool_w, pool_scale, w_router, l):
    i_odd = l // 2
    nblk = lay.n // SEG
    blk = lambda imap: pl.BlockSpec((SEG, D_MODEL), imap)
    halo = lambda imap: pl.BlockSpec((POOL_HALO, D_MODEL), imap)
    per = SEG // POOL_HALO
    lane_blk = pl.BlockSpec((SEG, LANES), lambda i: (i, 0))
    return pl.pallas_call(
        functools.partial(_pool_kernel, nc=lay.nc, seq=lay.seq, dec_seq=lay.dec_seq),
        out_shape=(jax.ShapeDtypeStruct((lay.n, D_MODEL), F32),
                   jax.ShapeDtypeStruct((lay.n, D_MODEL), F32),
                   jax.ShapeDtypeStruct((lay.n, LANES), jnp.int32),
                   jax.ShapeDtypeStruct((lay.n, LANES), F32),
                   jax.ShapeDtypeStruct((8, LANES), F32)),
        grid=(nblk,),
        in_specs=[halo(lambda i: (jnp.maximum(i * per - 1, 0), 0)),
                  blk(lambda i: (i, 0)),
                  halo(lambda i: (jnp.minimum((i + 1) * per, nblk * per - 1), 0)),
                  _row_spec(l, 0), lay.mod_spec(l, 0, SEG), lay.mod_spec(l, 1, SEG), lay.mod_spec(l, 2, SEG),
                  pl.BlockSpec((None, len(POOL_WINDOWS), POOL_G, POOL_G), lambda i: (i_odd, 0, 0, 0)),
                  pl.BlockSpec((None, 1, D_MODEL), lambda i: (i_odd, 0, 0)),
                  _row_spec(l, 1), lay.mod_spec(l, 3, SEG), lay.mod_spec(l, 4, SEG),
                  pl.BlockSpec((None, D_MODEL, LANES), lambda i: (i_odd, 0, 0))],
        out_specs=(blk(lambda i: (i, 0)), blk(lambda i: (i, 0)), lane_blk, lane_blk,
                   pl.BlockSpec((8, LANES), lambda i: (0, 0))),
        scratch_shapes=[pltpu.VMEM((1, LANES), F32)],
        compiler_params=_params(("arbitrary",)),
        name="pool_route",
    )(x, x, x, normg, modtab, modtab, modtab, pool_w, pool_scale, normg, modtab, modtab, w_router)


def _dispatch_kernel(pad_ref, dest_ref, h_ref, xs_hbm, zero_row, sem, *, tokens):
    def row_copy(t, k):
        return pltpu.make_async_copy(h_ref.at[pl.ds(t, 1)], xs_hbm.at[pl.ds(dest_ref[0, 2 * t + k], 1)], sem)

    def wait_rows(count):
        one_row = pltpu.make_async_copy(h_ref.at[pl.ds(0, 1)], xs_hbm.at[pl.ds(0, 1)], sem)
        for _ in range(count):
            one_row.wait()

    def start(t, carry):
        row_copy(t, 0).start()
        row_copy(t, 1).start()
        return carry

    def wait(t, carry):
        wait_rows(2)
        return carry

    @pl.when(pl.program_id(0) == 0)
    def _():
        zero_row[...] = jnp.zeros_like(zero_row)
        for e in range(N_EXP):
            def zero_start(r, carry, e=e):
                pltpu.make_async_copy(zero_row.at[pl.ds(0, 1)], xs_hbm.at[pl.ds(pad_ref[0, e] + r, 1)], sem).start()
                return carry

            def zero_wait(r, carry):
                wait_rows(1)
                return carry

            lax.fori_loop(0, pad_ref[1, e], zero_start, 0)
            lax.fori_loop(0, pad_ref[1, e], zero_wait, 0)

    lax.fori_loop(0, tokens, start, 0, unroll=16)
    lax.fori_loop(0, tokens, wait, 0, unroll=16)


def _moe_dispatch(h2, dest, pad, p_max):
    n = h2.shape[0]
    tokens = math.gcd(n, 1024)
    grid_spec = pltpu.PrefetchScalarGridSpec(
        num_scalar_prefetch=1,
        grid=(n // tokens,),
        in_specs=[pl.BlockSpec((None, 1, 2 * tokens), lambda i, pad: (i, 0, 0), memory_space=pltpu.SMEM),
                  pl.BlockSpec((tokens, D_MODEL), lambda i, pad: (i, 0))],
        out_specs=pl.BlockSpec(memory_space=pl.ANY),
        scratch_shapes=[pltpu.VMEM((8, D_MODEL), F32), pltpu.SemaphoreType.DMA],
    )
    return pl.pallas_call(
        functools.partial(_dispatch_kernel, tokens=tokens),
        out_shape=jax.ShapeDtypeStruct((p_max, D_MODEL), F32),
        grid_spec=grid_spec,
        compiler_params=_params(("arbitrary",)),
        name="moe_dispatch",
    )(pad, dest.reshape(n // tokens, 1, 2 * tokens), h2)


def _moe_kernel(ge_ref, gs_ref, gn_ref, xs_hbm, wg_ref, wu_ref, wd_ref, ys_hbm, xbuf, x16, acc, sem_in, sem_out):
    del ge_ref
    g = pl.program_id(0)
    j = pl.program_id(1)
    last_j = pl.num_programs(1) - 1
    nsub = gn_ref[g]
    row0 = gs_ref[g]

    def in_copy(first_row, s):
        rows = pl.ds(pl.multiple_of(first_row + s * MOE_SUB, MOE_SUB), MOE_SUB)
        return pltpu.make_async_copy(xs_hbm.at[rows], xbuf.at[pl.ds(pl.multiple_of(s * MOE_SUB, MOE_SUB), MOE_SUB)],
                                     sem_in.at[s])

    def out_copy(first, n_rows):
        rows = pl.ds(pl.multiple_of(row0 + first, MOE_SUB), n_rows)
        return pltpu.make_async_copy(acc.at[pl.ds(pl.multiple_of(first, MOE_SUB), n_rows)], ys_hbm.at[rows], sem_out)

    def loop(count, fn):
        def body(s, carry):
            fn(s)
            return carry
        lax.fori_loop(0, count, body, 0)

    @pl.when((g == 0) & (j == 0))
    def _():
        loop(nsub, lambda s: in_copy(row0, s).start())

    nxt = jnp.minimum(g + 1, pl.num_programs(0) - 1)

    @pl.when((j == 1) & (g + 1 < pl.num_programs(0)))
    def _():
        loop(gn_ref[nxt], lambda s: in_copy(gs_ref[nxt], s).start())

    wg = wg_ref[...].astype(BF16)
    wu = wu_ref[...].astype(BF16)
    wd = wd_ref[...].astype(BF16)

    def tile(first_sub, n_sub):
        n_rows = n_sub * MOE_SUB
        first = pl.multiple_of(first_sub * MOE_SUB, MOE_SUB)
        rows = pl.ds(first, n_rows)

        @pl.when(j == 0)
        def _():
            for k in range(n_sub):
                in_copy(row0, first_sub + k).wait()
            x16[rows, :] = xbuf[rows, :].astype(BF16)

        xs = x16[rows, :]
        a = (_silu(_dot(xs, wg)) * _dot(xs, wu)).astype(BF16)
        contrib = _dot(a, wd)

        @pl.when(j == 0)
        def _():
            acc[rows, :] = contrib

        @pl.when(j > 0)
        def _():
            acc[rows, :] += contrib

        @pl.when(j == last_j)
        def _():
            out_copy(first, n_rows).start()

    n_quads = lax.shift_right_logical(nsub, 2)
    has_two = (nsub & 2) == 2
    has_one = (nsub & 1) == 1
    loop(n_quads, lambda p: tile(4 * p, 4))

    @pl.when(has_two)
    def _():
        tile(4 * n_quads, 2)

    @pl.when(has_one)
    def _():
        tile(nsub - 1, 1)

    @pl.when(j == last_j)
    def _():
        loop(n_quads, lambda p: out_copy(4 * p * MOE_SUB, 4 * MOE_SUB).wait())

        @pl.when(has_two)
        def _():
            out_copy(4 * n_quads * MOE_SUB, 2 * MOE_SUB).wait()

        @pl.when(has_one)
        def _():
            out_copy((nsub - 1) * MOE_SUB, MOE_SUB).wait()


def _moe_plan(ridx, counts, n_tok):
    n_pairs = 2 * n_tok
    group_rows = MOE_SUB * MOE_GROUP
    g_max = -(-n_pairs // group_rows) + N_EXP
    counts = counts[0, :N_EXP].astype(jnp.int32)
    nsub_e = (counts + MOE_SUB - 1) // MOE_SUB
    region = nsub_e * MOE_SUB
    start_e = jnp.cumsum(region) - region
    expert = ridx[:, 0:2]
    is_e = expert[:, :, None] == jnp.arange(N_EXP, dtype=jnp.int32)[None, None, :]
    dest = jnp.sum(jnp.where(is_e, start_e[None, None, :], 0), axis=-1) + ridx[:, 2:4]

    ngrp_e = (nsub_e + MOE_GROUP - 1) // MOE_GROUP
    gend = jnp.cumsum(ngrp_e)
    gstart = gend - ngrp_e
    total = gend[-1]
    gid = jnp.arange(g_max, dtype=jnp.int32)
    ge = jnp.minimum(jnp.searchsorted(gend, gid, side="right").astype(jnp.int32), N_EXP - 1)
    kk = gid - gstart[ge]
    live = gid < total
    last_e = jnp.minimum(jnp.searchsorted(gend, total - 1, side="right").astype(jnp.int32), N_EXP - 1)
    g_expert = jnp.where(live, ge, last_e)
    g_row = jnp.where(live, start_e[ge] + kk * group_rows, 0)
    g_nsub = jnp.where(live, jnp.clip(nsub_e[ge] - kk * MOE_GROUP, 0, MOE_GROUP), 0)
    pad = jnp.stack([start_e + counts, region - counts]).astype(jnp.int32)
    return dest, pad, g_expert.astype(jnp.int32), g_row.astype(jnp.int32), g_nsub.astype(jnp.int32)


def _moe_experts(xs, g_expert, g_row, g_nsub, moe_gate, moe_up, moe_down, i_odd):
    p_max = xs.shape[0]
    g_max = g_expert.shape[0]
    tf = FF_TILE
    n_j = D_FF // tf
    group_rows = MOE_SUB * MOE_GROUP
    tile_of = lambda g, j, gn: jnp.where(gn[g] > 0, j, n_j - 1)
    grid_spec = pltpu.PrefetchScalarGridSpec(
        num_scalar_prefetch=3,
        grid=(g_max, n_j),
        in_specs=[pl.BlockSpec(memory_space=pl.ANY),
                  pl.BlockSpec((None, None, D_MODEL, tf),
                               lambda g, j, ge, gs, gn: (i_odd, ge[g], 0, tile_of(g, j, gn))),
                  pl.BlockSpec((None, None, D_MODEL, tf),
                               lambda g, j, ge, gs, gn: (i_odd, ge[g], 0, tile_of(g, j, gn))),
                  pl.BlockSpec((None, None, tf, D_MODEL),
                               lambda g, j, ge, gs, gn: (i_odd, ge[g], tile_of(g, j, gn), 0))],
        out_specs=pl.BlockSpec(memory_space=pl.ANY),
        scratch_shapes=[pltpu.VMEM((group_rows, D_MODEL), F32),
                        pltpu.VMEM((group_rows, D_MODEL), BF16),
                        pltpu.VMEM((group_rows, D_MODEL), F32),
                        pltpu.SemaphoreType.DMA((MOE_GROUP,)), pltpu.SemaphoreType.DMA],
    )
    return pl.pallas_call(
        _moe_kernel,
        out_shape=jax.ShapeDtypeStruct((p_max, D_MODEL), F32),
        grid_spec=grid_spec,
        input_output_aliases={3: 0},
        compiler_params=_params(("arbitrary", "arbitrary")),
        name="moe_experts",
    )(g_expert, g_row, g_nsub, xs, moe_gate, moe_up, moe_down)


def _combine_kernel(dest_ref, ys_hbm, x_ref, rw_ref, gt_ref, fg_ref, *rest, tokens, ctx_tiles):
    out_refs, (ybuf, sem) = rest[:-2], rest[-2:]
    def row_copy(t, k):
        return pltpu.make_async_copy(ys_hbm.at[pl.ds(dest_ref[0, 2 * t + k], 1)], ybuf.at[k, pl.ds(t, 1)], sem)

    def start(t, carry):
        row_copy(t, 0).start()
        row_copy(t, 1).start()
        return carry

    def wait(t, carry):
        one_row = pltpu.make_async_copy(ys_hbm.at[pl.ds(0, 1)], ybuf.at[0, pl.ds(0, 1)], sem)
        one_row.wait()
        one_row.wait()
        return carry

    lax.fori_loop(0, tokens, start, 0, unroll=16)
    lax.fori_loop(0, tokens, wait, 0, unroll=16)
    rw = rw_ref[...]
    y = rw[:, 0:1] * ybuf[0] + rw[:, 1:2] * ybuf[1]
    xn = x_ref[...] + gt_ref[0] * y
    if ctx_tiles is None:
        out_refs[0][...] = xn
        return
    xn = xn * lax.rsqrt(jnp.mean(xn * xn, axis=-1, keepdims=True) + EPS) * fg_ref[...]
    is_ctx = pl.program_id(0) < ctx_tiles

    @pl.when(is_ctx)
    def _():
        out_refs[0][...] = xn

    @pl.when(jnp.logical_not(is_ctx))
    def _():
        out_refs[1][...] = xn


def _moe_combine(lay, x, ys, dest, rw, modtab, final_g, l, final):
    tm = lay.tm
    full = lambda: pl.BlockSpec((tm, D_MODEL), lambda i: (i, 0))
    ctx_tiles = lay.nc // tm if final else None
    if final:
        out_shape = (jax.ShapeDtypeStruct((lay.nc, D_MODEL), F32), jax.ShapeDtypeStruct((lay.nl, D_MODEL), F32))
        out_specs = (pl.BlockSpec((tm, D_MODEL), lambda i: (jnp.minimum(i, ctx_tiles - 1), 0)),
                     pl.BlockSpec((tm, D_MODEL), lambda i: (jnp.maximum(i - ctx_tiles, 0), 0)))
    else:
        out_shape, out_specs = jax.ShapeDtypeStruct((lay.n, D_MODEL), F32), full()
    return pl.pallas_call(
        functools.partial(_combine_kernel, tokens=tm, ctx_tiles=ctx_tiles),
        out_shape=out_shape,
        grid=(lay.n // tm,),
        in_specs=[pl.BlockSpec((None, 1, 2 * tm), lambda i: (i, 0, 0), memory_space=pltpu.SMEM),
                  pl.BlockSpec(memory_space=pl.ANY),
                  full(),
                  pl.BlockSpec((tm, LANES), lambda i: (i, 0)),
                  lay.mod_spec(l, 5, tm),
                  pl.BlockSpec((1, D_MODEL), lambda i: (0, 0))],
        out_specs=out_specs,
        scratch_shapes=[pltpu.VMEM((2, tm, D_MODEL), F32), pltpu.SemaphoreType.DMA],
        compiler_params=_params(("arbitrary",)),
        name="moe_combine",
    )(dest.reshape(lay.n // tm, 1, 2 * tm), ys, x, rw, modtab, final_g)


def kernel(x_prompt, x_sample, cache_k_ctx, cache_v_ctx, state_delta, c, c_ctx, w_mod, b_mod, norm_g, final_g,
           w_in, conv_w, a_log, dt_bias, rpb, o_gain, w_out, ffn_gate, ffn_up, ffn_down, pool_w, pool_scale,
           w_router, moe_gate, moe_up, moe_down):
    batch, seq, d = x_prompt.shape
    dec_batch, dec_seq, _ = x_sample.shape
    depth = w_mod.shape[0]
    n_even = w_in.shape[0]
    past = cache_k_ctx.shape[2]
    assert d == D_MODEL and depth % 2 == 0
    lay = _Layout(batch, seq, dec_batch, dec_seq)

    cond = jnp.zeros((lay.mod_rows, d), F32).at[0].set(c_ctx).at[1:1 + dec_batch].set(c)
    mod = _adaln(cond, w_mod, b_mod)
    modtab = mod.reshape(depth, lay.mod_rows, 6, d).transpose(0, 2, 1, 3).reshape(depth * 6 * lay.mod_rows, 1, d)
    normg = norm_g.reshape(depth * 2, 1, d)

    x_parts = (x_prompt.reshape(lay.nc, d), x_sample.reshape(lay.nl, d))
    cache_k = cache_k_ctx.reshape(dec_batch, n_even, past, A_W)
    cache_v = cache_v_ctx.reshape(dec_batch, n_even, past, A_W)
    rows = dec_seq // GRID_W
    kh = min(NA_KH, rows)
    n_main = 3 * A_W + 4 * B_W
    pad128 = lambda a: jnp.pad(a.reshape(1, -1), ((0, 0), (0, LANES - a.size)))
    w_router_p = jnp.pad(w_router, ((0, 0), (0, 0), (0, LANES - N_EXP)))
    w_in16, w_out16 = _to_bf16(w_in), _to_bf16(w_out)
    ffn_gate, ffn_up, ffn_down = _to_bf16(ffn_gate), _to_bf16(ffn_up), _to_bf16(ffn_down)

    k_list, v_list, s_list = [], [], []
    for l in range(depth):
        i = l // 2
        if l % 2 == 0:
            w_ab = jnp.pad(w_in[i][:, n_main:], ((0, 0), (0, LANES - 4 * H_B)))
            proj, ab = _even_project(lay, x_parts, normg, modtab, w_in16, w_ab, l)
            bias = _na_bias_table(rpb[i], rows, kh)
            oa_ctx, oa_lat = _attention(lay, proj, cache_k, cache_v, bias, i)
            qd, kd, vd, gates, gates_t = _delta_prep(lay, proj, conv_w, ab, pad128(a_log[i]), pad128(dt_bias[i]), i)
            o_f, o_b, s_fin = _delta_net(lay, qd, kd, vd, gates, gates_t, state_delta[:, i])
            x, h2 = _even_output(lay, oa_ctx, oa_lat, o_f, o_b, proj, o_gain.reshape(n_even, 1, DV), w_out16, x_parts,
                                 normg, modtab, l)
            x = _dense_ffn(lay, h2, x, ffn_gate, ffn_up, ffn_down, modtab, l)
            k_list.append(proj[:lay.nc, A_W:2 * A_W].reshape(batch, seq, A_W))
            v_list.append(proj[:lay.nc, 2 * A_W:3 * A_W].reshape(batch, seq, A_W))
            s_list.append(s_fin)
        else:
            x, h2, ridx, rw, counts = _pool_and_route(lay, x, normg, modtab, pool_w,
                                                      pool_scale.reshape(-1, 1, d), w_router_p, l)
            dest, pad, g_expert, g_row, g_nsub = _moe_plan(ridx, counts, lay.n)
            xs = _moe_dispatch(h2, dest, pad, 2 * lay.n + N_EXP * MOE_SUB)
            ys = _moe_experts(xs, g_expert, g_row, g_nsub, moe_gate, moe_up, moe_down, i)
            x = _moe_combine(lay, x, ys, dest, rw, modtab, final_g.reshape(1, d), l, final=(l == depth - 1))
            x_parts = (x,)
    y_ctx, y_lat = x
    heads = lambda parts: jnp.stack(parts, axis=1).reshape(batch, n_even, seq, H_A, DH_A)
    return (y_ctx.reshape(batch, seq, d), y_lat.reshape(dec_batch, dec_seq, d), heads(k_list), heads(v_list),
            jnp.stack(s_list, axis=1))
```

```python
import functools
import math

import numpy as np
import jax
import jax.numpy as jnp
from jax import lax
from jax.experimental import pallas as pl
from jax.experimental.pallas import tpu as pltpu

F32 = jnp.float32
BF16 = jnp.bfloat16

D_MODEL = 1024
GRID_W = 64
DH_A = 64
H_A = 8
A_W = H_A * DH_A
NA_KH = 8
NA_KW = 16
DK = 128
DV = 128
H_B = 4
B_W = H_B * DK
CONV_K = 3
CHUNK = 64
POOL_WINDOWS = (2, 4, 8, 16)
POOL_G = D_MODEL // len(POOL_WINDOWS)
D_FF = 7 * D_MODEL // 2
N_EXP = 8
EPS = 1e-6
NEG_INF = -1e30

LANES = 128
SEG = 256
POOL_HALO = 64
FF_TILE = 512
PROJ_TILE = 1792
MOE_SUB = 256
MOE_GROUP = 8
VMEM_LIMIT = 56 * 2 ** 20


def _params(sem, vmem=VMEM_LIMIT):
    return pltpu.CompilerParams(dimension_semantics=sem, vmem_limit_bytes=vmem)


def _silu(x):
    return x * jax.nn.sigmoid(x)


def _dot(a, b):
    return jnp.dot(a, b, preferred_element_type=F32)


def _dot_nt(a, b):
    return lax.dot_general(a, b, (((1,), (1,)), ((), ())), preferred_element_type=F32)


def _dot_f32x3(a01, x):
    x1 = x.astype(BF16)
    r1 = x - x1.astype(F32)
    x2 = r1.astype(BF16)
    x3 = (r1 - x2.astype(F32)).astype(BF16)
    return _dot(a01, x1) + _dot(a01, x2) + _dot(a01, x3)


def _dot_f32ish(a, b):
    a_hi, b_hi = a.astype(BF16), b.astype(BF16)
    a_lo, b_lo = (a - a_hi.astype(F32)).astype(BF16), (b - b_hi.astype(F32)).astype(BF16)
    return _dot(a_hi, b_hi) + (_dot(a_hi, b_lo) + _dot(a_lo, b_hi))


def _mask01(mask):
    return jnp.where(mask, 1.0, 0.0).astype(BF16)


def _modulate(x, g, shift, scale):
    y = x * lax.rsqrt(jnp.mean(x * x, axis=-1, keepdims=True) + EPS)
    return (y * g) * (1.0 + scale) + shift


def _cast_kernel(w_ref, o_ref):
    o_ref[...] = w_ref[...].astype(BF16)


def _to_bf16(w):
    layers, rows, cols = w.shape
    tr = 256
    assert rows % tr == 0
    spec = pl.BlockSpec((None, tr, cols), lambda l, r: (l, r, 0))
    return pl.pallas_call(
        _cast_kernel,
        out_shape=jax.ShapeDtypeStruct(w.shape, BF16),
        grid=(layers, rows // tr),
        in_specs=[spec],
        out_specs=spec,
        compiler_params=_params(("arbitrary", "arbitrary")),
        name="to_bf16",
    )(w)


def _adaln_kernel(cond_ref, w_ref, b_ref, o_ref):
    s = _silu(cond_ref[...]).astype(BF16)
    o_ref[0] = _dot(s, w_ref[0].astype(BF16)) + b_ref[0]


def _adaln(cond, w_mod, b_mod):
    depth, d, six_d = w_mod.shape
    r = cond.shape[0]
    tn = six_d // 4
    return pl.pallas_call(
        _adaln_kernel,
        out_shape=jax.ShapeDtypeStruct((depth, r, six_d), F32),
        grid=(depth, six_d // tn),
        in_specs=[
            pl.BlockSpec((r, d), lambda l, j: (0, 0)),
            pl.BlockSpec((1, d, tn), lambda l, j: (l, 0, j)),
            pl.BlockSpec((1, 1, tn), lambda l, j: (l, 0, j)),
        ],
        out_specs=pl.BlockSpec((1, r, tn), lambda l, j: (l, 0, j)),
        compiler_params=_params(("arbitrary", "arbitrary")),
        name="adaln",
    )(cond, w_mod, b_mod.reshape(depth, 1, six_d))


class _Layout:
    def __init__(self, batch, seq, dec_batch, dec_seq):
        self.batch, self.seq, self.dec_batch, self.dec_seq = batch, seq, dec_batch, dec_seq
        self.nc = batch * seq
        self.nl = dec_batch * dec_seq
        self.n = self.nc + self.nl
        self.mod_rows = -(-(1 + dec_batch) // 8) * 8
        assert seq % SEG == 0 and dec_seq % SEG == 0 and dec_seq % GRID_W == 0
        self.tm = math.gcd(self.nc, 1024)
        assert self.tm % SEG == 0 and dec_seq % self.tm == 0 and self.nc % dec_seq == 0

    def mod_row(self, first_row):
        return jnp.maximum((first_row - self.nc) // self.dec_seq + 1, 0)

    def stream_specs(self, parts, tm):
        ct = self.nc // tm
        off = ct if len(parts) == 1 else 0
        specs = (pl.BlockSpec((tm, D_MODEL), lambda i, *_: (jnp.minimum(i, ct - 1), 0)),
                 pl.BlockSpec((tm, D_MODEL), lambda i, *_: (off + jnp.maximum(i - ct, 0), 0)))
        return specs, (parts[0], parts[-1])

    def mod_spec(self, l, k, rows_per_step):
        base = (l * 6 + k) * self.mod_rows
        return pl.BlockSpec((1, 1, D_MODEL),
                            lambda i, *_: (base + self.mod_row(i * rows_per_step), 0, 0))


def _stream_tile(xa_ref, xb_ref, ctx_tiles):
    return jnp.where(pl.program_id(0) < ctx_tiles, xa_ref[...], xb_ref[...])


def _row_spec(l, k):
    return pl.BlockSpec((1, 1, D_MODEL), lambda i, *_: (l * 2 + k, 0, 0))


def _proj_kernel(xa_ref, xb_ref, g_ref, sh_ref, sc_ref, w_ref, wab_ref, proj_ref, ab_ref, h_scr, *, ctx_tiles):
    @pl.when(pl.program_id(1) == 0)
    def _():
        h = _modulate(_stream_tile(xa_ref, xb_ref, ctx_tiles), g_ref[0], sh_ref[0], sc_ref[0]).astype(BF16)
        h_scr[...] = h
        ab_ref[...] = _dot(h, wab_ref[...].astype(BF16))

    proj_ref[...] = _dot(h_scr[...], w_ref[...])


def _even_project(lay, x_parts, normg, modtab, w_in, w_ab, l):
    i_even = l // 2
    tm, tn = lay.tm, PROJ_TILE
    ncols = 3 * A_W + 4 * B_W
    x_specs, x_args = lay.stream_specs(x_parts, tm)
    return pl.pallas_call(
        functools.partial(_proj_kernel, ctx_tiles=lay.nc // tm),
        out_shape=(jax.ShapeDtypeStruct((lay.n, ncols), F32),
                   jax.ShapeDtypeStruct((lay.n, LANES), F32)),
        grid=(lay.n // tm, ncols // tn),
        in_specs=[
            *x_specs,
            _row_spec(l, 0),
            lay.mod_spec(l, 0, tm),
            lay.mod_spec(l, 1, tm),
            pl.BlockSpec((None, D_MODEL, tn), lambda i, j: (i_even, 0, j)),
            pl.BlockSpec((D_MODEL, LANES), lambda i, j: (0, 0)),
        ],
        out_specs=(pl.BlockSpec((tm, tn), lambda i, j: (i, j)),
                   pl.BlockSpec((tm, LANES), lambda i, j: (i, 0))),
        scratch_shapes=[pltpu.VMEM((tm, D_MODEL), BF16)],
        compiler_params=_params(("arbitrary", "arbitrary")),
        name="even_project",
    )(*x_args, normg, modtab, modtab, w_in, w_ab)


def _attend_heads(q, keys, values, bias_of):
    scale = DH_A ** -0.5
    per = LANES // DH_A
    lane = lax.broadcasted_iota(jnp.int32, (1, LANES), 1)
    own = [(lane >= s * DH_A) & (lane < (s + 1) * DH_A) for s in range(per)]
    groups = range(A_W // LANES)
    blocks = range(len(keys))
    gsl = lambda g: slice(g * LANES, (g + 1) * LANES)
    k16 = [[k[:, gsl(g)].astype(BF16) for k in keys] for g in groups]
    v16 = [[v[:, gsl(g)].astype(BF16) for v in values] for g in groups]
    heads = [(g, s) for g in groups for s in range(per)]
    qh = [jnp.where(own[s], q[:, gsl(g)], 0.0).astype(BF16) for g, s in heads]
    scores = []
    for h, (g, s) in enumerate(heads):
        row = []
        for i in blocks:
            sc = _dot_nt(qh[h], k16[g][i]) * scale
            b = bias_of(h, i)
            row.append(sc if b is None else sc + b)
        scores.append(row)
    peak = [functools.reduce(jnp.maximum, [jnp.max(sc, axis=-1, keepdims=True) for sc in row]) for row in scores]
    probs = [[jnp.exp(sc - m) for sc in row] for row, m in zip(scores, peak)]
    denom = [functools.reduce(jnp.add, [jnp.sum(p, axis=-1, keepdims=True) for p in row]) for row in probs]
    acc = [functools.reduce(jnp.add, [_dot(p.astype(BF16), v16[g][i]) for i, p in enumerate(row)])
           for row, (g, s) in zip(probs, heads)]
    outs = []
    for g in groups:
        o = jnp.zeros_like(acc[0])
        for s in range(per):
            h = g * per + s
            o = jnp.where(own[s], acc[h] / denom[h], o)
        outs.append(o)
    return jnp.concatenate(outs, axis=1)


def _ctx_attn_kernel(q_ref, k_ref, v_ref, o_ref):
    o_ref[...] = _attend_heads(q_ref[...], [k_ref[...]], [v_ref[...]], lambda h, i: None)


def _na_attn_kernel(q_ref, k_ref, v_ref, kc_ref, vc_ref, bias_ref, o_ref, k16, v16, kc16, vc16, *, rows, kh):
    r = pl.program_id(1)

    @pl.when(r == 0)
    def _():
        k16[...] = k_ref[...].astype(BF16)
        v16[...] = v_ref[...].astype(BF16)
        kc16[...] = kc_ref[...].astype(BF16)
        vc16[...] = vc_ref[...].astype(BF16)

    row_lo = jnp.clip(r - kh // 2, 0, rows - kh)
    start = pl.multiple_of(row_lo * GRID_W, GRID_W)
    kl = k16[pl.ds(start, kh * GRID_W), :]
    vl = v16[pl.ds(start, kh * GRID_W), :]
    o_ref[...] = _attend_heads(q_ref[...], [kl, kc16[...]], [vl, vc16[...]],
                               lambda h, i: bias_ref[h, 0] if i == 0 else None)


def _na_bias_table(rpb, rows, kh):
    r = np.arange(rows)
    row_idx = np.clip(r - kh // 2, 0, rows - kh)[:, None] + np.arange(kh)[None, :]
    dr = row_idx - r[:, None] + (NA_KH - 1)
    qcol = np.arange(GRID_W)
    kcol = np.arange(GRID_W)
    col_lo = np.clip(qcol - NA_KW // 2, 0, GRID_W - NA_KW)
    valid = (kcol[None, :] >= col_lo[:, None]) & (kcol[None, :] < col_lo[:, None] + NA_KW)
    dc = np.clip(kcol[None, :] - qcol[:, None], 1 - NA_KW, NA_KW - 1) + (NA_KW - 1)
    onehot = (dc[None, :, :] == np.arange(2 * NA_KW - 1)[:, None, None]).astype(np.float32)
    picked = jnp.einsum("hrjc,cqk->hrqjk", rpb[:, dr].astype(F32), onehot, precision=lax.Precision.HIGHEST)
    bias = jnp.where(valid[None, None, :, None, :], picked, NEG_INF)
    return bias.reshape(H_A, rows, GRID_W, kh * GRID_W)


def _attention(lay, proj, cache_k, cache_v, bias, i_even):
    ctx = pl.pallas_call(
        _ctx_attn_kernel,
        out_shape=jax.ShapeDtypeStruct((lay.nc, A_W), F32),
        grid=(lay.batch,),
        in_specs=[pl.BlockSpec((lay.seq, A_W), lambda b: (b, 0)),
                  pl.BlockSpec((lay.seq, A_W), lambda b: (b, 1)),
                  pl.BlockSpec((lay.seq, A_W), lambda b: (b, 2))],
        out_specs=pl.BlockSpec((lay.seq, A_W), lambda b: (b, 0)),
        compiler_params=_params(("arbitrary",)),
        name="ctx_attention",
    )(proj, proj, proj)

    rows = lay.dec_seq // GRID_W
    kh = min(NA_KH, rows)
    past = cache_k.shape[2]
    q0 = lay.nc // GRID_W
    b0 = lay.nc // lay.dec_seq
    lat = pl.pallas_call(
        functools.partial(_na_attn_kernel, rows=rows, kh=kh),
        out_shape=jax.ShapeDtypeStruct((lay.nl, A_W), F32),
        grid=(lay.dec_batch, rows),
        in_specs=[pl.BlockSpec((GRID_W, A_W), lambda b, r: (q0 + b * rows + r, 0)),
                  pl.BlockSpec((lay.dec_seq, A_W), lambda b, r: (b0 + b, 1)),
                  pl.BlockSpec((lay.dec_seq, A_W), lambda b, r: (b0 + b, 2)),
                  pl.BlockSpec((None, None, past, A_W), lambda b, r: (b, i_even, 0, 0)),
                  pl.BlockSpec((None, None, past, A_W), lambda b, r: (b, i_even, 0, 0)),
                  pl.BlockSpec((H_A, 1, GRID_W, kh * GRID_W), lambda b, r: (0, r, 0, 0))],
        out_specs=pl.BlockSpec((GRID_W, A_W), lambda b, r: (b * rows + r, 0)),
        scratch_shapes=[pltpu.VMEM((lay.dec_seq, A_W), BF16), pltpu.VMEM((lay.dec_seq, A_W), BF16),
                        pltpu.VMEM((past, A_W), BF16), pltpu.VMEM((past, A_W), BF16)],
        compiler_params=_params(("arbitrary", "arbitrary")),
        name="na_attention",
    )(proj, proj, proj, cache_k, cache_v, bias)
    return ctx, lat


def _seq_position(first, nc, seq, dec_seq):
    is_lat = first >= nc
    return jnp.where(is_lat, (first - nc) % dec_seq, first % seq), jnp.where(is_lat, dec_seq, seq)


GATE_GC, GATE_BETA, GATE_GT = 0, 2 * H_B, 4 * H_B


def _gate_table(ab, alog, dtb):
    n = SEG
    lane = lax.broadcasted_iota(jnp.int32, (1, LANES), 1)
    ri = lax.broadcasted_iota(jnp.int32, (n, n), 0)
    ci = lax.broadcasted_iota(jnp.int32, (n, n), 1)
    shift = int(math.log2(CHUNK))
    same = jnp.right_shift(ri, shift) == jnp.right_shift(ci, shift)
    g_all = -jnp.exp(alog) * jax.nn.softplus(ab + dtb)
    prefix = _dot_f32x3(_mask01(same & (ci <= ri)), g_all)
    total = _dot_f32x3(_mask01(same), g_all)
    gc = jnp.where(lane < H_B, prefix, total - prefix + g_all)
    return jnp.where(lane < GATE_BETA, gc,
                     jnp.where(lane < GATE_GT, jax.nn.sigmoid(ab), pltpu.roll(total, GATE_GT, 1)))


def _delta_prep_kernel(xp_ref, xc_ref, xn_ref, w_ref, ab_ref, alog_ref, dtb_ref, q_ref, k_ref, v_ref, g_ref, gt_ref,
                       *, nc, seq, dec_seq):
    table = _gate_table(ab_ref[...], alog_ref[...], dtb_ref[...])
    g_ref[...] = table
    gt_ref[...] = table.T
    off, t_len = _seq_position(pl.program_id(0) * SEG, nc, seq, dec_seq)
    x = xc_ref[...]
    row = lax.broadcasted_iota(jnp.int32, (SEG, 1), 0)
    before = jnp.where(off > 0, xp_ref[7:8, :], 0.0)
    after = jnp.where(off + SEG < t_len, xn_ref[0:1, :], 0.0)
    prev = jnp.where(row == 0, before, pltpu.roll(x, 1, 0))
    nxt = jnp.where(row == SEG - 1, after, pltpu.roll(x, SEG - 1, 0))
    w = w_ref[...]
    y = _silu(prev * w[0:1] + x * w[1:2] + nxt * w[2:3])

    def l2norm(a):
        return a * lax.rsqrt(jnp.sum(a * a, axis=-1, keepdims=True) + EPS)

    heads = lambda part: [y[:, (part * H_B + h) * DK:(part * H_B + h + 1) * DK] for h in range(H_B)]
    q_ref[...] = jnp.concatenate([l2norm(a) * (DK ** -0.5) for a in heads(0)], axis=1)
    k_ref[...] = jnp.concatenate([l2norm(a) for a in heads(1)], axis=1)
    v_ref[...] = y[:, 2 * B_W:]


def _delta_prep(lay, proj, conv_w, ab, alog, dtb, i_even):
    width = 3 * B_W
    cblk = 3 * A_W // width
    assert cblk * width == 3 * A_W
    per = SEG // 8
    n8 = lay.n // 8
    ospec = pl.BlockSpec((SEG, B_W), lambda i: (i, 0))
    shape = jax.ShapeDtypeStruct((lay.n, B_W), F32)
    row = pl.BlockSpec((1, LANES), lambda i: (0, 0))
    return pl.pallas_call(
        functools.partial(_delta_prep_kernel, nc=lay.nc, seq=lay.seq, dec_seq=lay.dec_seq),
        out_shape=(shape, shape, shape,
                   jax.ShapeDtypeStruct((lay.n, LANES), F32), jax.ShapeDtypeStruct((LANES, lay.n), F32)),
        grid=(lay.n // SEG,),
        in_specs=[pl.BlockSpec((8, width), lambda i: (jnp.maximum(i * per - 1, 0), cblk)),
                  pl.BlockSpec((SEG, width), lambda i: (i, cblk)),
                  pl.BlockSpec((8, width), lambda i: (jnp.minimum((i + 1) * per, n8 - 1), cblk)),
                  pl.BlockSpec((None, CONV_K, width), lambda i: (i_even, 0, 0)),
                  pl.BlockSpec((SEG, LANES), lambda i: (i, 0)), row, row],
        out_specs=(ospec, ospec, ospec,
                   pl.BlockSpec((SEG, LANES), lambda i: (i, 0)), pl.BlockSpec((LANES, SEG), lambda i: (0, i))),
        compiler_params=_params(("arbitrary",)),
        name="delta_prep",
    )(proj, proj, proj, conv_w, ab, alog, dtb)


def _delta_masks(d):
    n = SEG
    ri = lax.broadcasted_iota(jnp.int32, (n, n), 0)
    ci = lax.broadcasted_iota(jnp.int32, (n, n), 1)
    shift = int(math.log2(CHUNK))
    same = jnp.right_shift(ri, shift) == jnp.right_shift(ci, shift)
    incl = same & ((ci <= ri) if d == 0 else (ci >= ri))
    strict = same & ((ci < ri) if d == 0 else (ci > ri))
    levels = []
    for level in range(shift):
        bi, bj = jnp.right_shift(ri, level), jnp.right_shift(ci, level)
        siblings = jnp.right_shift(bi, 1) == jnp.right_shift(bj, 1)
        levels.append(siblings & ((bi > bj) if d == 0 else (bi < bj)))
    return incl, strict, jnp.where(ri == ci, 1.0, 0.0), levels


def _delta_chains(chains):
    n = SEG
    n_chunks = n // CHUNK
    each = lambda fn, *lists: [fn(*args) for args in zip(*lists)]
    ds, masks, qs, ks, vs, betas, gcs, gts, gc_rows, gt_rows, states = (list(t) for t in zip(*chains))
    incls, stricts = [m[0] for m in masks], [m[1] for m in masks]
    n_levels = len(masks[0][3])

    decays = each(lambda m, gc, gr: jnp.where(m, jnp.exp(jnp.where(m, gc - gr, 0.0)), 0.0), incls, gcs, gc_rows)
    kbs = each(lambda k, b: k * b, ks, betas)
    k16s = [k.astype(BF16) for k in ks]
    grams = each(lambda kb, k16: _dot_nt(kb.astype(BF16), k16), kbs, k16s)
    lmats = each(lambda m, g, dec: jnp.where(m, g * dec, 0.0), stricts, grams, decays)

    xs = each(lambda m, lm: m[2] - jnp.where(m[3][0], lm, 0.0), masks, lmats)
    for level in range(1, n_levels):
        x16s = [x.astype(BF16) for x in xs]
        cs = each(lambda m, lm: jnp.where(m[3][level], lm, 0.0).astype(BF16), masks, lmats)
        xcs = each(lambda x16, c: _dot(x16, c).astype(BF16), x16s, cs)
        xs = each(lambda x, xc, x16: x - _dot(xc, x16), xs, xcs, x16s)

    egs = [jnp.exp(gc) for gc in gcs]
    rhss = each(lambda v, b, kb, eg: jnp.concatenate([v * b, kb * eg], axis=1).astype(BF16), vs, betas, kbs, egs)
    uws = each(lambda x, rhs: _dot(x.astype(BF16), rhs), xs, rhss)
    qks = each(lambda q, k16: _dot_nt(q.astype(BF16), k16), qs, k16s)
    attns = each(lambda m, qk, dec: jnp.where(m, qk * dec, 0.0).astype(BF16), incls, qks, decays)
    qgs = each(lambda q, eg: (q * eg).astype(BF16), qs, egs)
    kd_ts = each(lambda k, gt_r, gc_r: (k.T * jnp.exp(gt_r - gc_r)).astype(BF16), ks, gt_rows, gc_rows)
    gls = [jnp.exp(gt) for gt in gts]

    v_new = [[None] * n_chunks for _ in chains]
    o_state = [[None] * n_chunks for _ in chains]
    for step in range(n_chunks):
        for i, d in enumerate(ds):
            c = step if d == 0 else n_chunks - 1 - step
            rs = slice(c * CHUNK, (c + 1) * CHUNK)
            s16 = states[i].astype(BF16)
            vn = uws[i][rs, :DV] - _dot(uws[i][rs, DV:].astype(BF16), s16)
            o_state[i][c] = _dot(qgs[i][rs], s16)
            v_new[i][c] = vn
            pieces = [jnp.zeros((c * CHUNK, DV), BF16), vn.astype(BF16),
                      jnp.zeros((n - (c + 1) * CHUNK, DV), BF16)]
            padded = jnp.concatenate([p for p in pieces if p.shape[0]], axis=0)
            states[i] = states[i] * gls[i][c * CHUNK:c * CHUNK + 1, :] + _dot(kd_ts[i], padded)
    outs = each(lambda os, a, vn: jnp.concatenate(os, axis=0) + _dot(a, jnp.concatenate(vn, axis=0).astype(BF16)),
                o_state, attns, v_new)
    return list(zip(outs, states))


def _delta_kernel(plan_ref, *refs):
    dir_refs = (refs[0:5], refs[5:10])
    s0_ref, of_ref, ob_ref, sfin_ref, s_scr = refs[10:]
    g = pl.program_id(0)
    first, last, has_s0 = plan_ref[2, g] == 1, plan_ref[3, g] == 1, plan_ref[4, g] == 1

    @pl.when(first)
    def _():
        s_scr[...] = jnp.where(has_s0, s0_ref[...], 0.0)

    chains = []
    for d, (q_ref, k_ref, v_ref, g_ref, gt_ref) in enumerate(dir_refs):
        masks = _delta_masks(d)
        q, k, v, gates, gates_t = q_ref[...], k_ref[...], v_ref[...], g_ref[...], gt_ref[...]
        for h in range(H_B):
            sl = slice(h * DK, (h + 1) * DK)
            col = d * H_B + h
            pick = lambda base: gates[:, base + col:base + col + 1]
            pick_t = lambda base: gates_t[base + col:base + col + 1, :]
            chains.append((d, masks, q[:, sl], k[:, sl], v[:, sl], pick(GATE_BETA), pick(GATE_GC),
                           pick(GATE_GT), pick_t(GATE_GC), pick_t(GATE_GT), s_scr[d, h]))
    results = _delta_chains(chains)
    for d, o_ref in enumerate((of_ref, ob_ref)):
        o_ref[...] = jnp.concatenate([results[d * H_B + h][0] for h in range(H_B)], axis=1)
        for h in range(H_B):
            s_scr[d, h] = results[d * H_B + h][1]

    @pl.when(last & jnp.logical_not(has_s0))
    def _():
        sfin_ref[...] = s_scr[...]


def _delta_plan(lay):
    rows = []
    for kind, n_seq, t in ((0, lay.batch, lay.seq), (1, lay.dec_batch, lay.dec_seq)):
        nseg = t // SEG
        base = 0 if kind == 0 else lay.nc // SEG
        for b in range(n_seq):
            for s in range(nseg):
                rows.append((base + b * nseg + s, base + b * nseg + nseg - 1 - s, int(s == 0), int(s == nseg - 1),
                             kind, b if kind else 0, b if kind == 0 else lay.batch - 1))
    return jnp.asarray(np.array(rows, np.int32).T)


def _delta_net(lay, q, k, v, gates, gates_t, s0):
    plan = _delta_plan(lay)
    in_specs, args = [], []
    for d in (0, 1):
        in_specs += [pl.BlockSpec((SEG, B_W), lambda g, plan, d=d: (plan[d, g], 0))] * 3
        in_specs += [pl.BlockSpec((SEG, LANES), lambda g, plan, d=d: (plan[d, g], 0)),
                     pl.BlockSpec((LANES, SEG), lambda g, plan, d=d: (0, plan[d, g]))]
        args += [q, k, v, gates, gates_t]
    state = lambda row: pl.BlockSpec((None, 2, H_B, DK, DV), lambda g, plan: (plan[row, g], 0, 0, 0, 0))
    oshape = jax.ShapeDtypeStruct((lay.n, B_W), F32)
    grid_spec = pltpu.PrefetchScalarGridSpec(
        num_scalar_prefetch=1,
        grid=(lay.n // SEG,),
        in_specs=in_specs + [state(5)],
        out_specs=(pl.BlockSpec((SEG, B_W), lambda g, plan: (plan[0, g], 0)),
                   pl.BlockSpec((SEG, B_W), lambda g, plan: (plan[1, g], 0)), state(6)),
        scratch_shapes=[pltpu.VMEM((2, H_B, DK, DV), F32)],
    )
    return pl.pallas_call(
        _delta_kernel,
        out_shape=(oshape, oshape, jax.ShapeDtypeStruct((lay.batch, 2, H_B, DK, DV), F32)),
        grid_spec=grid_spec,
        compiler_params=_params(("arbitrary",)),
        name="delta_net",
    )(plan, *args, s0)


def _even_out_kernel(oac_ref, oal_ref, of_ref, ob_ref, z_ref, gain_ref, w_ref, xa_ref, xb_ref, gt_ref,
                     g2_ref, sh2_ref, sc2_ref, xo_ref, h2_ref, *, ctx_tiles):
    ob = of_ref[...] + ob_ref[...]
    z = z_ref[...]
    oa = _stream_tile(oac_ref, oal_ref, ctx_tiles)
    parts = [oa.astype(BF16)]
    for h in range(H_B):
        sl = slice(h * DV, (h + 1) * DV)
        o_h = ob[:, sl]
        y = o_h * lax.rsqrt(jnp.mean(o_h * o_h, axis=-1, keepdims=True) + EPS) * gain_ref[...] * _silu(z[:, sl])
        parts.append(y.astype(BF16))
    mix = _dot(jnp.concatenate(parts, axis=1), w_ref[...])
    xn = _stream_tile(xa_ref, xb_ref, ctx_tiles) + gt_ref[0] * mix
    xo_ref[...] = xn
    h2_ref[...] = _modulate(xn, g2_ref[0], sh2_ref[0], sc2_ref[0]).astype(BF16)


def _even_output(lay, oa_ctx, oa_lat, o_f, o_b, proj, o_gain, w_out, x_parts, normg, modtab, l):
    i_even = l // 2
    tm = lay.tm
    ctx_tiles = lay.nc // tm
    x_specs, x_args = lay.stream_specs(x_parts, tm)
    zcol = (3 * A_W + 3 * B_W) // B_W
    half = lambda: pl.BlockSpec((tm, B_W), lambda i: (i, 0))
    full = lambda: pl.BlockSpec((tm, D_MODEL), lambda i: (i, 0))
    return pl.pallas_call(
        functools.partial(_even_out_kernel, ctx_tiles=ctx_tiles),
        out_shape=(jax.ShapeDtypeStruct((lay.n, D_MODEL), F32),
                   jax.ShapeDtypeStruct((lay.n, D_MODEL), BF16)),
        grid=(lay.n // tm,),
        in_specs=[pl.BlockSpec((tm, A_W), lambda i: (jnp.minimum(i, ctx_tiles - 1), 0)),
                  pl.BlockSpec((tm, A_W), lambda i: (jnp.maximum(i - ctx_tiles, 0), 0)),
                  half(), half(),
                  pl.BlockSpec((tm, B_W), lambda i: (i, zcol)),
                  pl.BlockSpec((None, 1, DV), lambda i: (i_even, 0, 0)),
                  pl.BlockSpec((None, D_MODEL, D_MODEL), lambda i: (i_even, 0, 0)),
                  *x_specs,
                  lay.mod_spec(l, 2, tm),
                  _row_spec(l, 1), lay.mod_spec(l, 3, tm), lay.mod_spec(l, 4, tm)],
        out_specs=(full(), full()),
        compiler_params=_params(("arbitrary",)),
        name="even_output",
    )(oa_ctx, oa_lat, o_f, o_b, proj, o_gain, w_out, *x_args, modtab, normg, modtab, modtab)


def _ffn_kernel(h_ref, wg_ref, wu_ref, wd_ref, x_ref, gt_ref, o_ref, acc_ref):
    j = pl.program_id(1)

    @pl.when(j == 0)
    def _():
        acc_ref[...] = jnp.zeros_like(acc_ref)

    h = h_ref[...]
    a = (_silu(_dot(h, wg_ref[...])) * _dot(h, wu_ref[...])).astype(BF16)
    acc_ref[...] += _dot(a, wd_ref[...])

    @pl.when(j == pl.num_programs(1) - 1)
    def _():
        o_ref[...] = x_ref[...] + gt_ref[0] * acc_ref[...]


def _dense_ffn(lay, h2, x, ffn_gate, ffn_up, ffn_down, modtab, l):
    i_even = l // 2
    tm, tf = lay.tm, FF_TILE
    return pl.pallas_call(
        _ffn_kernel,
        out_shape=jax.ShapeDtypeStruct((lay.n, D_MODEL), F32),
        grid=(lay.n // tm, D_FF // tf),
        in_specs=[pl.BlockSpec((tm, D_MODEL), lambda i, j: (i, 0)),
                  pl.BlockSpec((None, D_MODEL, tf), lambda i, j: (i_even, 0, j)),
                  pl.BlockSpec((None, D_MODEL, tf), lambda i, j: (i_even, 0, j)),
                  pl.BlockSpec((None, tf, D_MODEL), lambda i, j: (i_even, j, 0)),
                  pl.BlockSpec((tm, D_MODEL), lambda i, j: (i, 0)),
                  lay.mod_spec(l, 5, tm)],
        out_specs=pl.BlockSpec((tm, D_MODEL), lambda i, j: (i, 0)),
        scratch_shapes=[pltpu.VMEM((tm, D_MODEL), F32)],
        compiler_params=_params(("arbitrary", "arbitrary")),
        name="dense_ffn",
    )(h2, ffn_gate, ffn_up, ffn_down, x, modtab)


def _pool_kernel(xp_ref, xc_ref, xn_ref, g1_ref, sh1_ref, sc1_ref, gt1_ref, pw_ref, ps_ref,
                 g2_ref, sh2_ref, sc2_ref, wr_ref, xo_ref, h2_ref, ridx_ref, rw_ref, cnt_ref, count_scr,
                 *, nc, seq, dec_seq):
    i = pl.program_id(0)
    off, t_len = _seq_position(i * SEG, nc, seq, dec_seq)

    g1, sh1, sc1 = g1_ref[0], sh1_ref[0], sc1_ref[0]
    x = xc_ref[...]
    h_cur = _modulate(x, g1, sh1, sc1)
    h_cat = jnp.concatenate([_modulate(xp_ref[...], g1, sh1, sc1), h_cur,
                             _modulate(xn_ref[...], g1, sh1, sc1)], axis=0)

    t = off + lax.broadcasted_iota(jnp.int32, (SEG, 1), 0)
    p = off - POOL_HALO + lax.broadcasted_iota(jnp.int32, (1, SEG + 2 * POOL_HALO), 1)
    mixes = []
    for gi, w in enumerate(POOL_WINDOWS):
        sl = slice(gi * POOL_G, (gi + 1) * POOL_G)
        lo = jnp.maximum(t - w // 2, 0)
        hi = jnp.minimum(t + (w - w // 2), t_len)
        band = _mask01((p >= lo) & (p < hi))
        hg = h_cat[:, sl]
        hg_hi = hg.astype(BF16)
        hg_lo = (hg - hg_hi.astype(F32)).astype(BF16)
        window_sum = _dot(band, hg_hi) + _dot(band, hg_lo)
        y = (window_sum / (hi - lo).astype(F32) - h_cur[:, sl]).astype(BF16)
        mixes.append(_dot(y, pw_ref[gi].astype(BF16)))
    mix = jnp.concatenate(mixes, axis=1) * ps_ref[...]
    xn = x + gt1_ref[0] * mix
    xo_ref[...] = xn
    h2 = _modulate(xn, g2_ref[0], sh2_ref[0], sc2_ref[0])
    h2_ref[...] = h2

    logits = _dot_f32ish(h2, wr_ref[...])
    lane = lax.broadcasted_iota(jnp.int32, logits.shape, 1)
    lane_f = lane.astype(F32)
    lg = jnp.where(lane < N_EXP, logits, -jnp.inf)
    m1 = jnp.max(lg, axis=-1, keepdims=True)
    i1 = jnp.min(jnp.where(lg == m1, lane_f, float(LANES)), axis=-1, keepdims=True)
    lg2 = jnp.where(lane_f == i1, -jnp.inf, lg)
    m2 = jnp.max(lg2, axis=-1, keepdims=True)
    i2 = jnp.min(jnp.where(lg2 == m2, lane_f, float(LANES)), axis=-1, keepdims=True)
    e = jnp.exp(m2 - m1)
    w1 = 1.0 / (1.0 + e)
    w2 = e / (1.0 + e)
    rw_ref[...] = jnp.where(lane == 0, w1, jnp.where(lane == 1, w2, 0.0))

    @pl.when(i == 0)
    def _():
        count_scr[...] = jnp.zeros_like(count_scr)

    hit1, hit2 = lane_f == i1, lane_f == i2
    picks = jnp.where(hit1 | hit2, 1.0, 0.0)
    ri = lax.broadcasted_iota(jnp.int32, (SEG, SEG), 0)
    ci = lax.broadcasted_iota(jnp.int32, (SEG, SEG), 1)
    before = _dot(_mask01(ci < ri), picks.astype(BF16)) + count_scr[...]
    r1 = jnp.sum(jnp.where(hit1, before, 0.0), axis=-1, keepdims=True)
    r2 = jnp.sum(jnp.where(hit2, before, 0.0), axis=-1, keepdims=True)
    count_scr[...] += jnp.sum(picks, axis=0, keepdims=True)
    route = jnp.where(lane == 0, i1, jnp.where(lane == 1, i2, jnp.where(lane == 2, r1, jnp.where(lane == 3, r2, 0.0))))
    ridx_ref[...] = route.astype(jnp.int32)
    cnt_ref[...] = jnp.broadcast_to(count_scr[...], cnt_ref.shape)


def _pool_and_route(lay, x, normg, modtab, pool_w, pool_scale, w_router, l):
    i_odd = l // 2
    nblk = lay.n // SEG
    blk = lambda imap: pl.BlockSpec((SEG, D_MODEL), imap)
    halo = lambda imap: pl.BlockSpec((POOL_HALO, D_MODEL), imap)
    per = SEG // POOL_HALO
    lane_blk = pl.BlockSpec((SEG, LANES), lambda i: (i, 0))
    return pl.pallas_call(
        functools.partial(_pool_kernel, nc=lay.nc, seq=lay.seq, dec_seq=lay.dec_seq),
        out_shape=(jax.ShapeDtypeStruct((lay.n, D_MODEL), F32),
                   jax.ShapeDtypeStruct((lay.n, D_MODEL), F32),
                   jax.ShapeDtypeStruct((lay.n, LANES), jnp.int32),
                   jax.ShapeDtypeStruct((lay.n, LANES), F32),
                   jax.ShapeDtypeStruct((8, LANES), F32)),
        grid=(nblk,),
        in_specs=[halo(lambda i: (jnp.maximum(i * per - 1, 0), 0)),
                  blk(lambda i: (i, 0)),
                  halo(lambda i: (jnp.minimum((i + 1) * per, nblk * per - 1), 0)),
                  _row_spec(l, 0), lay.mod_spec(l, 0, SEG), lay.mod_spec(l, 1, SEG), lay.mod_spec(l, 2, SEG),
                  pl.BlockSpec((None, len(POOL_WINDOWS), POOL_G, POOL_G), lambda i: (i_odd, 0, 0, 0)),
                  pl.BlockSpec((None, 1, D_MODEL), lambda i: (i_odd, 0, 0)),
                  _row_spec(l, 1), lay.mod_spec(l, 3, SEG), lay.mod_spec(l, 4, SEG),
                  pl.BlockSpec((None, D_MODEL, LANES), lambda i: (i_odd, 0, 0))],
        out_specs=(blk(lambda i: (i, 0)), blk(lambda i: (i, 0)), lane_blk, lane_blk,
                   pl.BlockSpec((8, LANES), lambda i: (0, 0))),
        scratch_shapes=[pltpu.VMEM((1, LANES), F32)],
        compiler_params=_params(("arbitrary",)),
        name="pool_route",
    )(x, x, x, normg, modtab, modtab, modtab, pool_w, pool_scale, normg, modtab, modtab, w_router)


def _dispatch_kernel(pad_ref, dest_ref, h_ref, xs_hbm, zero_blk, sem, *, tokens):
    def row_copy(t, k):
        return pltpu.make_async_copy(h_ref.at[pl.ds(t, 1)], xs_hbm.at[pl.ds(dest_ref[0, 2 * t + k], 1)], sem)

    def wait_rows(count):
        one_row = pltpu.make_async_copy(h_ref.at[pl.ds(0, 1)], xs_hbm.at[pl.ds(0, 1)], sem)
        for _ in range(count):
            one_row.wait()

    def start(t, carry):
        row_copy(t, 0).start()
        row_copy(t, 1).start()
        return carry

    def wait(t, carry):
        wait_rows(2)
        return carry

    @pl.when(pl.program_id(0) == 0)
    def _():
        zero_blk[...] = jnp.zeros_like(zero_blk)
        for e in range(N_EXP):
            def zero_start(r, carry, e=e):
                pltpu.make_async_copy(zero_blk.at[pl.ds(0, 1)], xs_hbm.at[pl.ds(pad_ref[0, e] + r, 1)], sem).start()
                return carry

            def zero_wait(r, carry):
                wait_rows(1)
                return carry

            lax.fori_loop(0, pad_ref[1, e], zero_start, 0)
            lax.fori_loop(0, pad_ref[1, e], zero_wait, 0)

        def tail_copy(b):
            rows = pl.ds(pl.multiple_of(pad_ref[0, N_EXP] + b * MOE_SUB, MOE_SUB), MOE_SUB)
            return pltpu.make_async_copy(zero_blk, xs_hbm.at[rows], sem)

        def tail_start(b, carry):
            tail_copy(b).start()
            return carry

        def tail_wait(b, carry):
            tail_copy(b).wait()
            return carry

        lax.fori_loop(0, pad_ref[1, N_EXP], tail_start, 0)
        lax.fori_loop(0, pad_ref[1, N_EXP], tail_wait, 0)

    lax.fori_loop(0, tokens, start, 0, unroll=16)
    lax.fori_loop(0, tokens, wait, 0, unroll=16)


def _moe_dispatch(h2, dest, pad, p_max):
    n = h2.shape[0]
    tokens = math.gcd(n, 1024)
    grid_spec = pltpu.PrefetchScalarGridSpec(
        num_scalar_prefetch=1,
        grid=(n // tokens,),
        in_specs=[pl.BlockSpec((None, 1, 2 * tokens), lambda i, pad: (i, 0, 0), memory_space=pltpu.SMEM),
                  pl.BlockSpec((tokens, D_MODEL), lambda i, pad: (i, 0))],
        out_specs=pl.BlockSpec(memory_space=pl.ANY),
        scratch_shapes=[pltpu.VMEM((MOE_SUB, D_MODEL), F32), pltpu.SemaphoreType.DMA],
    )
    return pl.pallas_call(
        functools.partial(_dispatch_kernel, tokens=tokens),
        out_shape=jax.ShapeDtypeStruct((p_max, D_MODEL), F32),
        grid_spec=grid_spec,
        compiler_params=_params(("arbitrary",)),
        name="moe_dispatch",
    )(pad, dest.reshape(n // tokens, 1, 2 * tokens), h2)


def _moe_kernel(ge_ref, gs_ref, gn_ref, xs_hbm, wg_ref, wu_ref, wd_ref, ys_hbm, xbuf, x16, acc, sem_in, sem_out):
    del ge_ref
    g = pl.program_id(0)
    j = pl.program_id(1)
    last_j = pl.num_programs(1) - 1
    nsub = gn_ref[g]
    row0 = gs_ref[g]

    def in_copy(first_row, s):
        rows = pl.ds(pl.multiple_of(first_row + s * MOE_SUB, MOE_SUB), MOE_SUB)
        return pltpu.make_async_copy(xs_hbm.at[rows], xbuf.at[pl.ds(pl.multiple_of(s * MOE_SUB, MOE_SUB), MOE_SUB)],
                                     sem_in.at[s])

    def out_copy(first, n_rows):
        rows = pl.ds(pl.multiple_of(row0 + first, MOE_SUB), n_rows)
        return pltpu.make_async_copy(acc.at[pl.ds(pl.multiple_of(first, MOE_SUB), n_rows)], ys_hbm.at[rows], sem_out)

    def loop(count, fn):
        def body(s, carry):
            fn(s)
            return carry
        lax.fori_loop(0, count, body, 0)

    @pl.when((g == 0) & (j == 0))
    def _():
        loop(nsub, lambda s: in_copy(row0, s).start())

    nxt = jnp.minimum(g + 1, pl.num_programs(0) - 1)

    @pl.when((j == 1) & (g + 1 < pl.num_programs(0)))
    def _():
        loop(gn_ref[nxt], lambda s: in_copy(gs_ref[nxt], s).start())

    wg = wg_ref[...].astype(BF16)
    wu = wu_ref[...].astype(BF16)
    wd = wd_ref[...].astype(BF16)

    def tile(first_sub, n_sub):
        n_rows = n_sub * MOE_SUB
        first = pl.multiple_of(first_sub * MOE_SUB, MOE_SUB)
        rows = pl.ds(first, n_rows)

        @pl.when(j == 0)
        def _():
            for k in range(n_sub):
                in_copy(row0, first_sub + k).wait()
            x16[rows, :] = xbuf[rows, :].astype(BF16)

        xs = x16[rows, :]
        a = (_silu(_dot(xs, wg)) * _dot(xs, wu)).astype(BF16)
        contrib = _dot(a, wd)

        @pl.when(j == 0)
        def _():
            acc[rows, :] = contrib

        @pl.when(j > 0)
        def _():
            acc[rows, :] += contrib

        @pl.when(j == last_j)
        def _():
            out_copy(first, n_rows).start()

    n_quads = lax.shift_right_logical(nsub, 2)
    has_two = (nsub & 2) == 2
    has_one = (nsub & 1) == 1
    loop(n_quads, lambda p: tile(4 * p, 4))

    @pl.when(has_two)
    def _():
        tile(4 * n_quads, 2)

    @pl.when(has_one)
    def _():
        tile(nsub - 1, 1)

    @pl.when(j == last_j)
    def _():
        loop(n_quads, lambda p: out_copy(4 * p * MOE_SUB, 4 * MOE_SUB).wait())

        @pl.when(has_two)
        def _():
            out_copy(4 * n_quads * MOE_SUB, 2 * MOE_SUB).wait()

        @pl.when(has_one)
        def _():
            out_copy((nsub - 1) * MOE_SUB, MOE_SUB).wait()


def _sorted_rows(n_tok):
    return 2 * n_tok + N_EXP * MOE_SUB


def _moe_plan(ridx, counts, n_tok):
    n_pairs = 2 * n_tok
    group_rows = MOE_SUB * MOE_GROUP
    g_max = -(-n_pairs // group_rows) + N_EXP
    counts = counts[0, :N_EXP].astype(jnp.int32)
    nsub_e = (counts + MOE_SUB - 1) // MOE_SUB
    region = nsub_e * MOE_SUB
    start_e = jnp.cumsum(region) - region
    expert = ridx[:, 0:2]
    is_e = expert[:, :, None] == jnp.arange(N_EXP, dtype=jnp.int32)[None, None, :]
    dest = jnp.sum(jnp.where(is_e, start_e[None, None, :], 0), axis=-1) + ridx[:, 2:4]

    ngrp_e = (nsub_e + MOE_GROUP - 1) // MOE_GROUP
    gend = jnp.cumsum(ngrp_e)
    gstart = gend - ngrp_e
    total = gend[-1]
    gid = jnp.arange(g_max, dtype=jnp.int32)
    ge = jnp.minimum(jnp.searchsorted(gend, gid, side="right").astype(jnp.int32), N_EXP - 1)
    kk = gid - gstart[ge]
    live = gid < total
    last_e = jnp.minimum(jnp.searchsorted(gend, total - 1, side="right").astype(jnp.int32), N_EXP - 1)
    g_expert = jnp.where(live, ge, last_e)
    g_row = jnp.where(live, start_e[ge] + kk * group_rows, 0)
    g_nsub = jnp.where(live, jnp.clip(nsub_e[ge] - kk * MOE_GROUP, 0, MOE_GROUP), 0)
    used = jnp.sum(region)
    p_max = _sorted_rows(n_tok)
    pad = jnp.stack([jnp.append(start_e + counts, used),
                     jnp.append(region - counts, (p_max - used) // MOE_SUB)]).astype(jnp.int32)
    return dest, pad, g_expert.astype(jnp.int32), g_row.astype(jnp.int32), g_nsub.astype(jnp.int32)


def _moe_experts(xs, g_expert, g_row, g_nsub, moe_gate, moe_up, moe_down, i_odd):
    p_max = xs.shape[0]
    g_max = g_expert.shape[0]
    tf = FF_TILE
    n_j = D_FF // tf
    group_rows = MOE_SUB * MOE_GROUP
    tile_of = lambda g, j, gn: jnp.where(gn[g] > 0, j, n_j - 1)
    grid_spec = pltpu.PrefetchScalarGridSpec(
        num_scalar_prefetch=3,
        grid=(g_max, n_j),
        in_specs=[pl.BlockSpec(memory_space=pl.ANY),
                  pl.BlockSpec((None, None, D_MODEL, tf),
                               lambda g, j, ge, gs, gn: (i_odd, ge[g], 0, tile_of(g, j, gn))),
                  pl.BlockSpec((None, None, D_MODEL, tf),
                               lambda g, j, ge, gs, gn: (i_odd, ge[g], 0, tile_of(g, j, gn))),
                  pl.BlockSpec((None, None, tf, D_MODEL),
                               lambda g, j, ge, gs, gn: (i_odd, ge[g], tile_of(g, j, gn), 0))],
        out_specs=pl.BlockSpec(memory_space=pl.ANY),
        scratch_shapes=[pltpu.VMEM((group_rows, D_MODEL), F32),
                        pltpu.VMEM((group_rows, D_MODEL), BF16),
                        pltpu.VMEM((group_rows, D_MODEL), F32),
                        pltpu.SemaphoreType.DMA((MOE_GROUP,)), pltpu.SemaphoreType.DMA],
    )
    return pl.pallas_call(
        _moe_kernel,
        out_shape=jax.ShapeDtypeStruct((p_max, D_MODEL), F32),
        grid_spec=grid_spec,
        input_output_aliases={3: 0},
        compiler_params=_params(("arbitrary", "arbitrary")),
        name="moe_experts",
    )(g_expert, g_row, g_nsub, xs, moe_gate, moe_up, moe_down)


def _combine_kernel(dest_ref, ys_hbm, x_ref, rw_ref, gt_ref, fg_ref, *rest, tokens, ctx_tiles):
    out_refs, (ybuf, sem) = rest[:-2], rest[-2:]
    def row_copy(t, k):
        return pltpu.make_async_copy(ys_hbm.at[pl.ds(dest_ref[0, 2 * t + k], 1)], ybuf.at[k, pl.ds(t, 1)], sem)

    def start(t, carry):
        row_copy(t, 0).start()
        row_copy(t, 1).start()
        return carry

    def wait(t, carry):
        one_row = pltpu.make_async_copy(ys_hbm.at[pl.ds(0, 1)], ybuf.at[0, pl.ds(0, 1)], sem)
        one_row.wait()
        one_row.wait()
        return carry

    lax.fori_loop(0, tokens, start, 0, unroll=16)
    lax.fori_loop(0, tokens, wait, 0, unroll=16)
    rw = rw_ref[...]
    y = rw[:, 0:1] * ybuf[0] + rw[:, 1:2] * ybuf[1]
    xn = x_ref[...] + gt_ref[0] * y
    if ctx_tiles is None:
        out_refs[0][...] = xn
        return
    xn = xn * lax.rsqrt(jnp.mean(xn * xn, axis=-1, keepdims=True) + EPS) * fg_ref[...]
    is_ctx = pl.program_id(0) < ctx_tiles

    @pl.when(is_ctx)
    def _():
        out_refs[0][...] = xn

    @pl.when(jnp.logical_not(is_ctx))
    def _():
        out_refs[1][...] = xn


def _moe_combine(lay, x, ys, dest, rw, modtab, final_g, l, final):
    tm = lay.tm
    full = lambda: pl.BlockSpec((tm, D_MODEL), lambda i: (i, 0))
    ctx_tiles = lay.nc // tm if final else None
    if final:
        out_shape = (jax.ShapeDtypeStruct((lay.nc, D_MODEL), F32), jax.ShapeDtypeStruct((lay.nl, D_MODEL), F32))
        out_specs = (pl.BlockSpec((tm, D_MODEL), lambda i: (jnp.minimum(i, ctx_tiles - 1), 0)),
                     pl.BlockSpec((tm, D_MODEL), lambda i: (jnp.maximum(i - ctx_tiles, 0), 0)))
    else:
        out_shape, out_specs = jax.ShapeDtypeStruct((lay.n, D_MODEL), F32), full()
    return pl.pallas_call(
        functools.partial(_combine_kernel, tokens=tm, ctx_tiles=ctx_tiles),
        out_shape=out_shape,
        grid=(lay.n // tm,),
        in_specs=[pl.BlockSpec((None, 1, 2 * tm), lambda i: (i, 0, 0), memory_space=pltpu.SMEM),
                  pl.BlockSpec(memory_space=pl.ANY),
                  full(),
                  pl.BlockSpec((tm, LANES), lambda i: (i, 0)),
                  lay.mod_spec(l, 5, tm),
                  pl.BlockSpec((1, D_MODEL), lambda i: (0, 0))],
        out_specs=out_specs,
        scratch_shapes=[pltpu.VMEM((2, tm, D_MODEL), F32), pltpu.SemaphoreType.DMA],
        compiler_params=_params(("arbitrary",)),
        name="moe_combine",
    )(dest.reshape(lay.n // tm, 1, 2 * tm), ys, x, rw, modtab, final_g)


def _kv_kernel(*refs, n_layers):
    k_refs, v_refs = refs[:n_layers], refs[n_layers:2 * n_layers]
    ko_ref, vo_ref = refs[2 * n_layers:]
    layer = pl.program_id(1)
    for srcs, o_ref in ((k_refs, ko_ref), (v_refs, vo_ref)):
        x = srcs[0][...]
        for i in range(1, n_layers):
            x = jnp.where(layer == i, srcs[i][...], x)
        for h in range(H_A):
            o_ref[:, h, :] = x[:, h * DH_A:(h + 1) * DH_A]


def _kv_outputs(lay, projs):
    n_layers = len(projs)
    shape = jax.ShapeDtypeStruct((lay.batch, n_layers, lay.seq, H_A, DH_A), F32)
    src = lambda col: [pl.BlockSpec((lay.seq, A_W), lambda b, i: (b, col))] * n_layers
    out = pl.BlockSpec((None, None, lay.seq, H_A, DH_A), lambda b, i: (b, i, 0, 0, 0))
    return pl.pallas_call(
        functools.partial(_kv_kernel, n_layers=n_layers),
        out_shape=(shape, shape),
        grid=(lay.batch, n_layers),
        in_specs=src(1) + src(2),
        out_specs=(out, out),
        compiler_params=_params(("arbitrary", "arbitrary")),
        name="kv_outputs",
    )(*projs, *projs)


def kernel(x_prompt, x_sample, cache_k_ctx, cache_v_ctx, state_delta, c, c_ctx, w_mod, b_mod, norm_g, final_g,
           w_in, conv_w, a_log, dt_bias, rpb, o_gain, w_out, ffn_gate, ffn_up, ffn_down, pool_w, pool_scale,
           w_router, moe_gate, moe_up, moe_down):
    batch, seq, d = x_prompt.shape
    dec_batch, dec_seq, _ = x_sample.shape
    depth = w_mod.shape[0]
    n_even = w_in.shape[0]
    past = cache_k_ctx.shape[2]
    assert d == D_MODEL and depth % 2 == 0
    lay = _Layout(batch, seq, dec_batch, dec_seq)

    cond = jnp.zeros((lay.mod_rows, d), F32).at[0].set(c_ctx).at[1:1 + dec_batch].set(c)
    mod = _adaln(cond, w_mod, b_mod)
    modtab = mod.reshape(depth, lay.mod_rows, 6, d).transpose(0, 2, 1, 3).reshape(depth * 6 * lay.mod_rows, 1, d)
    normg = norm_g.reshape(depth * 2, 1, d)

    x_parts = (x_prompt.reshape(lay.nc, d), x_sample.reshape(lay.nl, d))
    cache_k = cache_k_ctx.reshape(dec_batch, n_even, past, A_W)
    cache_v = cache_v_ctx.reshape(dec_batch, n_even, past, A_W)
    rows = dec_seq // GRID_W
    kh = min(NA_KH, rows)
    n_main = 3 * A_W + 4 * B_W
    pad128 = lambda a: jnp.pad(a.reshape(1, -1), ((0, 0), (0, LANES - a.size)))
    w_router_p = jnp.pad(w_router, ((0, 0), (0, 0), (0, LANES - N_EXP)))
    w_in16, w_out16 = _to_bf16(w_in), _to_bf16(w_out)
    ffn_gate, ffn_up, ffn_down = _to_bf16(ffn_gate), _to_bf16(ffn_up), _to_bf16(ffn_down)

    projs, s_list = [], []
    for l in range(depth):
        i = l // 2
        if l % 2 == 0:
            w_ab = jnp.pad(w_in[i][:, n_main:], ((0, 0), (0, LANES - 4 * H_B)))
            proj, ab = _even_project(lay, x_parts, normg, modtab, w_in16, w_ab, l)
            bias = _na_bias_table(rpb[i], rows, kh)
            oa_ctx, oa_lat = _attention(lay, proj, cache_k, cache_v, bias, i)
            qd, kd, vd, gates, gates_t = _delta_prep(lay, proj, conv_w, ab, pad128(a_log[i]), pad128(dt_bias[i]), i)
            o_f, o_b, s_fin = _delta_net(lay, qd, kd, vd, gates, gates_t, state_delta[:, i])
            x, h2 = _even_output(lay, oa_ctx, oa_lat, o_f, o_b, proj, o_gain.reshape(n_even, 1, DV), w_out16, x_parts,
                                 normg, modtab, l)
            x = _dense_ffn(lay, h2, x, ffn_gate, ffn_up, ffn_down, modtab, l)
            projs.append(proj)
            s_list.append(s_fin)
        else:
            x, h2, ridx, rw, counts = _pool_and_route(lay, x, normg, modtab, pool_w,
                                                      pool_scale.reshape(-1, 1, d), w_router_p, l)
            dest, pad, g_expert, g_row, g_nsub = _moe_plan(ridx, counts, lay.n)
            xs = _moe_dispatch(h2, dest, pad, _sorted_rows(lay.n))
            ys = _moe_experts(xs, g_expert, g_row, g_nsub, moe_gate, moe_up, moe_down, i)
            x = _moe_combine(lay, x, ys, dest, rw, modtab, final_g.reshape(1, d), l, final=(l == depth - 1))
            x_parts = (x,)
    y_ctx, y_lat = x
    new_k, new_v = _kv_outputs(lay, projs)
    return (y_ctx.reshape(batch, seq, d), y_lat.reshape(dec_batch, dec_seq, d), new_k, new_v,
            jnp.stack(s_list, axis=1))
```

```python
import functools
import math

import numpy as np
import jax
import jax.numpy as jnp
from jax import lax
from jax.experimental import pallas as pl
from jax.experimental.pallas import tpu as pltpu

F32 = jnp.float32
BF16 = jnp.bfloat16

D_MODEL = 1024
GRID_W = 64
DH_A = 64
H_A = 8
A_W = H_A * DH_A
NA_KH = 8
NA_KW = 16
DK = 128
DV = 128
H_B = 4
B_W = H_B * DK
CONV_K = 3
CHUNK = 64
POOL_WINDOWS = (2, 4, 8, 16)
POOL_G = D_MODEL // len(POOL_WINDOWS)
D_FF = 7 * D_MODEL // 2
N_EXP = 8
EPS = 1e-6
NEG_INF = -1e30

LANES = 128
SEG = 256
POOL_HALO = 64
FF_TILE = 512
PROJ_TILE = 1792
MOE_SUB = 256
MOE_GROUP = 8
VMEM_LIMIT = 56 * 2 ** 20


def _params(sem, vmem=VMEM_LIMIT):
    return pltpu.CompilerParams(dimension_semantics=sem, vmem_limit_bytes=vmem)


def _silu(x):
    return x * jax.nn.sigmoid(x)


def _dot(a, b):
    return jnp.dot(a, b, preferred_element_type=F32)


def _dot_nt(a, b):
    return lax.dot_general(a, b, (((1,), (1,)), ((), ())), preferred_element_type=F32)


def _dot_f32x3(a01, x):
    x1 = x.astype(BF16)
    r1 = x - x1.astype(F32)
    x2 = r1.astype(BF16)
    x3 = (r1 - x2.astype(F32)).astype(BF16)
    return _dot(a01, x1) + _dot(a01, x2) + _dot(a01, x3)


def _dot_f32ish(a, b):
    a_hi, b_hi = a.astype(BF16), b.astype(BF16)
    a_lo, b_lo = (a - a_hi.astype(F32)).astype(BF16), (b - b_hi.astype(F32)).astype(BF16)
    return _dot(a_hi, b_hi) + (_dot(a_hi, b_lo) + _dot(a_lo, b_hi))


def _mask01(mask):
    return jnp.where(mask, 1.0, 0.0).astype(BF16)


def _modulate(x, g, shift, scale):
    y = x * lax.rsqrt(jnp.mean(x * x, axis=-1, keepdims=True) + EPS)
    return (y * g) * (1.0 + scale) + shift


def _cast_kernel(w_ref, o_ref):
    o_ref[...] = w_ref[...].astype(BF16)


def _to_bf16(w):
    layers, rows, cols = w.shape
    tr = 256
    assert rows % tr == 0
    spec = pl.BlockSpec((None, tr, cols), lambda l, r: (l, r, 0))
    return pl.pallas_call(
        _cast_kernel,
        out_shape=jax.ShapeDtypeStruct(w.shape, BF16),
        grid=(layers, rows // tr),
        in_specs=[spec],
        out_specs=spec,
        compiler_params=_params(("arbitrary", "arbitrary")),
        name="to_bf16",
    )(w)


def _adaln_kernel(cond_ref, w_ref, b_ref, o_ref):
    s = _silu(cond_ref[...]).astype(BF16)
    o_ref[0] = _dot(s, w_ref[0].astype(BF16)) + b_ref[0]


def _adaln(cond, w_mod, b_mod):
    depth, d, six_d = w_mod.shape
    r = cond.shape[0]
    tn = six_d // 4
    return pl.pallas_call(
        _adaln_kernel,
        out_shape=jax.ShapeDtypeStruct((depth, r, six_d), F32),
        grid=(depth, six_d // tn),
        in_specs=[
            pl.BlockSpec((r, d), lambda l, j: (0, 0)),
            pl.BlockSpec((1, d, tn), lambda l, j: (l, 0, j)),
            pl.BlockSpec((1, 1, tn), lambda l, j: (l, 0, j)),
        ],
        out_specs=pl.BlockSpec((1, r, tn), lambda l, j: (l, 0, j)),
        compiler_params=_params(("arbitrary", "arbitrary")),
        name="adaln",
    )(cond, w_mod, b_mod.reshape(depth, 1, six_d))


class _Layout:
    def __init__(self, batch, seq, dec_batch, dec_seq):
        self.batch, self.seq, self.dec_batch, self.dec_seq = batch, seq, dec_batch, dec_seq
        self.nc = batch * seq
        self.nl = dec_batch * dec_seq
        self.n = self.nc + self.nl
        self.mod_rows = -(-(1 + dec_batch) // 8) * 8
        assert seq % SEG == 0 and dec_seq % SEG == 0 and dec_seq % GRID_W == 0
        self.tm = math.gcd(self.nc, 1024)
        assert self.tm % SEG == 0 and dec_seq % self.tm == 0 and self.nc % dec_seq == 0

    def mod_row(self, first_row):
        return jnp.maximum((first_row - self.nc) // self.dec_seq + 1, 0)

    def stream_specs(self, parts, tm):
        ct = self.nc // tm
        off = ct if len(parts) == 1 else 0
        specs = (pl.BlockSpec((tm, D_MODEL), lambda i, *_: (jnp.minimum(i, ct - 1), 0)),
                 pl.BlockSpec((tm, D_MODEL), lambda i, *_: (off + jnp.maximum(i - ct, 0), 0)))
        return specs, (parts[0], parts[-1])

    def mod_spec(self, l, k, rows_per_step):
        base = (l * 6 + k) * self.mod_rows
        return pl.BlockSpec((1, 1, D_MODEL),
                            lambda i, *_: (base + self.mod_row(i * rows_per_step), 0, 0))


def _stream_tile(xa_ref, xb_ref, ctx_tiles):
    return jnp.where(pl.program_id(0) < ctx_tiles, xa_ref[...], xb_ref[...])


def _row_spec(l, k):
    return pl.BlockSpec((1, 1, D_MODEL), lambda i, *_: (l * 2 + k, 0, 0))


def _proj_kernel(xa_ref, xb_ref, g_ref, sh_ref, sc_ref, w_ref, wab_ref, proj_ref, ab_ref, h_scr, *, ctx_tiles):
    @pl.when(pl.program_id(1) == 0)
    def _():
        h = _modulate(_stream_tile(xa_ref, xb_ref, ctx_tiles), g_ref[0], sh_ref[0], sc_ref[0]).astype(BF16)
        h_scr[...] = h
        ab_ref[...] = _dot(h, wab_ref[...].astype(BF16))

    proj_ref[...] = _dot(h_scr[...], w_ref[...])


def _even_project(lay, x_parts, normg, modtab, w_in, w_ab, l):
    i_even = l // 2
    tm, tn = lay.tm, PROJ_TILE
    ncols = 3 * A_W + 4 * B_W
    x_specs, x_args = lay.stream_specs(x_parts, tm)
    return pl.pallas_call(
        functools.partial(_proj_kernel, ctx_tiles=lay.nc // tm),
        out_shape=(jax.ShapeDtypeStruct((lay.n, ncols), F32),
                   jax.ShapeDtypeStruct((lay.n, LANES), F32)),
        grid=(lay.n // tm, ncols // tn),
        in_specs=[
            *x_specs,
            _row_spec(l, 0),
            lay.mod_spec(l, 0, tm),
            lay.mod_spec(l, 1, tm),
            pl.BlockSpec((None, D_MODEL, tn), lambda i, j: (i_even, 0, j)),
            pl.BlockSpec((D_MODEL, LANES), lambda i, j: (0, 0)),
        ],
        out_specs=(pl.BlockSpec((tm, tn), lambda i, j: (i, j)),
                   pl.BlockSpec((tm, LANES), lambda i, j: (i, 0))),
        scratch_shapes=[pltpu.VMEM((tm, D_MODEL), BF16)],
        compiler_params=_params(("arbitrary", "arbitrary")),
        name="even_project",
    )(*x_args, normg, modtab, modtab, w_in, w_ab)


def _attend_heads(q, keys, values, bias_of):
    scale = DH_A ** -0.5
    per = LANES // DH_A
    lane = lax.broadcasted_iota(jnp.int32, (1, LANES), 1)
    own = [(lane >= s * DH_A) & (lane < (s + 1) * DH_A) for s in range(per)]
    groups = range(A_W // LANES)
    blocks = range(len(keys))
    gsl = lambda g: slice(g * LANES, (g + 1) * LANES)
    k16 = [[k[:, gsl(g)].astype(BF16) for k in keys] for g in groups]
    v16 = [[v[:, gsl(g)].astype(BF16) for v in values] for g in groups]
    heads = [(g, s) for g in groups for s in range(per)]
    qh = [jnp.where(own[s], q[:, gsl(g)], 0.0).astype(BF16) for g, s in heads]
    scores = []
    for h, (g, s) in enumerate(heads):
        row = []
        for i in blocks:
            sc = _dot_nt(qh[h], k16[g][i]) * scale
            b = bias_of(h, i)
            row.append(sc if b is None else sc + b)
        scores.append(row)
    peak = [functools.reduce(jnp.maximum, [jnp.max(sc, axis=-1, keepdims=True) for sc in row]) for row in scores]
    probs = [[jnp.exp(sc - m) for sc in row] for row, m in zip(scores, peak)]
    denom = [functools.reduce(jnp.add, [jnp.sum(p, axis=-1, keepdims=True) for p in row]) for row in probs]
    acc = [functools.reduce(jnp.add, [_dot(p.astype(BF16), v16[g][i]) for i, p in enumerate(row)])
           for row, (g, s) in zip(probs, heads)]
    outs = []
    for g in groups:
        o = jnp.zeros_like(acc[0])
        for s in range(per):
            h = g * per + s
            o = jnp.where(own[s], acc[h] / denom[h], o)
        outs.append(o)
    return jnp.concatenate(outs, axis=1)


def _ctx_attn_kernel(q_ref, k_ref, v_ref, o_ref):
    o_ref[...] = _attend_heads(q_ref[...], [k_ref[...]], [v_ref[...]], lambda h, i: None).astype(o_ref.dtype)


def _na_attn_kernel(q_ref, k_ref, v_ref, kc_ref, vc_ref, bias_ref, o_ref, k16, v16, kc16, vc16, *, rows, kh):
    r = pl.program_id(1)

    @pl.when(r == 0)
    def _():
        k16[...] = k_ref[...].astype(BF16)
        v16[...] = v_ref[...].astype(BF16)
        kc16[...] = kc_ref[...].astype(BF16)
        vc16[...] = vc_ref[...].astype(BF16)

    row_lo = jnp.clip(r - kh // 2, 0, rows - kh)
    start = pl.multiple_of(row_lo * GRID_W, GRID_W)
    kl = k16[pl.ds(start, kh * GRID_W), :]
    vl = v16[pl.ds(start, kh * GRID_W), :]
    o_ref[...] = _attend_heads(q_ref[...], [kl, kc16[...]], [vl, vc16[...]],
                               lambda h, i: bias_ref[h, 0] if i == 0 else None).astype(o_ref.dtype)


def _na_bias_table(rpb, rows, kh):
    r = np.arange(rows)
    row_idx = np.clip(r - kh // 2, 0, rows - kh)[:, None] + np.arange(kh)[None, :]
    dr = row_idx - r[:, None] + (NA_KH - 1)
    qcol = np.arange(GRID_W)
    kcol = np.arange(GRID_W)
    col_lo = np.clip(qcol - NA_KW // 2, 0, GRID_W - NA_KW)
    valid = (kcol[None, :] >= col_lo[:, None]) & (kcol[None, :] < col_lo[:, None] + NA_KW)
    dc = np.clip(kcol[None, :] - qcol[:, None], 1 - NA_KW, NA_KW - 1) + (NA_KW - 1)
    onehot = (dc[None, :, :] == np.arange(2 * NA_KW - 1)[:, None, None]).astype(np.float32)
    picked = jnp.einsum("hrjc,cqk->hrqjk", rpb[:, dr].astype(F32), onehot, precision=lax.Precision.HIGHEST)
    bias = jnp.where(valid[None, None, :, None, :], picked, NEG_INF)
    return bias.reshape(H_A, rows, GRID_W, kh * GRID_W)


def _attention(lay, proj, cache_k, cache_v, bias, i_even):
    ctx = pl.pallas_call(
        _ctx_attn_kernel,
        out_shape=jax.ShapeDtypeStruct((lay.nc, A_W), BF16),
        grid=(lay.batch,),
        in_specs=[pl.BlockSpec((lay.seq, A_W), lambda b: (b, 0)),
                  pl.BlockSpec((lay.seq, A_W), lambda b: (b, 1)),
                  pl.BlockSpec((lay.seq, A_W), lambda b: (b, 2))],
        out_specs=pl.BlockSpec((lay.seq, A_W), lambda b: (b, 0)),
        compiler_params=_params(("arbitrary",)),
        name="ctx_attention",
    )(proj, proj, proj)

    rows = lay.dec_seq // GRID_W
    kh = min(NA_KH, rows)
    past = cache_k.shape[2]
    q0 = lay.nc // GRID_W
    b0 = lay.nc // lay.dec_seq
    lat = pl.pallas_call(
        functools.partial(_na_attn_kernel, rows=rows, kh=kh),
        out_shape=jax.ShapeDtypeStruct((lay.nl, A_W), BF16),
        grid=(lay.dec_batch, rows),
        in_specs=[pl.BlockSpec((GRID_W, A_W), lambda b, r: (q0 + b * rows + r, 0)),
                  pl.BlockSpec((lay.dec_seq, A_W), lambda b, r: (b0 + b, 1)),
                  pl.BlockSpec((lay.dec_seq, A_W), lambda b, r: (b0 + b, 2)),
                  pl.BlockSpec((None, None, past, A_W), lambda b, r: (b, i_even, 0, 0)),
                  pl.BlockSpec((None, None, past, A_W), lambda b, r: (b, i_even, 0, 0)),
                  pl.BlockSpec((H_A, 1, GRID_W, kh * GRID_W), lambda b, r: (0, r, 0, 0))],
        out_specs=pl.BlockSpec((GRID_W, A_W), lambda b, r: (b * rows + r, 0)),
        scratch_shapes=[pltpu.VMEM((lay.dec_seq, A_W), BF16), pltpu.VMEM((lay.dec_seq, A_W), BF16),
                        pltpu.VMEM((past, A_W), BF16), pltpu.VMEM((past, A_W), BF16)],
        compiler_params=_params(("arbitrary", "arbitrary")),
        name="na_attention",
    )(proj, proj, proj, cache_k, cache_v, bias)
    return ctx, lat


def _seq_position(first, nc, seq, dec_seq):
    is_lat = first >= nc
    return jnp.where(is_lat, (first - nc) % dec_seq, first % seq), jnp.where(is_lat, dec_seq, seq)


GATE_GC, GATE_BETA, GATE_GT = 0, 2 * H_B, 4 * H_B


def _gate_table(ab, alog, dtb):
    n = SEG
    lane = lax.broadcasted_iota(jnp.int32, (1, LANES), 1)
    ri = lax.broadcasted_iota(jnp.int32, (n, n), 0)
    ci = lax.broadcasted_iota(jnp.int32, (n, n), 1)
    shift = int(math.log2(CHUNK))
    same = jnp.right_shift(ri, shift) == jnp.right_shift(ci, shift)
    g_all = -jnp.exp(alog) * jax.nn.softplus(ab + dtb)
    prefix = _dot_f32x3(_mask01(same & (ci <= ri)), g_all)
    total = _dot_f32x3(_mask01(same), g_all)
    gc = jnp.where(lane < H_B, prefix, total - prefix + g_all)
    return jnp.where(lane < GATE_BETA, gc,
                     jnp.where(lane < GATE_GT, jax.nn.sigmoid(ab), pltpu.roll(total, GATE_GT, 1)))


def _delta_prep_kernel(xp_ref, xc_ref, xn_ref, w_ref, ab_ref, alog_ref, dtb_ref, q_ref, k_ref, v_ref, g_ref, gt_ref,
                       *, nc, seq, dec_seq):
    table = _gate_table(ab_ref[...], alog_ref[...], dtb_ref[...])
    g_ref[...] = table
    gt_ref[...] = table.T
    off, t_len = _seq_position(pl.program_id(0) * SEG, nc, seq, dec_seq)
    x = xc_ref[...]
    row = lax.broadcasted_iota(jnp.int32, (SEG, 1), 0)
    before = jnp.where(off > 0, xp_ref[7:8, :], 0.0)
    after = jnp.where(off + SEG < t_len, xn_ref[0:1, :], 0.0)
    prev = jnp.where(row == 0, before, pltpu.roll(x, 1, 0))
    nxt = jnp.where(row == SEG - 1, after, pltpu.roll(x, SEG - 1, 0))
    w = w_ref[...]
    y = _silu(prev * w[0:1] + x * w[1:2] + nxt * w[2:3])

    def l2norm(a):
        return a * lax.rsqrt(jnp.sum(a * a, axis=-1, keepdims=True) + EPS)

    heads = lambda part: [y[:, (part * H_B + h) * DK:(part * H_B + h + 1) * DK] for h in range(H_B)]
    q_ref[...] = jnp.concatenate([l2norm(a) * (DK ** -0.5) for a in heads(0)], axis=1)
    k_ref[...] = jnp.concatenate([l2norm(a) for a in heads(1)], axis=1)
    v_ref[...] = y[:, 2 * B_W:]


def _delta_prep(lay, proj, conv_w, ab, alog, dtb, i_even):
    width = 3 * B_W
    cblk = 3 * A_W // width
    assert cblk * width == 3 * A_W
    per = SEG // 8
    n8 = lay.n // 8
    ospec = pl.BlockSpec((SEG, B_W), lambda i: (i, 0))
    shape = jax.ShapeDtypeStruct((lay.n, B_W), F32)
    row = pl.BlockSpec((1, LANES), lambda i: (0, 0))
    return pl.pallas_call(
        functools.partial(_delta_prep_kernel, nc=lay.nc, seq=lay.seq, dec_seq=lay.dec_seq),
        out_shape=(shape, shape, shape,
                   jax.ShapeDtypeStruct((lay.n, LANES), F32), jax.ShapeDtypeStruct((LANES, lay.n), F32)),
        grid=(lay.n // SEG,),
        in_specs=[pl.BlockSpec((8, width), lambda i: (jnp.maximum(i * per - 1, 0), cblk)),
                  pl.BlockSpec((SEG, width), lambda i: (i, cblk)),
                  pl.BlockSpec((8, width), lambda i: (jnp.minimum((i + 1) * per, n8 - 1), cblk)),
                  pl.BlockSpec((None, CONV_K, width), lambda i: (i_even, 0, 0)),
                  pl.BlockSpec((SEG, LANES), lambda i: (i, 0)), row, row],
        out_specs=(ospec, ospec, ospec,
                   pl.BlockSpec((SEG, LANES), lambda i: (i, 0)), pl.BlockSpec((LANES, SEG), lambda i: (0, i))),
        compiler_params=_params(("arbitrary",)),
        name="delta_prep",
    )(proj, proj, proj, conv_w, ab, alog, dtb)


def _delta_masks(d):
    n = SEG
    ri = lax.broadcasted_iota(jnp.int32, (n, n), 0)
    ci = lax.broadcasted_iota(jnp.int32, (n, n), 1)
    shift = int(math.log2(CHUNK))
    same = jnp.right_shift(ri, shift) == jnp.right_shift(ci, shift)
    incl = same & ((ci <= ri) if d == 0 else (ci >= ri))
    strict = same & ((ci < ri) if d == 0 else (ci > ri))
    levels = []
    for level in range(shift):
        bi, bj = jnp.right_shift(ri, level), jnp.right_shift(ci, level)
        siblings = jnp.right_shift(bi, 1) == jnp.right_shift(bj, 1)
        levels.append(siblings & ((bi > bj) if d == 0 else (bi < bj)))
    return incl, strict, jnp.where(ri == ci, 1.0, 0.0), levels


def _delta_chains(chains):
    n = SEG
    n_chunks = n // CHUNK
    each = lambda fn, *lists: [fn(*args) for args in zip(*lists)]
    ds, masks, qs, ks, vs, betas, gcs, gts, gc_rows, gt_rows, states = (list(t) for t in zip(*chains))
    incls, stricts = [m[0] for m in masks], [m[1] for m in masks]
    n_levels = len(masks[0][3])

    decays = each(lambda m, gc, gr: jnp.where(m, jnp.exp(jnp.where(m, gc - gr, 0.0)), 0.0), incls, gcs, gc_rows)
    kbs = each(lambda k, b: k * b, ks, betas)
    k16s = [k.astype(BF16) for k in ks]
    grams = each(lambda kb, k16: _dot_nt(kb.astype(BF16), k16), kbs, k16s)
    lmats = each(lambda m, g, dec: jnp.where(m, g * dec, 0.0), stricts, grams, decays)

    xs = each(lambda m, lm: m[2] - jnp.where(m[3][0], lm, 0.0), masks, lmats)
    for level in range(1, n_levels):
        x16s = [x.astype(BF16) for x in xs]
        cs = each(lambda m, lm: jnp.where(m[3][level], lm, 0.0).astype(BF16), masks, lmats)
        xcs = each(lambda x16, c: _dot(x16, c).astype(BF16), x16s, cs)
        xs = each(lambda x, xc, x16: x - _dot(xc, x16), xs, xcs, x16s)

    egs = [jnp.exp(gc) for gc in gcs]
    rhss = each(lambda v, b, kb, eg: jnp.concatenate([v * b, kb * eg], axis=1).astype(BF16), vs, betas, kbs, egs)
    uws = each(lambda x, rhs: _dot(x.astype(BF16), rhs), xs, rhss)
    qks = each(lambda q, k16: _dot_nt(q.astype(BF16), k16), qs, k16s)
    attns = each(lambda m, qk, dec: jnp.where(m, qk * dec, 0.0).astype(BF16), incls, qks, decays)
    qgs = each(lambda q, eg: (q * eg).astype(BF16), qs, egs)
    kd_ts = each(lambda k, gt_r, gc_r: (k.T * jnp.exp(gt_r - gc_r)).astype(BF16), ks, gt_rows, gc_rows)
    gls = [jnp.exp(gt) for gt in gts]

    v_new = [[None] * n_chunks for _ in chains]
    o_state = [[None] * n_chunks for _ in chains]
    for step in range(n_chunks):
        for i, d in enumerate(ds):
            c = step if d == 0 else n_chunks - 1 - step
            rs = slice(c * CHUNK, (c + 1) * CHUNK)
            s16 = states[i].astype(BF16)
            vn = uws[i][rs, :DV] - _dot(uws[i][rs, DV:].astype(BF16), s16)
            o_state[i][c] = _dot(qgs[i][rs], s16)
            v_new[i][c] = vn
            pieces = [jnp.zeros((c * CHUNK, DV), BF16), vn.astype(BF16),
                      jnp.zeros((n - (c + 1) * CHUNK, DV), BF16)]
            padded = jnp.concatenate([p for p in pieces if p.shape[0]], axis=0)
            states[i] = states[i] * gls[i][c * CHUNK:c * CHUNK + 1, :] + _dot(kd_ts[i], padded)
    outs = each(lambda os, a, vn: jnp.concatenate(os, axis=0) + _dot(a, jnp.concatenate(vn, axis=0).astype(BF16)),
                o_state, attns, v_new)
    return list(zip(outs, states))


def _delta_kernel(plan_ref, *refs):
    dir_refs = (refs[0:5], refs[5:10])
    s0_ref, of_ref, ob_ref, sfin_ref, s_scr = refs[10:]
    g = pl.program_id(0)
    first, last, has_s0 = plan_ref[2, g] == 1, plan_ref[3, g] == 1, plan_ref[4, g] == 1

    @pl.when(first)
    def _():
        s_scr[...] = jnp.where(has_s0, s0_ref[...], 0.0)

    chains = []
    for d, (q_ref, k_ref, v_ref, g_ref, gt_ref) in enumerate(dir_refs):
        masks = _delta_masks(d)
        q, k, v, gates, gates_t = q_ref[...], k_ref[...], v_ref[...], g_ref[...], gt_ref[...]
        for h in range(H_B):
            sl = slice(h * DK, (h + 1) * DK)
            col = d * H_B + h
            pick = lambda base: gates[:, base + col:base + col + 1]
            pick_t = lambda base: gates_t[base + col:base + col + 1, :]
            chains.append((d, masks, q[:, sl], k[:, sl], v[:, sl], pick(GATE_BETA), pick(GATE_GC),
                           pick(GATE_GT), pick_t(GATE_GC), pick_t(GATE_GT), s_scr[d, h]))
    results = _delta_chains(chains)
    for d, o_ref in enumerate((of_ref, ob_ref)):
        o_ref[...] = jnp.concatenate([results[d * H_B + h][0] for h in range(H_B)], axis=1)
        for h in range(H_B):
            s_scr[d, h] = results[d * H_B + h][1]

    @pl.when(last & jnp.logical_not(has_s0))
    def _():
        sfin_ref[...] = s_scr[...]


def _delta_plan(lay):
    rows = []
    for kind, n_seq, t in ((0, lay.batch, lay.seq), (1, lay.dec_batch, lay.dec_seq)):
        nseg = t // SEG
        base = 0 if kind == 0 else lay.nc // SEG
        for b in range(n_seq):
            for s in range(nseg):
                rows.append((base + b * nseg + s, base + b * nseg + nseg - 1 - s, int(s == 0), int(s == nseg - 1),
                             kind, b if kind else 0, b if kind == 0 else lay.batch - 1))
    return jnp.asarray(np.array(rows, np.int32).T)


def _delta_net(lay, q, k, v, gates, gates_t, s0):
    plan = _delta_plan(lay)
    in_specs, args = [], []
    for d in (0, 1):
        in_specs += [pl.BlockSpec((SEG, B_W), lambda g, plan, d=d: (plan[d, g], 0))] * 3
        in_specs += [pl.BlockSpec((SEG, LANES), lambda g, plan, d=d: (plan[d, g], 0)),
                     pl.BlockSpec((LANES, SEG), lambda g, plan, d=d: (0, plan[d, g]))]
        args += [q, k, v, gates, gates_t]
    state = lambda row: pl.BlockSpec((None, 2, H_B, DK, DV), lambda g, plan: (plan[row, g], 0, 0, 0, 0))
    oshape = jax.ShapeDtypeStruct((lay.n, B_W), F32)
    grid_spec = pltpu.PrefetchScalarGridSpec(
        num_scalar_prefetch=1,
        grid=(lay.n // SEG,),
        in_specs=in_specs + [state(5)],
        out_specs=(pl.BlockSpec((SEG, B_W), lambda g, plan: (plan[0, g], 0)),
                   pl.BlockSpec((SEG, B_W), lambda g, plan: (plan[1, g], 0)), state(6)),
        scratch_shapes=[pltpu.VMEM((2, H_B, DK, DV), F32)],
    )
    return pl.pallas_call(
        _delta_kernel,
        out_shape=(oshape, oshape, jax.ShapeDtypeStruct((lay.batch, 2, H_B, DK, DV), F32)),
        grid_spec=grid_spec,
        compiler_params=_params(("arbitrary",)),
        name="delta_net",
    )(plan, *args, s0)


def _even_out_kernel(oac_ref, oal_ref, of_ref, ob_ref, z_ref, gain_ref, w_ref, xa_ref, xb_ref, gt_ref,
                     g2_ref, sh2_ref, sc2_ref, xo_ref, h2_ref, *, ctx_tiles):
    ob = of_ref[...] + ob_ref[...]
    z = z_ref[...]
    parts = [_stream_tile(oac_ref, oal_ref, ctx_tiles)]
    for h in range(H_B):
        sl = slice(h * DV, (h + 1) * DV)
        o_h = ob[:, sl]
        y = o_h * lax.rsqrt(jnp.mean(o_h * o_h, axis=-1, keepdims=True) + EPS) * gain_ref[...] * _silu(z[:, sl])
        parts.append(y.astype(BF16))
    mix = _dot(jnp.concatenate(parts, axis=1), w_ref[...])
    xn = _stream_tile(xa_ref, xb_ref, ctx_tiles) + gt_ref[0] * mix
    xo_ref[...] = xn
    h2_ref[...] = _modulate(xn, g2_ref[0], sh2_ref[0], sc2_ref[0]).astype(BF16)


def _even_output(lay, oa_ctx, oa_lat, o_f, o_b, proj, o_gain, w_out, x_parts, normg, modtab, l):
    i_even = l // 2
    tm = lay.tm
    ctx_tiles = lay.nc // tm
    x_specs, x_args = lay.stream_specs(x_parts, tm)
    zcol = (3 * A_W + 3 * B_W) // B_W
    half = lambda: pl.BlockSpec((tm, B_W), lambda i: (i, 0))
    full = lambda: pl.BlockSpec((tm, D_MODEL), lambda i: (i, 0))
    return pl.pallas_call(
        functools.partial(_even_out_kernel, ctx_tiles=ctx_tiles),
        out_shape=(jax.ShapeDtypeStruct((lay.n, D_MODEL), F32),
                   jax.ShapeDtypeStruct((lay.n, D_MODEL), BF16)),
        grid=(lay.n // tm,),
        in_specs=[pl.BlockSpec((tm, A_W), lambda i: (jnp.minimum(i, ctx_tiles - 1), 0)),
                  pl.BlockSpec((tm, A_W), lambda i: (jnp.maximum(i - ctx_tiles, 0), 0)),
                  half(), half(),
                  pl.BlockSpec((tm, B_W), lambda i: (i, zcol)),
                  pl.BlockSpec((None, 1, DV), lambda i: (i_even, 0, 0)),
                  pl.BlockSpec((None, D_MODEL, D_MODEL), lambda i: (i_even, 0, 0)),
                  *x_specs,
                  lay.mod_spec(l, 2, tm),
                  _row_spec(l, 1), lay.mod_spec(l, 3, tm), lay.mod_spec(l, 4, tm)],
        out_specs=(full(), full()),
        compiler_params=_params(("arbitrary",)),
        name="even_output",
    )(oa_ctx, oa_lat, o_f, o_b, proj, o_gain, w_out, *x_args, modtab, normg, modtab, modtab)


def _ffn_kernel(h_ref, wg_ref, wu_ref, wd_ref, x_ref, gt_ref, o_ref, acc_ref):
    j = pl.program_id(1)

    @pl.when(j == 0)
    def _():
        acc_ref[...] = jnp.zeros_like(acc_ref)

    h = h_ref[...]
    a = (_silu(_dot(h, wg_ref[...])) * _dot(h, wu_ref[...])).astype(BF16)
    acc_ref[...] += _dot(a, wd_ref[...])

    @pl.when(j == pl.num_programs(1) - 1)
    def _():
        o_ref[...] = x_ref[...] + gt_ref[0] * acc_ref[...]


def _dense_ffn(lay, h2, x, ffn_gate, ffn_up, ffn_down, modtab, l):
    i_even = l // 2
    tm, tf = lay.tm, FF_TILE
    return pl.pallas_call(
        _ffn_kernel,
        out_shape=jax.ShapeDtypeStruct((lay.n, D_MODEL), F32),
        grid=(lay.n // tm, D_FF // tf),
        in_specs=[pl.BlockSpec((tm, D_MODEL), lambda i, j: (i, 0)),
                  pl.BlockSpec((None, D_MODEL, tf), lambda i, j: (i_even, 0, j)),
                  pl.BlockSpec((None, D_MODEL, tf), lambda i, j: (i_even, 0, j)),
                  pl.BlockSpec((None, tf, D_MODEL), lambda i, j: (i_even, j, 0)),
                  pl.BlockSpec((tm, D_MODEL), lambda i, j: (i, 0)),
                  lay.mod_spec(l, 5, tm)],
        out_specs=pl.BlockSpec((tm, D_MODEL), lambda i, j: (i, 0)),
        scratch_shapes=[pltpu.VMEM((tm, D_MODEL), F32)],
        compiler_params=_params(("arbitrary", "arbitrary")),
        name="dense_ffn",
    )(h2, ffn_gate, ffn_up, ffn_down, x, modtab)


def _pool_kernel(xp_ref, xc_ref, xn_ref, g1_ref, sh1_ref, sc1_ref, gt1_ref, pw_ref, ps_ref,
                 g2_ref, sh2_ref, sc2_ref, wr_ref, xo_ref, h2_ref, ridx_ref, rw_ref, cnt_ref, count_scr,
                 *, nc, seq, dec_seq):
    i = pl.program_id(0)
    off, t_len = _seq_position(i * SEG, nc, seq, dec_seq)

    g1, sh1, sc1 = g1_ref[0], sh1_ref[0], sc1_ref[0]
    x = xc_ref[...]
    h_cur = _modulate(x, g1, sh1, sc1)
    h_cat = jnp.concatenate([_modulate(xp_ref[...], g1, sh1, sc1), h_cur,
                             _modulate(xn_ref[...], g1, sh1, sc1)], axis=0)

    t = off + lax.broadcasted_iota(jnp.int32, (SEG, 1), 0)
    p = off - POOL_HALO + lax.broadcasted_iota(jnp.int32, (1, SEG + 2 * POOL_HALO), 1)
    mixes = []
    for gi, w in enumerate(POOL_WINDOWS):
        sl = slice(gi * POOL_G, (gi + 1) * POOL_G)
        lo = jnp.maximum(t - w // 2, 0)
        hi = jnp.minimum(t + (w - w // 2), t_len)
        band = _mask01((p >= lo) & (p < hi))
        hg = h_cat[:, sl]
        hg_hi = hg.astype(BF16)
        hg_lo = (hg - hg_hi.astype(F32)).astype(BF16)
        window_sum = _dot(band, hg_hi) + _dot(band, hg_lo)
        y = (window_sum / (hi - lo).astype(F32) - h_cur[:, sl]).astype(BF16)
        mixes.append(_dot(y, pw_ref[gi].astype(BF16)))
    mix = jnp.concatenate(mixes, axis=1) * ps_ref[...]
    xn = x + gt1_ref[0] * mix
    xo_ref[...] = xn
    h2 = _modulate(xn, g2_ref[0], sh2_ref[0], sc2_ref[0])
    h2_ref[...] = h2

    logits = _dot_f32ish(h2, wr_ref[...])
    lane = lax.broadcasted_iota(jnp.int32, logits.shape, 1)
    lane_f = lane.astype(F32)
    lg = jnp.where(lane < N_EXP, logits, -jnp.inf)
    m1 = jnp.max(lg, axis=-1, keepdims=True)
    i1 = jnp.min(jnp.where(lg == m1, lane_f, float(LANES)), axis=-1, keepdims=True)
    lg2 = jnp.where(lane_f == i1, -jnp.inf, lg)
    m2 = jnp.max(lg2, axis=-1, keepdims=True)
    i2 = jnp.min(jnp.where(lg2 == m2, lane_f, float(LANES)), axis=-1, keepdims=True)
    e = jnp.exp(m2 - m1)
    w1 = 1.0 / (1.0 + e)
    w2 = e / (1.0 + e)
    rw_ref[...] = jnp.where(lane == 0, w1, jnp.where(lane == 1, w2, 0.0))

    @pl.when(i == 0)
    def _():
        count_scr[...] = jnp.zeros_like(count_scr)

    hit1, hit2 = lane_f == i1, lane_f == i2
    picks = jnp.where(hit1 | hit2, 1.0, 0.0)
    ri = lax.broadcasted_iota(jnp.int32, (SEG, SEG), 0)
    ci = lax.broadcasted_iota(jnp.int32, (SEG, SEG), 1)
    before = _dot(_mask01(ci < ri), picks.astype(BF16)) + count_scr[...]
    r1 = jnp.sum(jnp.where(hit1, before, 0.0), axis=-1, keepdims=True)
    r2 = jnp.sum(jnp.where(hit2, before, 0.0), axis=-1, keepdims=True)
    count_scr[...] += jnp.sum(picks, axis=0, keepdims=True)
    route = jnp.where(lane == 0, i1, jnp.where(lane == 1, i2, jnp.where(lane == 2, r1, jnp.where(lane == 3, r2, 0.0))))
    ridx_ref[...] = route.astype(jnp.int32)
    cnt_ref[...] = jnp.broadcast_to(count_scr[...], cnt_ref.shape)


def _pool_and_route(lay, x, normg, modtab, pool_w, pool_scale, w_router, l):
    i_odd = l // 2
    nblk = lay.n // SEG
    blk = lambda imap: pl.BlockSpec((SEG, D_MODEL), imap)
    halo = lambda imap: pl.BlockSpec((POOL_HALO, D_MODEL), imap)
    per = SEG // POOL_HALO
    lane_blk = pl.BlockSpec((SEG, LANES), lambda i: (i, 0))
    return pl.pallas_call(
        functools.partial(_pool_kernel, nc=lay.nc, seq=lay.seq, dec_seq=lay.dec_seq),
        out_shape=(jax.ShapeDtypeStruct((lay.n, D_MODEL), F32),
                   jax.ShapeDtypeStruct((lay.n, D_MODEL), F32),
                   jax.ShapeDtypeStruct((lay.n, LANES), jnp.int32),
                   jax.ShapeDtypeStruct((lay.n, LANES), F32),
                   jax.ShapeDtypeStruct((8, LANES), F32)),
        grid=(nblk,),
        in_specs=[halo(lambda i: (jnp.maximum(i * per - 1, 0), 0)),
                  blk(lambda i: (i, 0)),
                  halo(lambda i: (jnp.minimum((i + 1) * per, nblk * per - 1), 0)),
                  _row_spec(l, 0), lay.mod_spec(l, 0, SEG), lay.mod_spec(l, 1, SEG), lay.mod_spec(l, 2, SEG),
                  pl.BlockSpec((None, len(POOL_WINDOWS), POOL_G, POOL_G), lambda i: (i_odd, 0, 0, 0)),
                  pl.BlockSpec((None, 1, D_MODEL), lambda i: (i_odd, 0, 0)),
                  _row_spec(l, 1), lay.mod_spec(l, 3, SEG), lay.mod_spec(l, 4, SEG),
                  pl.BlockSpec((None, D_MODEL, LANES), lambda i: (i_odd, 0, 0))],
        out_specs=(blk(lambda i: (i, 0)), blk(lambda i: (i, 0)), lane_blk, lane_blk,
                   pl.BlockSpec((8, LANES), lambda i: (0, 0))),
        scratch_shapes=[pltpu.VMEM((1, LANES), F32)],
        compiler_params=_params(("arbitrary",)),
        name="pool_route",
    )(x, x, x, normg, modtab, modtab, modtab, pool_w, pool_scale, normg, modtab, modtab, w_router)


def _dispatch_kernel(pad_ref, dest_ref, h_ref, xs_hbm, zero_blk, sem, *, tokens):
    def row_copy(t, k):
        return pltpu.make_async_copy(h_ref.at[pl.ds(t, 1)], xs_hbm.at[pl.ds(dest_ref[0, 2 * t + k], 1)], sem)

    def wait_rows(count):
        one_row = pltpu.make_async_copy(h_ref.at[pl.ds(0, 1)], xs_hbm.at[pl.ds(0, 1)], sem)
        for _ in range(count):
            one_row.wait()

    def start(t, carry):
        row_copy(t, 0).start()
        row_copy(t, 1).start()
        return carry

    def wait(t, carry):
        wait_rows(2)
        return carry

    @pl.when(pl.program_id(0) == 0)
    def _():
        zero_blk[...] = jnp.zeros_like(zero_blk)
        for e in range(N_EXP):
            def zero_start(r, carry, e=e):
                pltpu.make_async_copy(zero_blk.at[pl.ds(0, 1)], xs_hbm.at[pl.ds(pad_ref[0, e] + r, 1)], sem).start()
                return carry

            def zero_wait(r, carry):
                wait_rows(1)
                return carry

            lax.fori_loop(0, pad_ref[1, e], zero_start, 0)
            lax.fori_loop(0, pad_ref[1, e], zero_wait, 0)

        def tail_copy(b):
            rows = pl.ds(pl.multiple_of(pad_ref[0, N_EXP] + b * MOE_SUB, MOE_SUB), MOE_SUB)
            return pltpu.make_async_copy(zero_blk, xs_hbm.at[rows], sem)

        def tail_start(b, carry):
            tail_copy(b).start()
            return carry

        def tail_wait(b, carry):
            tail_copy(b).wait()
            return carry

        lax.fori_loop(0, pad_ref[1, N_EXP], tail_start, 0)
        lax.fori_loop(0, pad_ref[1, N_EXP], tail_wait, 0)

    lax.fori_loop(0, tokens, start, 0, unroll=16)
    lax.fori_loop(0, tokens, wait, 0, unroll=16)


def _moe_dispatch(h2, dest, pad, p_max):
    n = h2.shape[0]
    tokens = math.gcd(n, 1024)
    grid_spec = pltpu.PrefetchScalarGridSpec(
        num_scalar_prefetch=1,
        grid=(n // tokens,),
        in_specs=[pl.BlockSpec((None, 1, 2 * tokens), lambda i, pad: (i, 0, 0), memory_space=pltpu.SMEM),
                  pl.BlockSpec((tokens, D_MODEL), lambda i, pad: (i, 0))],
        out_specs=pl.BlockSpec(memory_space=pl.ANY),
        scratch_shapes=[pltpu.VMEM((MOE_SUB, D_MODEL), F32), pltpu.SemaphoreType.DMA],
    )
    return pl.pallas_call(
        functools.partial(_dispatch_kernel, tokens=tokens),
        out_shape=jax.ShapeDtypeStruct((p_max, D_MODEL), F32),
        grid_spec=grid_spec,
        compiler_params=_params(("arbitrary",)),
        name="moe_dispatch",
    )(pad, dest.reshape(n // tokens, 1, 2 * tokens), h2)


def _moe_kernel(ge_ref, gs_ref, gn_ref, xs_hbm, wg_ref, wu_ref, wd_ref, ys_hbm, xbuf, x16, acc, sem_in, sem_out):
    del ge_ref
    g = pl.program_id(0)
    j = pl.program_id(1)
    last_j = pl.num_programs(1) - 1
    nsub = gn_ref[g]
    row0 = gs_ref[g]

    def in_copy(first_row, s):
        rows = pl.ds(pl.multiple_of(first_row + s * MOE_SUB, MOE_SUB), MOE_SUB)
        return pltpu.make_async_copy(xs_hbm.at[rows], xbuf.at[pl.ds(pl.multiple_of(s * MOE_SUB, MOE_SUB), MOE_SUB)],
                                     sem_in.at[s])

    def out_copy(first, n_rows):
        rows = pl.ds(pl.multiple_of(row0 + first, MOE_SUB), n_rows)
        return pltpu.make_async_copy(acc.at[pl.ds(pl.multiple_of(first, MOE_SUB), n_rows)], ys_hbm.at[rows], sem_out)

    def loop(count, fn):
        def body(s, carry):
            fn(s)
            return carry
        lax.fori_loop(0, count, body, 0)

    @pl.when((g == 0) & (j == 0))
    def _():
        loop(nsub, lambda s: in_copy(row0, s).start())

    nxt = jnp.minimum(g + 1, pl.num_programs(0) - 1)

    @pl.when((j == 1) & (g + 1 < pl.num_programs(0)))
    def _():
        loop(gn_ref[nxt], lambda s: in_copy(gs_ref[nxt], s).start())

    def tile(first_sub, n_sub):
        n_rows = n_sub * MOE_SUB
        first = pl.multiple_of(first_sub * MOE_SUB, MOE_SUB)
        rows = pl.ds(first, n_rows)

        @pl.when(j == 0)
        def _():
            for k in range(n_sub):
                in_copy(row0, first_sub + k).wait()
            x16[rows, :] = xbuf[rows, :].astype(BF16)

        xs = x16[rows, :]
        a = (_silu(_dot(xs, wg_ref[...].astype(BF16))) * _dot(xs, wu_ref[...].astype(BF16))).astype(BF16)
        contrib = _dot(a, wd_ref[...].astype(BF16))

        @pl.when(j == 0)
        def _():
            acc[rows, :] = contrib

        @pl.when(j > 0)
        def _():
            acc[rows, :] += contrib

        @pl.when(j == last_j)
        def _():
            out_copy(first, n_rows).start()

    n_quads = lax.shift_right_logical(nsub, 2)
    has_two = (nsub & 2) == 2
    has_one = (nsub & 1) == 1
    loop(n_quads, lambda p: tile(4 * p, 4))

    @pl.when(has_two)
    def _():
        tile(4 * n_quads, 2)

    @pl.when(has_one)
    def _():
        tile(nsub - 1, 1)

    @pl.when(j == last_j)
    def _():
        loop(n_quads, lambda p: out_copy(4 * p * MOE_SUB, 4 * MOE_SUB).wait())

        @pl.when(has_two)
        def _():
            out_copy(4 * n_quads * MOE_SUB, 2 * MOE_SUB).wait()

        @pl.when(has_one)
        def _():
            out_copy((nsub - 1) * MOE_SUB, MOE_SUB).wait()


def _sorted_rows(n_tok):
    return 2 * n_tok + N_EXP * MOE_SUB


def _moe_plan(ridx, counts, n_tok):
    n_pairs = 2 * n_tok
    group_rows = MOE_SUB * MOE_GROUP
    g_max = -(-n_pairs // group_rows) + N_EXP
    counts = counts[0, :N_EXP].astype(jnp.int32)
    nsub_e = (counts + MOE_SUB - 1) // MOE_SUB
    region = nsub_e * MOE_SUB
    start_e = jnp.cumsum(region) - region
    expert = ridx[:, 0:2]
    is_e = expert[:, :, None] == jnp.arange(N_EXP, dtype=jnp.int32)[None, None, :]
    dest = jnp.sum(jnp.where(is_e, start_e[None, None, :], 0), axis=-1) + ridx[:, 2:4]

    ngrp_e = (nsub_e + MOE_GROUP - 1) // MOE_GROUP
    gend = jnp.cumsum(ngrp_e)
    gstart = gend - ngrp_e
    total = gend[-1]
    gid = jnp.arange(g_max, dtype=jnp.int32)
    ge = jnp.minimum(jnp.searchsorted(gend, gid, side="right").astype(jnp.int32), N_EXP - 1)
    kk = gid - gstart[ge]
    live = gid < total
    last_e = jnp.minimum(jnp.searchsorted(gend, total - 1, side="right").astype(jnp.int32), N_EXP - 1)
    g_expert = jnp.where(live, ge, last_e)
    g_row = jnp.where(live, start_e[ge] + kk * group_rows, 0)
    g_nsub = jnp.where(live, jnp.clip(nsub_e[ge] - kk * MOE_GROUP, 0, MOE_GROUP), 0)
    used = jnp.sum(region)
    p_max = _sorted_rows(n_tok)
    pad = jnp.stack([jnp.append(start_e + counts, used),
                     jnp.append(region - counts, (p_max - used) // MOE_SUB)]).astype(jnp.int32)
    return dest, pad, g_expert.astype(jnp.int32), g_row.astype(jnp.int32), g_nsub.astype(jnp.int32)


def _moe_experts(xs, g_expert, g_row, g_nsub, moe_gate, moe_up, moe_down, i_odd):
    p_max = xs.shape[0]
    g_max = g_expert.shape[0]
    tf = FF_TILE
    n_j = D_FF // tf
    group_rows = MOE_SUB * MOE_GROUP
    tile_of = lambda g, j, gn: jnp.where(gn[g] > 0, j, n_j - 1)
    grid_spec = pltpu.PrefetchScalarGridSpec(
        num_scalar_prefetch=3,
        grid=(g_max, n_j),
        in_specs=[pl.BlockSpec(memory_space=pl.ANY),
                  pl.BlockSpec((None, None, D_MODEL, tf),
                               lambda g, j, ge, gs, gn: (i_odd, ge[g], 0, tile_of(g, j, gn))),
                  pl.BlockSpec((None, None, D_MODEL, tf),
                               lambda g, j, ge, gs, gn: (i_odd, ge[g], 0, tile_of(g, j, gn))),
                  pl.BlockSpec((None, None, tf, D_MODEL),
                               lambda g, j, ge, gs, gn: (i_odd, ge[g], tile_of(g, j, gn), 0))],
        out_specs=pl.BlockSpec(memory_space=pl.ANY),
        scratch_shapes=[pltpu.VMEM((group_rows, D_MODEL), F32),
                        pltpu.VMEM((group_rows, D_MODEL), BF16),
                        pltpu.VMEM((group_rows, D_MODEL), F32),
                        pltpu.SemaphoreType.DMA((MOE_GROUP,)), pltpu.SemaphoreType.DMA],
    )
    return pl.pallas_call(
        _moe_kernel,
        out_shape=jax.ShapeDtypeStruct((p_max, D_MODEL), F32),
        grid_spec=grid_spec,
        input_output_aliases={3: 0},
        compiler_params=_params(("arbitrary", "arbitrary")),
        name="moe_experts",
    )(g_expert, g_row, g_nsub, xs, moe_gate, moe_up, moe_down)


def _combine_kernel(dest_ref, ys_hbm, x_ref, rw_ref, gt_ref, fg_ref, *rest, tokens, ctx_tiles):
    out_refs, (ybuf, sem) = rest[:-2], rest[-2:]
    def row_copy(t, k):
        return pltpu.make_async_copy(ys_hbm.at[pl.ds(dest_ref[0, 2 * t + k], 1)], ybuf.at[k, pl.ds(t, 1)], sem)

    def start(t, carry):
        row_copy(t, 0).start()
        row_copy(t, 1).start()
        return carry

    def wait(t, carry):
        one_row = pltpu.make_async_copy(ys_hbm.at[pl.ds(0, 1)], ybuf.at[0, pl.ds(0, 1)], sem)
        one_row.wait()
        one_row.wait()
        return carry

    lax.fori_loop(0, tokens, start, 0, unroll=16)
    lax.fori_loop(0, tokens, wait, 0, unroll=16)
    rw = rw_ref[...]
    y = rw[:, 0:1] * ybuf[0] + rw[:, 1:2] * ybuf[1]
    xn = x_ref[...] + gt_ref[0] * y
    if ctx_tiles is None:
        out_refs[0][...] = xn
        return
    xn = xn * lax.rsqrt(jnp.mean(xn * xn, axis=-1, keepdims=True) + EPS) * fg_ref[...]
    is_ctx = pl.program_id(0) < ctx_tiles

    @pl.when(is_ctx)
    def _():
        out_refs[0][...] = xn

    @pl.when(jnp.logical_not(is_ctx))
    def _():
        out_refs[1][...] = xn


def _moe_combine(lay, x, ys, dest, rw, modtab, final_g, l, final):
    tm = lay.tm
    full = lambda: pl.BlockSpec((tm, D_MODEL), lambda i: (i, 0))
    ctx_tiles = lay.nc // tm if final else None
    if final:
        out_shape = (jax.ShapeDtypeStruct((lay.nc, D_MODEL), F32), jax.ShapeDtypeStruct((lay.nl, D_MODEL), F32))
        out_specs = (pl.BlockSpec((tm, D_MODEL), lambda i: (jnp.minimum(i, ctx_tiles - 1), 0)),
                     pl.BlockSpec((tm, D_MODEL), lambda i: (jnp.maximum(i - ctx_tiles, 0), 0)))
    else:
        out_shape, out_specs = jax.ShapeDtypeStruct((lay.n, D_MODEL), F32), full()
    return pl.pallas_call(
        functools.partial(_combine_kernel, tokens=tm, ctx_tiles=ctx_tiles),
        out_shape=out_shape,
        grid=(lay.n // tm,),
        in_specs=[pl.BlockSpec((None, 1, 2 * tm), lambda i: (i, 0, 0), memory_space=pltpu.SMEM),
                  pl.BlockSpec(memory_space=pl.ANY),
                  full(),
                  pl.BlockSpec((tm, LANES), lambda i: (i, 0)),
                  lay.mod_spec(l, 5, tm),
                  pl.BlockSpec((1, D_MODEL), lambda i: (0, 0))],
        out_specs=out_specs,
        scratch_shapes=[pltpu.VMEM((2, tm, D_MODEL), F32), pltpu.SemaphoreType.DMA],
        compiler_params=_params(("arbitrary",)),
        name="moe_combine",
    )(dest.reshape(lay.n // tm, 1, 2 * tm), ys, x, rw, modtab, final_g)


def _kv_kernel(*refs, n_layers):
    k_refs, v_refs = refs[:n_layers], refs[n_layers:2 * n_layers]
    ko_ref, vo_ref = refs[2 * n_layers:]
    layer = pl.program_id(1)
    for srcs, o_ref in ((k_refs, ko_ref), (v_refs, vo_ref)):
        x = srcs[0][...]
        for i in range(1, n_layers):
            x = jnp.where(layer == i, srcs[i][...], x)
        for h in range(H_A):
            o_ref[:, h, :] = x[:, h * DH_A:(h + 1) * DH_A]


def _kv_outputs(lay, projs):
    n_layers = len(projs)
    shape = jax.ShapeDtypeStruct((lay.batch, n_layers, lay.seq, H_A, DH_A), F32)
    src = lambda col: [pl.BlockSpec((lay.seq, A_W), lambda b, i: (b, col))] * n_layers
    out = pl.BlockSpec((None, None, lay.seq, H_A, DH_A), lambda b, i: (b, i, 0, 0, 0))
    return pl.pallas_call(
        functools.partial(_kv_kernel, n_layers=n_layers),
        out_shape=(shape, shape),
        grid=(lay.batch, n_layers),
        in_specs=src(1) + src(2),
        out_specs=(out, out),
        compiler_params=_params(("arbitrary", "arbitrary")),
        name="kv_outputs",
    )(*projs, *projs)


def kernel(x_prompt, x_sample, cache_k_ctx, cache_v_ctx, state_delta, c, c_ctx, w_mod, b_mod, norm_g, final_g,
           w_in, conv_w, a_log, dt_bias, rpb, o_gain, w_out, ffn_gate, ffn_up, ffn_down, pool_w, pool_scale,
           w_router, moe_gate, moe_up, moe_down):
    batch, seq, d = x_prompt.shape
    dec_batch, dec_seq, _ = x_sample.shape
    depth = w_mod.shape[0]
    n_even = w_in.shape[0]
    past = cache_k_ctx.shape[2]
    assert d == D_MODEL and depth % 2 == 0
    lay = _Layout(batch, seq, dec_batch, dec_seq)

    cond = jnp.zeros((lay.mod_rows, d), F32).at[0].set(c_ctx).at[1:1 + dec_batch].set(c)
    mod = _adaln(cond, w_mod, b_mod)
    modtab = mod.reshape(depth, lay.mod_rows, 6, d).transpose(0, 2, 1, 3).reshape(depth * 6 * lay.mod_rows, 1, d)
    normg = norm_g.reshape(depth * 2, 1, d)

    x_parts = (x_prompt.reshape(lay.nc, d), x_sample.reshape(lay.nl, d))
    cache_k = cache_k_ctx.reshape(dec_batch, n_even, past, A_W)
    cache_v = cache_v_ctx.reshape(dec_batch, n_even, past, A_W)
    rows = dec_seq // GRID_W
    kh = min(NA_KH, rows)
    n_main = 3 * A_W + 4 * B_W
    pad128 = lambda a: jnp.pad(a.reshape(1, -1), ((0, 0), (0, LANES - a.size)))
    w_router_p = jnp.pad(w_router, ((0, 0), (0, 0), (0, LANES - N_EXP)))
    w_in16, w_out16 = _to_bf16(w_in), _to_bf16(w_out)
    ffn_gate, ffn_up, ffn_down = _to_bf16(ffn_gate), _to_bf16(ffn_up), _to_bf16(ffn_down)

    projs, s_list = [], []
    for l in range(depth):
        i = l // 2
        if l % 2 == 0:
            w_ab = jnp.pad(w_in[i][:, n_main:], ((0, 0), (0, LANES - 4 * H_B)))
            proj, ab = _even_project(lay, x_parts, normg, modtab, w_in16, w_ab, l)
            bias = _na_bias_table(rpb[i], rows, kh)
            oa_ctx, oa_lat = _attention(lay, proj, cache_k, cache_v, bias, i)
            qd, kd, vd, gates, gates_t = _delta_prep(lay, proj, conv_w, ab, pad128(a_log[i]), pad128(dt_bias[i]), i)
            o_f, o_b, s_fin = _delta_net(lay, qd, kd, vd, gates, gates_t, state_delta[:, i])
            x, h2 = _even_output(lay, oa_ctx, oa_lat, o_f, o_b, proj, o_gain.reshape(n_even, 1, DV), w_out16, x_parts,
                                 normg, modtab, l)
            x = _dense_ffn(lay, h2, x, ffn_gate, ffn_up, ffn_down, modtab, l)
            projs.append(proj)
            s_list.append(s_fin)
        else:
            x, h2, ridx, rw, counts = _pool_and_route(lay, x, normg, modtab, pool_w,
                                                      pool_scale.reshape(-1, 1, d), w_router_p, l)
            dest, pad, g_expert, g_row, g_nsub = _moe_plan(ridx, counts, lay.n)
            xs = _moe_dispatch(h2, dest, pad, _sorted_rows(lay.n))
            ys = _moe_experts(xs, g_expert, g_row, g_nsub, moe_gate, moe_up, moe_down, i)
            x = _moe_combine(lay, x, ys, dest, rw, modtab, final_g.reshape(1, d), l, final=(l == depth - 1))
            x_parts = (x,)
    y_ctx, y_lat = x
    new_k, new_v = _kv_outputs(lay, projs)
    return (y_ctx.reshape(batch, seq, d), y_lat.reshape(dec_batch, dec_seq, d), new_k, new_v,
            jnp.stack(s_list, axis=1))
```

```python
import functools
import math

import numpy as np
import jax
import jax.numpy as jnp
from jax import lax
from jax.experimental import pallas as pl
from jax.experimental.pallas import tpu as pltpu

F32 = jnp.float32
BF16 = jnp.bfloat16

D_MODEL = 1024
GRID_W = 64
DH_A = 64
H_A = 8
A_W = H_A * DH_A
NA_KH = 8
NA_KW = 16
DK = 128
DV = 128
H_B = 4
B_W = H_B * DK
CONV_K = 3
CHUNK = 64
POOL_WINDOWS = (2, 4, 8, 16)
POOL_G = D_MODEL // len(POOL_WINDOWS)
D_FF = 7 * D_MODEL // 2
N_EXP = 8
EPS = 1e-6
NEG_INF = -1e30

LANES = 128
SEG = 256
POOL_HALO = 64
FF_TILE = 512
PROJ_TILE = 1792
MOE_SUB = 256
MOE_GROUP = 8
VMEM_LIMIT = 56 * 2 ** 20


def _params(sem, vmem=VMEM_LIMIT):
    return pltpu.CompilerParams(dimension_semantics=sem, vmem_limit_bytes=vmem)


def _silu(x):
    return x * jax.nn.sigmoid(x)


def _dot(a, b):
    return jnp.dot(a, b, preferred_element_type=F32)


def _dot_nt(a, b):
    return lax.dot_general(a, b, (((1,), (1,)), ((), ())), preferred_element_type=F32)


def _dot_f32x3(a01, x):
    x1 = x.astype(BF16)
    r1 = x - x1.astype(F32)
    x2 = r1.astype(BF16)
    x3 = (r1 - x2.astype(F32)).astype(BF16)
    return _dot(a01, x1) + _dot(a01, x2) + _dot(a01, x3)


def _dot_f32ish(a, b):
    a_hi, b_hi = a.astype(BF16), b.astype(BF16)
    a_lo, b_lo = (a - a_hi.astype(F32)).astype(BF16), (b - b_hi.astype(F32)).astype(BF16)
    return _dot(a_hi, b_hi) + (_dot(a_hi, b_lo) + _dot(a_lo, b_hi))


def _mask01(mask):
    return jnp.where(mask, 1.0, 0.0).astype(BF16)


def _modulate(x, g, shift, scale):
    y = x * lax.rsqrt(jnp.mean(x * x, axis=-1, keepdims=True) + EPS)
    return (y * g) * (1.0 + scale) + shift


def _cast_kernel(w_ref, o_ref):
    o_ref[...] = w_ref[...].astype(BF16)


def _to_bf16(w):
    layers, rows, cols = w.shape
    tr = 256
    assert rows % tr == 0
    spec = pl.BlockSpec((None, tr, cols), lambda l, r: (l, r, 0))
    return pl.pallas_call(
        _cast_kernel,
        out_shape=jax.ShapeDtypeStruct(w.shape, BF16),
        grid=(layers, rows // tr),
        in_specs=[spec],
        out_specs=spec,
        compiler_params=_params(("arbitrary", "arbitrary")),
        name="to_bf16",
    )(w)


def _adaln_kernel(cond_ref, w_ref, b_ref, o_ref):
    s = _silu(cond_ref[...]).astype(BF16)
    o_ref[0] = _dot(s, w_ref[0].astype(BF16)) + b_ref[0]


def _adaln(cond, w_mod, b_mod):
    depth, d, six_d = w_mod.shape
    r = cond.shape[0]
    tn = six_d // 4
    return pl.pallas_call(
        _adaln_kernel,
        out_shape=jax.ShapeDtypeStruct((depth, r, six_d), F32),
        grid=(depth, six_d // tn),
        in_specs=[
            pl.BlockSpec((r, d), lambda l, j: (0, 0)),
            pl.BlockSpec((1, d, tn), lambda l, j: (l, 0, j)),
            pl.BlockSpec((1, 1, tn), lambda l, j: (l, 0, j)),
        ],
        out_specs=pl.BlockSpec((1, r, tn), lambda l, j: (l, 0, j)),
        compiler_params=_params(("arbitrary", "arbitrary")),
        name="adaln",
    )(cond, w_mod, b_mod.reshape(depth, 1, six_d))


class _Layout:
    def __init__(self, batch, seq, dec_batch, dec_seq):
        self.batch, self.seq, self.dec_batch, self.dec_seq = batch, seq, dec_batch, dec_seq
        self.nc = batch * seq
        self.nl = dec_batch * dec_seq
        self.n = self.nc + self.nl
        self.mod_rows = -(-(1 + dec_batch) // 8) * 8
        assert seq % SEG == 0 and dec_seq % SEG == 0 and dec_seq % GRID_W == 0
        self.tm = math.gcd(self.nc, 1024)
        assert self.tm % SEG == 0 and dec_seq % self.tm == 0 and self.nc % dec_seq == 0

    def mod_row(self, first_row):
        return jnp.maximum((first_row - self.nc) // self.dec_seq + 1, 0)

    def stream_specs(self, parts, tm):
        ct = self.nc // tm
        off = ct if len(parts) == 1 else 0
        specs = (pl.BlockSpec((tm, D_MODEL), lambda i, *_: (jnp.minimum(i, ct - 1), 0)),
                 pl.BlockSpec((tm, D_MODEL), lambda i, *_: (off + jnp.maximum(i - ct, 0), 0)))
        return specs, (parts[0], parts[-1])

    def mod_spec(self, l, k, rows_per_step):
        base = (l * 6 + k) * self.mod_rows
        return pl.BlockSpec((1, 1, D_MODEL),
                            lambda i, *_: (base + self.mod_row(i * rows_per_step), 0, 0))


def _stream_tile(xa_ref, xb_ref, ctx_tiles):
    return jnp.where(pl.program_id(0) < ctx_tiles, xa_ref[...], xb_ref[...])


def _row_spec(l, k):
    return pl.BlockSpec((1, 1, D_MODEL), lambda i, *_: (l * 2 + k, 0, 0))


def _proj_kernel(xa_ref, xb_ref, g_ref, sh_ref, sc_ref, w_ref, wab_ref, proj_ref, ab_ref, h_scr, *, ctx_tiles):
    @pl.when(pl.program_id(1) == 0)
    def _():
        h = _modulate(_stream_tile(xa_ref, xb_ref, ctx_tiles), g_ref[0], sh_ref[0], sc_ref[0]).astype(BF16)
        h_scr[...] = h
        ab_ref[...] = _dot(h, wab_ref[...].astype(BF16))

    proj_ref[...] = _dot(h_scr[...], w_ref[...])


def _even_project(lay, x_parts, normg, modtab, w_in, w_ab, l):
    i_even = l // 2
    tm, tn = lay.tm, PROJ_TILE
    ncols = 3 * A_W + 4 * B_W
    x_specs, x_args = lay.stream_specs(x_parts, tm)
    return pl.pallas_call(
        functools.partial(_proj_kernel, ctx_tiles=lay.nc // tm),
        out_shape=(jax.ShapeDtypeStruct((lay.n, ncols), F32),
                   jax.ShapeDtypeStruct((lay.n, LANES), F32)),
        grid=(lay.n // tm, ncols // tn),
        in_specs=[
            *x_specs,
            _row_spec(l, 0),
            lay.mod_spec(l, 0, tm),
            lay.mod_spec(l, 1, tm),
            pl.BlockSpec((None, D_MODEL, tn), lambda i, j: (i_even, 0, j)),
            pl.BlockSpec((D_MODEL, LANES), lambda i, j: (0, 0)),
        ],
        out_specs=(pl.BlockSpec((tm, tn), lambda i, j: (i, j)),
                   pl.BlockSpec((tm, LANES), lambda i, j: (i, 0))),
        scratch_shapes=[pltpu.VMEM((tm, D_MODEL), BF16)],
        compiler_params=_params(("arbitrary", "arbitrary")),
        name="even_project",
    )(*x_args, normg, modtab, modtab, w_in, w_ab)


def _attend_heads(q, keys, values, bias_of):
    scale = DH_A ** -0.5
    per = LANES // DH_A
    lane = lax.broadcasted_iota(jnp.int32, (1, LANES), 1)
    own = [(lane >= s * DH_A) & (lane < (s + 1) * DH_A) for s in range(per)]
    groups = range(A_W // LANES)
    blocks = range(len(keys))
    gsl = lambda g: slice(g * LANES, (g + 1) * LANES)
    k16 = [[k[:, gsl(g)].astype(BF16) for k in keys] for g in groups]
    v16 = [[v[:, gsl(g)].astype(BF16) for v in values] for g in groups]
    heads = [(g, s) for g in groups for s in range(per)]
    qh = [jnp.where(own[s], q[:, gsl(g)], 0.0).astype(BF16) for g, s in heads]
    scores = []
    for h, (g, s) in enumerate(heads):
        row = []
        for i in blocks:
            sc = _dot_nt(qh[h], k16[g][i]) * scale
            b = bias_of(h, i)
            row.append(sc if b is None else sc + b)
        scores.append(row)
    peak = [functools.reduce(jnp.maximum, [jnp.max(sc, axis=-1, keepdims=True) for sc in row]) for row in scores]
    probs = [[jnp.exp(sc - m) for sc in row] for row, m in zip(scores, peak)]
    denom = [functools.reduce(jnp.add, [jnp.sum(p, axis=-1, keepdims=True) for p in row]) for row in probs]
    acc = [functools.reduce(jnp.add, [_dot(p.astype(BF16), v16[g][i]) for i, p in enumerate(row)])
           for row, (g, s) in zip(probs, heads)]
    outs = []
    for g in groups:
        o = jnp.zeros_like(acc[0])
        for s in range(per):
            h = g * per + s
            o = jnp.where(own[s], acc[h] / denom[h], o)
        outs.append(o)
    return jnp.concatenate(outs, axis=1)


def _ctx_attn_kernel(q_ref, k_ref, v_ref, o_ref):
    o_ref[...] = _attend_heads(q_ref[...], [k_ref[...]], [v_ref[...]], lambda h, i: None).astype(o_ref.dtype)


def _na_attn_kernel(q_ref, k_ref, v_ref, kc_ref, vc_ref, bias_ref, o_ref, k16, v16, kc16, vc16, *, rows, kh):
    r = pl.program_id(1)

    @pl.when(r == 0)
    def _():
        k16[...] = k_ref[...].astype(BF16)
        v16[...] = v_ref[...].astype(BF16)
        kc16[...] = kc_ref[...].astype(BF16)
        vc16[...] = vc_ref[...].astype(BF16)

    row_lo = jnp.clip(r - kh // 2, 0, rows - kh)
    start = pl.multiple_of(row_lo * GRID_W, GRID_W)
    kl = k16[pl.ds(start, kh * GRID_W), :]
    vl = v16[pl.ds(start, kh * GRID_W), :]
    o_ref[...] = _attend_heads(q_ref[...], [kl, kc16[...]], [vl, vc16[...]],
                               lambda h, i: bias_ref[h, 0] if i == 0 else None).astype(o_ref.dtype)


def _na_class_of(r, rows, kh, xp):
    return xp.minimum(r, kh // 2) + xp.maximum(r - (rows - kh + kh // 2), 0)


def _na_bias_table(rpb, rows, kh):
    all_rows = np.arange(rows)
    all_dr = np.clip(all_rows - kh // 2, 0, rows - kh)[:, None] + np.arange(kh)[None, :] - all_rows[:, None]
    cls = _na_class_of(all_rows, rows, kh, np)
    r = np.array([int(np.argmax(cls == c)) for c in range(int(cls.max()) + 1)])
    assert (all_dr == all_dr[r][cls]).all()
    row_idx = np.clip(r - kh // 2, 0, rows - kh)[:, None] + np.arange(kh)[None, :]
    dr = row_idx - r[:, None] + (NA_KH - 1)
    qcol = np.arange(GRID_W)
    kcol = np.arange(GRID_W)
    col_lo = np.clip(qcol - NA_KW // 2, 0, GRID_W - NA_KW)
    valid = (kcol[None, :] >= col_lo[:, None]) & (kcol[None, :] < col_lo[:, None] + NA_KW)
    dc = np.clip(kcol[None, :] - qcol[:, None], 1 - NA_KW, NA_KW - 1) + (NA_KW - 1)
    onehot = (dc[None, :, :] == np.arange(2 * NA_KW - 1)[:, None, None]).astype(np.float32)
    picked = jnp.einsum("hrjc,cqk->hrqjk", rpb[:, dr].astype(F32), onehot, precision=lax.Precision.HIGHEST)
    bias = jnp.where(valid[None, None, :, None, :], picked, NEG_INF)
    return bias.reshape(H_A, len(r), GRID_W, kh * GRID_W)


def _attention(lay, proj, cache_k, cache_v, bias, i_even):
    ctx = pl.pallas_call(
        _ctx_attn_kernel,
        out_shape=jax.ShapeDtypeStruct((lay.nc, A_W), BF16),
        grid=(lay.batch,),
        in_specs=[pl.BlockSpec((lay.seq, A_W), lambda b: (b, 0)),
                  pl.BlockSpec((lay.seq, A_W), lambda b: (b, 1)),
                  pl.BlockSpec((lay.seq, A_W), lambda b: (b, 2))],
        out_specs=pl.BlockSpec((lay.seq, A_W), lambda b: (b, 0)),
        compiler_params=_params(("arbitrary",)),
        name="ctx_attention",
    )(proj, proj, proj)

    rows = lay.dec_seq // GRID_W
    kh = min(NA_KH, rows)
    past = cache_k.shape[2]
    q0 = lay.nc // GRID_W
    b0 = lay.nc // lay.dec_seq
    lat = pl.pallas_call(
        functools.partial(_na_attn_kernel, rows=rows, kh=kh),
        out_shape=jax.ShapeDtypeStruct((lay.nl, A_W), BF16),
        grid=(lay.dec_batch, rows),
        in_specs=[pl.BlockSpec((GRID_W, A_W), lambda b, r: (q0 + b * rows + r, 0)),
                  pl.BlockSpec((lay.dec_seq, A_W), lambda b, r: (b0 + b, 1)),
                  pl.BlockSpec((lay.dec_seq, A_W), lambda b, r: (b0 + b, 2)),
                  pl.BlockSpec((None, None, past, A_W), lambda b, r: (b, i_even, 0, 0)),
                  pl.BlockSpec((None, None, past, A_W), lambda b, r: (b, i_even, 0, 0)),
                  pl.BlockSpec((H_A, 1, GRID_W, kh * GRID_W),
                               lambda b, r: (0, _na_class_of(r, rows, kh, jnp), 0, 0))],
        out_specs=pl.BlockSpec((GRID_W, A_W), lambda b, r: (b * rows + r, 0)),
        scratch_shapes=[pltpu.VMEM((lay.dec_seq, A_W), BF16), pltpu.VMEM((lay.dec_seq, A_W), BF16),
                        pltpu.VMEM((past, A_W), BF16), pltpu.VMEM((past, A_W), BF16)],
        compiler_params=_params(("arbitrary", "arbitrary")),
        name="na_attention",
    )(proj, proj, proj, cache_k, cache_v, bias)
    return ctx, lat


def _seq_position(first, nc, seq, dec_seq):
    is_lat = first >= nc
    return jnp.where(is_lat, (first - nc) % dec_seq, first % seq), jnp.where(is_lat, dec_seq, seq)


GATE_GC, GATE_BETA, GATE_GT = 0, 2 * H_B, 4 * H_B


def _gate_table(ab, alog, dtb):
    n = SEG
    lane = lax.broadcasted_iota(jnp.int32, (1, LANES), 1)
    ri = lax.broadcasted_iota(jnp.int32, (n, n), 0)
    ci = lax.broadcasted_iota(jnp.int32, (n, n), 1)
    shift = int(math.log2(CHUNK))
    same = jnp.right_shift(ri, shift) == jnp.right_shift(ci, shift)
    g_all = -jnp.exp(alog) * jax.nn.softplus(ab + dtb)
    prefix = _dot_f32x3(_mask01(same & (ci <= ri)), g_all)
    total = _dot_f32x3(_mask01(same), g_all)
    gc = jnp.where(lane < H_B, prefix, total - prefix + g_all)
    return jnp.where(lane < GATE_BETA, gc,
                     jnp.where(lane < GATE_GT, jax.nn.sigmoid(ab), pltpu.roll(total, GATE_GT, 1)))


def _delta_prep_kernel(xp_ref, xc_ref, xn_ref, w_ref, ab_ref, alog_ref, dtb_ref, q_ref, k_ref, v_ref, g_ref, gt_ref,
                       *, nc, seq, dec_seq):
    table = _gate_table(ab_ref[...], alog_ref[...], dtb_ref[...])
    g_ref[...] = table
    gt_ref[...] = table.T
    off, t_len = _seq_position(pl.program_id(0) * SEG, nc, seq, dec_seq)
    x = xc_ref[...]
    row = lax.broadcasted_iota(jnp.int32, (SEG, 1), 0)
    before = jnp.where(off > 0, xp_ref[7:8, :], 0.0)
    after = jnp.where(off + SEG < t_len, xn_ref[0:1, :], 0.0)
    prev = jnp.where(row == 0, before, pltpu.roll(x, 1, 0))
    nxt = jnp.where(row == SEG - 1, after, pltpu.roll(x, SEG - 1, 0))
    w = w_ref[...]
    y = _silu(prev * w[0:1] + x * w[1:2] + nxt * w[2:3])

    def l2norm(a):
        return a * lax.rsqrt(jnp.sum(a * a, axis=-1, keepdims=True) + EPS)

    heads = lambda part: [y[:, (part * H_B + h) * DK:(part * H_B + h + 1) * DK] for h in range(H_B)]
    q_ref[...] = jnp.concatenate([l2norm(a) * (DK ** -0.5) for a in heads(0)], axis=1)
    k_ref[...] = jnp.concatenate([l2norm(a) for a in heads(1)], axis=1)
    v_ref[...] = y[:, 2 * B_W:]


def _delta_prep(lay, proj, conv_w, ab, alog, dtb, i_even):
    width = 3 * B_W
    cblk = 3 * A_W // width
    assert cblk * width == 3 * A_W
    per = SEG // 8
    n8 = lay.n // 8
    ospec = pl.BlockSpec((SEG, B_W), lambda i: (i, 0))
    shape = jax.ShapeDtypeStruct((lay.n, B_W), F32)
    row = pl.BlockSpec((1, LANES), lambda i: (0, 0))
    return pl.pallas_call(
        functools.partial(_delta_prep_kernel, nc=lay.nc, seq=lay.seq, dec_seq=lay.dec_seq),
        out_shape=(shape, shape, shape,
                   jax.ShapeDtypeStruct((lay.n, LANES), F32), jax.ShapeDtypeStruct((LANES, lay.n), F32)),
        grid=(lay.n // SEG,),
        in_specs=[pl.BlockSpec((8, width), lambda i: (jnp.maximum(i * per - 1, 0), cblk)),
                  pl.BlockSpec((SEG, width), lambda i: (i, cblk)),
                  pl.BlockSpec((8, width), lambda i: (jnp.minimum((i + 1) * per, n8 - 1), cblk)),
                  pl.BlockSpec((None, CONV_K, width), lambda i: (i_even, 0, 0)),
                  pl.BlockSpec((SEG, LANES), lambda i: (i, 0)), row, row],
        out_specs=(ospec, ospec, ospec,
                   pl.BlockSpec((SEG, LANES), lambda i: (i, 0)), pl.BlockSpec((LANES, SEG), lambda i: (0, i))),
        compiler_params=_params(("arbitrary",)),
        name="delta_prep",
    )(proj, proj, proj, conv_w, ab, alog, dtb)


def _delta_masks(d):
    n = SEG
    ri = lax.broadcasted_iota(jnp.int32, (n, n), 0)
    ci = lax.broadcasted_iota(jnp.int32, (n, n), 1)
    shift = int(math.log2(CHUNK))
    same = jnp.right_shift(ri, shift) == jnp.right_shift(ci, shift)
    incl = same & ((ci <= ri) if d == 0 else (ci >= ri))
    strict = same & ((ci < ri) if d == 0 else (ci > ri))
    levels = []
    for level in range(shift):
        bi, bj = jnp.right_shift(ri, level), jnp.right_shift(ci, level)
        siblings = jnp.right_shift(bi, 1) == jnp.right_shift(bj, 1)
        levels.append(siblings & ((bi > bj) if d == 0 else (bi < bj)))
    return incl, strict, jnp.where(ri == ci, 1.0, 0.0), levels


def _delta_chains(chains):
    n = SEG
    n_chunks = n // CHUNK
    each = lambda fn, *lists: [fn(*args) for args in zip(*lists)]
    ds, masks, qs, ks, vs, betas, gcs, gts, gc_rows, gt_rows, states = (list(t) for t in zip(*chains))
    incls, stricts = [m[0] for m in masks], [m[1] for m in masks]
    n_levels = len(masks[0][3])

    decays = each(lambda m, gc, gr: jnp.where(m, jnp.exp(jnp.where(m, gc - gr, 0.0)), 0.0), incls, gcs, gc_rows)
    kbs = each(lambda k, b: k * b, ks, betas)
    k16s = [k.astype(BF16) for k in ks]
    grams = each(lambda kb, k16: _dot_nt(kb.astype(BF16), k16), kbs, k16s)
    lmats = each(lambda m, g, dec: jnp.where(m, g * dec, 0.0), stricts, grams, decays)

    xs = each(lambda m, lm: m[2] - jnp.where(m[3][0], lm, 0.0), masks, lmats)
    for level in range(1, n_levels):
        x16s = [x.astype(BF16) for x in xs]
        cs = each(lambda m, lm: jnp.where(m[3][level], lm, 0.0).astype(BF16), masks, lmats)
        xcs = each(lambda x16, c: _dot(x16, c).astype(BF16), x16s, cs)
        xs = each(lambda x, xc, x16: x - _dot(xc, x16), xs, xcs, x16s)

    egs = [jnp.exp(gc) for gc in gcs]
    rhss = each(lambda v, b, kb, eg: jnp.concatenate([v * b, kb * eg], axis=1).astype(BF16), vs, betas, kbs, egs)
    uws = each(lambda x, rhs: _dot(x.astype(BF16), rhs), xs, rhss)
    qks = each(lambda q, k16: _dot_nt(q.astype(BF16), k16), qs, k16s)
    attns = each(lambda m, qk, dec: jnp.where(m, qk * dec, 0.0).astype(BF16), incls, qks, decays)
    qgs = each(lambda q, eg: (q * eg).astype(BF16), qs, egs)
    kd_ts = each(lambda k, gt_r, gc_r: (k.T * jnp.exp(gt_r - gc_r)).astype(BF16), ks, gt_rows, gc_rows)
    gls = [jnp.exp(gt) for gt in gts]

    v_new = [[None] * n_chunks for _ in chains]
    o_state = [[None] * n_chunks for _ in chains]
    for step in range(n_chunks):
        for i, d in enumerate(ds):
            c = step if d == 0 else n_chunks - 1 - step
            rs = slice(c * CHUNK, (c + 1) * CHUNK)
            s16 = states[i].astype(BF16)
            vn = uws[i][rs, :DV] - _dot(uws[i][rs, DV:].astype(BF16), s16)
            o_state[i][c] = _dot(qgs[i][rs], s16)
            v_new[i][c] = vn
            pieces = [jnp.zeros((c * CHUNK, DV), BF16), vn.astype(BF16),
                      jnp.zeros((n - (c + 1) * CHUNK, DV), BF16)]
            padded = jnp.concatenate([p for p in pieces if p.shape[0]], axis=0)
            states[i] = states[i] * gls[i][c * CHUNK:c * CHUNK + 1, :] + _dot(kd_ts[i], padded)
    outs = each(lambda os, a, vn: jnp.concatenate(os, axis=0) + _dot(a, jnp.concatenate(vn, axis=0).astype(BF16)),
                o_state, attns, v_new)
    return list(zip(outs, states))


def _delta_kernel(plan_ref, *refs):
    dir_refs = (refs[0:5], refs[5:10])
    s0_ref, of_ref, ob_ref, sfin_ref, s_scr = refs[10:]
    g = pl.program_id(0)
    first, last, has_s0 = plan_ref[2, g] == 1, plan_ref[3, g] == 1, plan_ref[4, g] == 1

    @pl.when(first)
    def _():
        s_scr[...] = jnp.where(has_s0, s0_ref[...], 0.0)

    chains = []
    for d, (q_ref, k_ref, v_ref, g_ref, gt_ref) in enumerate(dir_refs):
        masks = _delta_masks(d)
        q, k, v, gates, gates_t = q_ref[...], k_ref[...], v_ref[...], g_ref[...], gt_ref[...]
        for h in range(H_B):
            sl = slice(h * DK, (h + 1) * DK)
            col = d * H_B + h
            pick = lambda base: gates[:, base + col:base + col + 1]
            pick_t = lambda base: gates_t[base + col:base + col + 1, :]
            chains.append((d, masks, q[:, sl], k[:, sl], v[:, sl], pick(GATE_BETA), pick(GATE_GC),
                           pick(GATE_GT), pick_t(GATE_GC), pick_t(GATE_GT), s_scr[d, h]))
    results = _delta_chains(chains)
    for d, o_ref in enumerate((of_ref, ob_ref)):
        o_ref[...] = jnp.concatenate([results[d * H_B + h][0] for h in range(H_B)], axis=1)
        for h in range(H_B):
            s_scr[d, h] = results[d * H_B + h][1]

    @pl.when(last & jnp.logical_not(has_s0))
    def _():
        sfin_ref[...] = s_scr[...]


def _delta_plan(lay):
    rows = []
    for kind, n_seq, t in ((0, lay.batch, lay.seq), (1, lay.dec_batch, lay.dec_seq)):
        nseg = t // SEG
        base = 0 if kind == 0 else lay.nc // SEG
        for b in range(n_seq):
            for s in range(nseg):
                rows.append((base + b * nseg + s, base + b * nseg + nseg - 1 - s, int(s == 0), int(s == nseg - 1),
                             kind, b if kind else 0, b if kind == 0 else lay.batch - 1))
    return jnp.asarray(np.array(rows, np.int32).T)


def _delta_net(lay, q, k, v, gates, gates_t, s0):
    plan = _delta_plan(lay)
    in_specs, args = [], []
    for d in (0, 1):
        in_specs += [pl.BlockSpec((SEG, B_W), lambda g, plan, d=d: (plan[d, g], 0))] * 3
        in_specs += [pl.BlockSpec((SEG, LANES), lambda g, plan, d=d: (plan[d, g], 0)),
                     pl.BlockSpec((LANES, SEG), lambda g, plan, d=d: (0, plan[d, g]))]
        args += [q, k, v, gates, gates_t]
    state = lambda row: pl.BlockSpec((None, 2, H_B, DK, DV), lambda g, plan: (plan[row, g], 0, 0, 0, 0))
    oshape = jax.ShapeDtypeStruct((lay.n, B_W), F32)
    grid_spec = pltpu.PrefetchScalarGridSpec(
        num_scalar_prefetch=1,
        grid=(lay.n // SEG,),
        in_specs=in_specs + [state(5)],
        out_specs=(pl.BlockSpec((SEG, B_W), lambda g, plan: (plan[0, g], 0)),
                   pl.BlockSpec((SEG, B_W), lambda g, plan: (plan[1, g], 0)), state(6)),
        scratch_shapes=[pltpu.VMEM((2, H_B, DK, DV), F32)],
    )
    return pl.pallas_call(
        _delta_kernel,
        out_shape=(oshape, oshape, jax.ShapeDtypeStruct((lay.batch, 2, H_B, DK, DV), F32)),
        grid_spec=grid_spec,
        compiler_params=_params(("arbitrary",)),
        name="delta_net",
    )(plan, *args, s0)


def _even_out_kernel(oac_ref, oal_ref, of_ref, ob_ref, z_ref, gain_ref, w_ref, xa_ref, xb_ref, gt_ref,
                     g2_ref, sh2_ref, sc2_ref, xo_ref, h2_ref, *, ctx_tiles):
    ob = of_ref[...] + ob_ref[...]
    z = z_ref[...]
    parts = [_stream_tile(oac_ref, oal_ref, ctx_tiles)]
    for h in range(H_B):
        sl = slice(h * DV, (h + 1) * DV)
        o_h = ob[:, sl]
        y = o_h * lax.rsqrt(jnp.mean(o_h * o_h, axis=-1, keepdims=True) + EPS) * gain_ref[...] * _silu(z[:, sl])
        parts.append(y.astype(BF16))
    mix = _dot(jnp.concatenate(parts, axis=1), w_ref[...])
    xn = _stream_tile(xa_ref, xb_ref, ctx_tiles) + gt_ref[0] * mix
    xo_ref[...] = xn
    h2_ref[...] = _modulate(xn, g2_ref[0], sh2_ref[0], sc2_ref[0]).astype(BF16)


def _even_output(lay, oa_ctx, oa_lat, o_f, o_b, proj, o_gain, w_out, x_parts, normg, modtab, l):
    i_even = l // 2
    tm = lay.tm
    ctx_tiles = lay.nc // tm
    x_specs, x_args = lay.stream_specs(x_parts, tm)
    zcol = (3 * A_W + 3 * B_W) // B_W
    half = lambda: pl.BlockSpec((tm, B_W), lambda i: (i, 0))
    full = lambda: pl.BlockSpec((tm, D_MODEL), lambda i: (i, 0))
    return pl.pallas_call(
        functools.partial(_even_out_kernel, ctx_tiles=ctx_tiles),
        out_shape=(jax.ShapeDtypeStruct((lay.n, D_MODEL), F32),
                   jax.ShapeDtypeStruct((lay.n, D_MODEL), BF16)),
        grid=(lay.n // tm,),
        in_specs=[pl.BlockSpec((tm, A_W), lambda i: (jnp.minimum(i, ctx_tiles - 1), 0)),
                  pl.BlockSpec((tm, A_W), lambda i: (jnp.maximum(i - ctx_tiles, 0), 0)),
                  half(), half(),
                  pl.BlockSpec((tm, B_W), lambda i: (i, zcol)),
                  pl.BlockSpec((None, 1, DV), lambda i: (i_even, 0, 0)),
                  pl.BlockSpec((None, D_MODEL, D_MODEL), lambda i: (i_even, 0, 0)),
                  *x_specs,
                  lay.mod_spec(l, 2, tm),
                  _row_spec(l, 1), lay.mod_spec(l, 3, tm), lay.mod_spec(l, 4, tm)],
        out_specs=(full(), full()),
        compiler_params=_params(("arbitrary",)),
        name="even_output",
    )(oa_ctx, oa_lat, o_f, o_b, proj, o_gain, w_out, *x_args, modtab, normg, modtab, modtab)


def _ffn_kernel(h_ref, wg_ref, wu_ref, wd_ref, x_ref, gt_ref, o_ref, acc_ref):
    j = pl.program_id(1)

    @pl.when(j == 0)
    def _():
        acc_ref[...] = jnp.zeros_like(acc_ref)

    h = h_ref[...]
    a = (_silu(_dot(h, wg_ref[...])) * _dot(h, wu_ref[...])).astype(BF16)
    acc_ref[...] += _dot(a, wd_ref[...])

    @pl.when(j == pl.num_programs(1) - 1)
    def _():
        o_ref[...] = x_ref[...] + gt_ref[0] * acc_ref[...]


def _dense_ffn(lay, h2, x, ffn_gate, ffn_up, ffn_down, modtab, l):
    i_even = l // 2
    tm, tf = lay.tm, FF_TILE
    return pl.pallas_call(
        _ffn_kernel,
        out_shape=jax.ShapeDtypeStruct((lay.n, D_MODEL), F32),
        grid=(lay.n // tm, D_FF // tf),
        in_specs=[pl.BlockSpec((tm, D_MODEL), lambda i, j: (i, 0)),
                  pl.BlockSpec((None, D_MODEL, tf), lambda i, j: (i_even, 0, j)),
                  pl.BlockSpec((None, D_MODEL, tf), lambda i, j: (i_even, 0, j)),
                  pl.BlockSpec((None, tf, D_MODEL), lambda i, j: (i_even, j, 0)),
                  pl.BlockSpec((tm, D_MODEL), lambda i, j: (i, 0)),
                  lay.mod_spec(l, 5, tm)],
        out_specs=pl.BlockSpec((tm, D_MODEL), lambda i, j: (i, 0)),
        scratch_shapes=[pltpu.VMEM((tm, D_MODEL), F32)],
        compiler_params=_params(("arbitrary", "arbitrary")),
        name="dense_ffn",
    )(h2, ffn_gate, ffn_up, ffn_down, x, modtab)


def _pool_kernel(xp_ref, xc_ref, xn_ref, g1_ref, sh1_ref, sc1_ref, gt1_ref, pw_ref, ps_ref,
                 g2_ref, sh2_ref, sc2_ref, wr_ref, xo_ref, h2_ref, ridx_ref, rw_ref, cnt_ref, count_scr,
                 *, nc, seq, dec_seq):
    i = pl.program_id(0)
    off, t_len = _seq_position(i * SEG, nc, seq, dec_seq)

    g1, sh1, sc1 = g1_ref[0], sh1_ref[0], sc1_ref[0]
    x = xc_ref[...]
    h_cur = _modulate(x, g1, sh1, sc1)
    h_cat = jnp.concatenate([_modulate(xp_ref[...], g1, sh1, sc1), h_cur,
                             _modulate(xn_ref[...], g1, sh1, sc1)], axis=0)

    t = off + lax.broadcasted_iota(jnp.int32, (SEG, 1), 0)
    p = off - POOL_HALO + lax.broadcasted_iota(jnp.int32, (1, SEG + 2 * POOL_HALO), 1)
    mixes = []
    for gi, w in enumerate(POOL_WINDOWS):
        sl = slice(gi * POOL_G, (gi + 1) * POOL_G)
        lo = jnp.maximum(t - w // 2, 0)
        hi = jnp.minimum(t + (w - w // 2), t_len)
        band = _mask01((p >= lo) & (p < hi))
        hg = h_cat[:, sl]
        hg_hi = hg.astype(BF16)
        hg_lo = (hg - hg_hi.astype(F32)).astype(BF16)
        window_sum = _dot(band, hg_hi) + _dot(band, hg_lo)
        y = (window_sum / (hi - lo).astype(F32) - h_cur[:, sl]).astype(BF16)
        mixes.append(_dot(y, pw_ref[gi].astype(BF16)))
    mix = jnp.concatenate(mixes, axis=1) * ps_ref[...]
    xn = x + gt1_ref[0] * mix
    xo_ref[...] = xn
    h2 = _modulate(xn, g2_ref[0], sh2_ref[0], sc2_ref[0])
    h2_ref[...] = h2

    logits = _dot_f32ish(h2, wr_ref[...])
    lane = lax.broadcasted_iota(jnp.int32, logits.shape, 1)
    lane_f = lane.astype(F32)
    lg = jnp.where(lane < N_EXP, logits, -jnp.inf)
    m1 = jnp.max(lg, axis=-1, keepdims=True)
    i1 = jnp.min(jnp.where(lg == m1, lane_f, float(LANES)), axis=-1, keepdims=True)
    lg2 = jnp.where(lane_f == i1, -jnp.inf, lg)
    m2 = jnp.max(lg2, axis=-1, keepdims=True)
    i2 = jnp.min(jnp.where(lg2 == m2, lane_f, float(LANES)), axis=-1, keepdims=True)
    e = jnp.exp(m2 - m1)
    w1 = 1.0 / (1.0 + e)
    w2 = e / (1.0 + e)
    rw_ref[...] = jnp.where(lane == 0, w1, jnp.where(lane == 1, w2, 0.0))

    @pl.when(i == 0)
    def _():
        count_scr[...] = jnp.zeros_like(count_scr)

    hit1, hit2 = lane_f == i1, lane_f == i2
    picks = jnp.where(hit1 | hit2, 1.0, 0.0)
    ri = lax.broadcasted_iota(jnp.int32, (SEG, SEG), 0)
    ci = lax.broadcasted_iota(jnp.int32, (SEG, SEG), 1)
    before = _dot(_mask01(ci < ri), picks.astype(BF16)) + count_scr[...]
    r1 = jnp.sum(jnp.where(hit1, before, 0.0), axis=-1, keepdims=True)
    r2 = jnp.sum(jnp.where(hit2, before, 0.0), axis=-1, keepdims=True)
    count_scr[...] += jnp.sum(picks, axis=0, keepdims=True)
    route = jnp.where(lane == 0, i1, jnp.where(lane == 1, i2, jnp.where(lane == 2, r1, jnp.where(lane == 3, r2, 0.0))))
    ridx_ref[...] = route.astype(jnp.int32)
    cnt_ref[...] = jnp.broadcast_to(count_scr[...], cnt_ref.shape)


def _pool_and_route(lay, x, normg, modtab, pool_w, pool_scale, w_router, l):
    i_odd = l // 2
    nblk = lay.n // SEG
    blk = lambda imap: pl.BlockSpec((SEG, D_MODEL), imap)
    halo = lambda imap: pl.BlockSpec((POOL_HALO, D_MODEL), imap)
    per = SEG // POOL_HALO
    lane_blk = pl.BlockSpec((SEG, LANES), lambda i: (i, 0))
    return pl.pallas_call(
        functools.partial(_pool_kernel, nc=lay.nc, seq=lay.seq, dec_seq=lay.dec_seq),
        out_shape=(jax.ShapeDtypeStruct((lay.n, D_MODEL), F32),
                   jax.ShapeDtypeStruct((lay.n, D_MODEL), F32),
                   jax.ShapeDtypeStruct((lay.n, LANES), jnp.int32),
                   jax.ShapeDtypeStruct((lay.n, LANES), F32),
                   jax.ShapeDtypeStruct((8, LANES), F32)),
        grid=(nblk,),
        in_specs=[halo(lambda i: (jnp.maximum(i * per - 1, 0), 0)),
                  blk(lambda i: (i, 0)),
                  halo(lambda i: (jnp.minimum((i + 1) * per, nblk * per - 1), 0)),
                  _row_spec(l, 0), lay.mod_spec(l, 0, SEG), lay.mod_spec(l, 1, SEG), lay.mod_spec(l, 2, SEG),
                  pl.BlockSpec((None, len(POOL_WINDOWS), POOL_G, POOL_G), lambda i: (i_odd, 0, 0, 0)),
                  pl.BlockSpec((None, 1, D_MODEL), lambda i: (i_odd, 0, 0)),
                  _row_spec(l, 1), lay.mod_spec(l, 3, SEG), lay.mod_spec(l, 4, SEG),
                  pl.BlockSpec((None, D_MODEL, LANES), lambda i: (i_odd, 0, 0))],
        out_specs=(blk(lambda i: (i, 0)), blk(lambda i: (i, 0)), lane_blk, lane_blk,
                   pl.BlockSpec((8, LANES), lambda i: (0, 0))),
        scratch_shapes=[pltpu.VMEM((1, LANES), F32)],
        compiler_params=_params(("arbitrary",)),
        name="pool_route",
    )(x, x, x, normg, modtab, modtab, modtab, pool_w, pool_scale, normg, modtab, modtab, w_router)


def _dispatch_kernel(pad_ref, dest_ref, h_ref, xs_hbm, zero_blk, sem, *, tokens):
    def row_copy(t, k):
        return pltpu.make_async_copy(h_ref.at[pl.ds(t, 1)], xs_hbm.at[pl.ds(dest_ref[0, 2 * t + k], 1)], sem)

    def wait_rows(count):
        one_row = pltpu.make_async_copy(h_ref.at[pl.ds(0, 1)], xs_hbm.at[pl.ds(0, 1)], sem)
        for _ in range(count):
            one_row.wait()

    def start(t, carry):
        row_copy(t, 0).start()
        row_copy(t, 1).start()
        return carry

    def wait(t, carry):
        wait_rows(2)
        return carry

    @pl.when(pl.program_id(0) == 0)
    def _():
        zero_blk[...] = jnp.zeros_like(zero_blk)
        for e in range(N_EXP):
            def zero_start(r, carry, e=e):
                pltpu.make_async_copy(zero_blk.at[pl.ds(0, 1)], xs_hbm.at[pl.ds(pad_ref[0, e] + r, 1)], sem).start()
                return carry

            def zero_wait(r, carry):
                wait_rows(1)
                return carry

            lax.fori_loop(0, pad_ref[1, e], zero_start, 0)
            lax.fori_loop(0, pad_ref[1, e], zero_wait, 0)

        def tail_copy(b):
            rows = pl.ds(pl.multiple_of(pad_ref[0, N_EXP] + b * MOE_SUB, MOE_SUB), MOE_SUB)
            return pltpu.make_async_copy(zero_blk, xs_hbm.at[rows], sem)

        def tail_start(b, carry):
            tail_copy(b).start()
            return carry

        def tail_wait(b, carry):
            tail_copy(b).wait()
            return carry

        lax.fori_loop(0, pad_ref[1, N_EXP], tail_start, 0)
        lax.fori_loop(0, pad_ref[1, N_EXP], tail_wait, 0)

    lax.fori_loop(0, tokens, start, 0, unroll=16)
    lax.fori_loop(0, tokens, wait, 0, unroll=16)


def _moe_dispatch(h2, dest, pad, p_max):
    n = h2.shape[0]
    tokens = math.gcd(n, 1024)
    grid_spec = pltpu.PrefetchScalarGridSpec(
        num_scalar_prefetch=1,
        grid=(n // tokens,),
        in_specs=[pl.BlockSpec((None, 1, 2 * tokens), lambda i, pad: (i, 0, 0), memory_space=pltpu.SMEM),
                  pl.BlockSpec((tokens, D_MODEL), lambda i, pad: (i, 0))],
        out_specs=pl.BlockSpec(memory_space=pl.ANY),
        scratch_shapes=[pltpu.VMEM((MOE_SUB, D_MODEL), F32), pltpu.SemaphoreType.DMA],
    )
    return pl.pallas_call(
        functools.partial(_dispatch_kernel, tokens=tokens),
        out_shape=jax.ShapeDtypeStruct((p_max, D_MODEL), F32),
        grid_spec=grid_spec,
        compiler_params=_params(("arbitrary",)),
        name="moe_dispatch",
    )(pad, dest.reshape(n // tokens, 1, 2 * tokens), h2)


def _moe_kernel(ge_ref, gs_ref, gn_ref, xs_hbm, wg_ref, wu_ref, wd_ref, ys_hbm, xbuf, x16, acc, sem_in, sem_out):
    del ge_ref
    g = pl.program_id(0)
    j = pl.program_id(1)
    last_j = pl.num_programs(1) - 1
    nsub = gn_ref[g]
    row0 = gs_ref[g]

    def in_copy(first_row, s):
        rows = pl.ds(pl.multiple_of(first_row + s * MOE_SUB, MOE_SUB), MOE_SUB)
        return pltpu.make_async_copy(xs_hbm.at[rows], xbuf.at[pl.ds(pl.multiple_of(s * MOE_SUB, MOE_SUB), MOE_SUB)],
                                     sem_in.at[s])

    def out_copy(first, n_rows):
        rows = pl.ds(pl.multiple_of(row0 + first, MOE_SUB), n_rows)
        return pltpu.make_async_copy(acc.at[pl.ds(pl.multiple_of(first, MOE_SUB), n_rows)], ys_hbm.at[rows], sem_out)

    def loop(count, fn):
        def body(s, carry):
            fn(s)
            return carry
        lax.fori_loop(0, count, body, 0)

    @pl.when((g == 0) & (j == 0))
    def _():
        loop(nsub, lambda s: in_copy(row0, s).start())

    nxt = jnp.minimum(g + 1, pl.num_programs(0) - 1)

    @pl.when((j == 1) & (g + 1 < pl.num_programs(0)))
    def _():
        loop(gn_ref[nxt], lambda s: in_copy(gs_ref[nxt], s).start())

    def tile(first_sub, n_sub):
        n_rows = n_sub * MOE_SUB
        first = pl.multiple_of(first_sub * MOE_SUB, MOE_SUB)
        rows = pl.ds(first, n_rows)

        @pl.when(j == 0)
        def _():
            for k in range(n_sub):
                in_copy(row0, first_sub + k).wait()
            x16[rows, :] = xbuf[rows, :].astype(BF16)

        xs = x16[rows, :]
        a = (_silu(_dot(xs, wg_ref[...].astype(BF16))) * _dot(xs, wu_ref[...].astype(BF16))).astype(BF16)
        contrib = _dot(a, wd_ref[...].astype(BF16))

        @pl.when(j == 0)
        def _():
            acc[rows, :] = contrib

        @pl.when(j > 0)
        def _():
            acc[rows, :] += contrib

        @pl.when(j == last_j)
        def _():
            out_copy(first, n_rows).start()

    n_quads = lax.shift_right_logical(nsub, 2)
    has_two = (nsub & 2) == 2
    has_one = (nsub & 1) == 1
    loop(n_quads, lambda p: tile(4 * p, 4))

    @pl.when(has_two)
    def _():
        tile(4 * n_quads, 2)

    @pl.when(has_one)
    def _():
        tile(nsub - 1, 1)

    @pl.when(j == last_j)
    def _():
        loop(n_quads, lambda p: out_copy(4 * p * MOE_SUB, 4 * MOE_SUB).wait())

        @pl.when(has_two)
        def _():
            out_copy(4 * n_quads * MOE_SUB, 2 * MOE_SUB).wait()

        @pl.when(has_one)
        def _():
            out_copy((nsub - 1) * MOE_SUB, MOE_SUB).wait()


def _sorted_rows(n_tok):
    return 2 * n_tok + N_EXP * MOE_SUB


def _moe_plan(ridx, counts, n_tok):
    n_pairs = 2 * n_tok
    group_rows = MOE_SUB * MOE_GROUP
    g_max = -(-n_pairs // group_rows) + N_EXP
    counts = counts[0, :N_EXP].astype(jnp.int32)
    nsub_e = (counts + MOE_SUB - 1) // MOE_SUB
    region = nsub_e * MOE_SUB
    start_e = jnp.cumsum(region) - region
    expert = ridx[:, 0:2]
    is_e = expert[:, :, None] == jnp.arange(N_EXP, dtype=jnp.int32)[None, None, :]
    dest = jnp.sum(jnp.where(is_e, start_e[None, None, :], 0), axis=-1) + ridx[:, 2:4]

    ngrp_e = (nsub_e + MOE_GROUP - 1) // MOE_GROUP
    gend = jnp.cumsum(ngrp_e)
    gstart = gend - ngrp_e
    total = gend[-1]
    gid = jnp.arange(g_max, dtype=jnp.int32)
    ge = jnp.minimum(jnp.searchsorted(gend, gid, side="right").astype(jnp.int32), N_EXP - 1)
    kk = gid - gstart[ge]
    live = gid < total
    last_e = jnp.minimum(jnp.searchsorted(gend, total - 1, side="right").astype(jnp.int32), N_EXP - 1)
    g_expert = jnp.where(live, ge, last_e)
    g_row = jnp.where(live, start_e[ge] + kk * group_rows, 0)
    g_nsub = jnp.where(live, jnp.clip(nsub_e[ge] - kk * MOE_GROUP, 0, MOE_GROUP), 0)
    used = jnp.sum(region)
    p_max = _sorted_rows(n_tok)
    pad = jnp.stack([jnp.append(start_e + counts, used),
                     jnp.append(region - counts, (p_max - used) // MOE_SUB)]).astype(jnp.int32)
    return dest, pad, g_expert.astype(jnp.int32), g_row.astype(jnp.int32), g_nsub.astype(jnp.int32)


def _moe_experts(xs, g_expert, g_row, g_nsub, moe_gate, moe_up, moe_down, i_odd):
    p_max = xs.shape[0]
    g_max = g_expert.shape[0]
    tf = FF_TILE
    n_j = D_FF // tf
    group_rows = MOE_SUB * MOE_GROUP
    tile_of = lambda g, j, gn: jnp.where(gn[g] > 0, j, n_j - 1)
    grid_spec = pltpu.PrefetchScalarGridSpec(
        num_scalar_prefetch=3,
        grid=(g_max, n_j),
        in_specs=[pl.BlockSpec(memory_space=pl.ANY),
                  pl.BlockSpec((None, None, D_MODEL, tf),
                               lambda g, j, ge, gs, gn: (i_odd, ge[g], 0, tile_of(g, j, gn))),
                  pl.BlockSpec((None, None, D_MODEL, tf),
                               lambda g, j, ge, gs, gn: (i_odd, ge[g], 0, tile_of(g, j, gn))),
                  pl.BlockSpec((None, None, tf, D_MODEL),
                               lambda g, j, ge, gs, gn: (i_odd, ge[g], tile_of(g, j, gn), 0))],
        out_specs=pl.BlockSpec(memory_space=pl.ANY),
        scratch_shapes=[pltpu.VMEM((group_rows, D_MODEL), F32),
                        pltpu.VMEM((group_rows, D_MODEL), BF16),
                        pltpu.VMEM((group_rows, D_MODEL), F32),
                        pltpu.SemaphoreType.DMA((MOE_GROUP,)), pltpu.SemaphoreType.DMA],
    )
    return pl.pallas_call(
        _moe_kernel,
        out_shape=jax.ShapeDtypeStruct((p_max, D_MODEL), F32),
        grid_spec=grid_spec,
        input_output_aliases={3: 0},
        compiler_params=_params(("arbitrary", "arbitrary")),
        name="moe_experts",
    )(g_expert, g_row, g_nsub, xs, moe_gate, moe_up, moe_down)


def _combine_kernel(dest_ref, ys_hbm, x_ref, rw_ref, gt_ref, fg_ref, *rest, tokens, ctx_tiles):
    out_refs, (ybuf, sem) = rest[:-2], rest[-2:]
    def row_copy(t, k):
        return pltpu.make_async_copy(ys_hbm.at[pl.ds(dest_ref[0, 2 * t + k], 1)], ybuf.at[k, pl.ds(t, 1)], sem)

    def start(t, carry):
        row_copy(t, 0).start()
        row_copy(t, 1).start()
        return carry

    def wait(t, carry):
        one_row = pltpu.make_async_copy(ys_hbm.at[pl.ds(0, 1)], ybuf.at[0, pl.ds(0, 1)], sem)
        one_row.wait()
        one_row.wait()
        return carry

    lax.fori_loop(0, tokens, start, 0, unroll=16)
    lax.fori_loop(0, tokens, wait, 0, unroll=16)
    rw = rw_ref[...]
    y = rw[:, 0:1] * ybuf[0] + rw[:, 1:2] * ybuf[1]
    xn = x_ref[...] + gt_ref[0] * y
    if ctx_tiles is None:
        out_refs[0][...] = xn
        return
    xn = xn * lax.rsqrt(jnp.mean(xn * xn, axis=-1, keepdims=True) + EPS) * fg_ref[...]
    is_ctx = pl.program_id(0) < ctx_tiles

    @pl.when(is_ctx)
    def _():
        out_refs[0][...] = xn

    @pl.when(jnp.logical_not(is_ctx))
    def _():
        out_refs[1][...] = xn


def _moe_combine(lay, x, ys, dest, rw, modtab, final_g, l, final):
    tm = lay.tm
    full = lambda: pl.BlockSpec((tm, D_MODEL), lambda i: (i, 0))
    ctx_tiles = lay.nc // tm if final else None
    if final:
        out_shape = (jax.ShapeDtypeStruct((lay.nc, D_MODEL), F32), jax.ShapeDtypeStruct((lay.nl, D_MODEL), F32))
        out_specs = (pl.BlockSpec((tm, D_MODEL), lambda i: (jnp.minimum(i, ctx_tiles - 1), 0)),
                     pl.BlockSpec((tm, D_MODEL), lambda i: (jnp.maximum(i - ctx_tiles, 0), 0)))
    else:
        out_shape, out_specs = jax.ShapeDtypeStruct((lay.n, D_MODEL), F32), full()
    return pl.pallas_call(
        functools.partial(_combine_kernel, tokens=tm, ctx_tiles=ctx_tiles),
        out_shape=out_shape,
        grid=(lay.n // tm,),
        in_specs=[pl.BlockSpec((None, 1, 2 * tm), lambda i: (i, 0, 0), memory_space=pltpu.SMEM),
                  pl.BlockSpec(memory_space=pl.ANY),
                  full(),
                  pl.BlockSpec((tm, LANES), lambda i: (i, 0)),
                  lay.mod_spec(l, 5, tm),
                  pl.BlockSpec((1, D_MODEL), lambda i: (0, 0))],
        out_specs=out_specs,
        scratch_shapes=[pltpu.VMEM((2, tm, D_MODEL), F32), pltpu.SemaphoreType.DMA],
        compiler_params=_params(("arbitrary",)),
        name="moe_combine",
    )(dest.reshape(lay.n // tm, 1, 2 * tm), ys, x, rw, modtab, final_g)


def _kv_kernel(*refs, n_layers):
    k_refs, v_refs = refs[:n_layers], refs[n_layers:2 * n_layers]
    ko_ref, vo_ref = refs[2 * n_layers:]
    layer = pl.program_id(1)
    for srcs, o_ref in ((k_refs, ko_ref), (v_refs, vo_ref)):
        x = srcs[0][...]
        for i in range(1, n_layers):
            x = jnp.where(layer == i, srcs[i][...], x)
        for h in range(H_A):
            o_ref[:, h, :] = x[:, h * DH_A:(h + 1) * DH_A]


def _kv_outputs(lay, projs):
    n_layers = len(projs)
    shape = jax.ShapeDtypeStruct((lay.batch, n_layers, lay.seq, H_A, DH_A), F32)
    src = lambda col: [pl.BlockSpec((lay.seq, A_W), lambda b, i: (b, col))] * n_layers
    out = pl.BlockSpec((None, None, lay.seq, H_A, DH_A), lambda b, i: (b, i, 0, 0, 0))
    return pl.pallas_call(
        functools.partial(_kv_kernel, n_layers=n_layers),
        out_shape=(shape, shape),
        grid=(lay.batch, n_layers),
        in_specs=src(1) + src(2),
        out_specs=(out, out),
        compiler_params=_params(("arbitrary", "arbitrary")),
        name="kv_outputs",
    )(*projs, *projs)


def kernel(x_prompt, x_sample, cache_k_ctx, cache_v_ctx, state_delta, c, c_ctx, w_mod, b_mod, norm_g, final_g,
           w_in, conv_w, a_log, dt_bias, rpb, o_gain, w_out, ffn_gate, ffn_up, ffn_down, pool_w, pool_scale,
           w_router, moe_gate, moe_up, moe_down):
    batch, seq, d = x_prompt.shape
    dec_batch, dec_seq, _ = x_sample.shape
    depth = w_mod.shape[0]
    n_even = w_in.shape[0]
    past = cache_k_ctx.shape[2]
    assert d == D_MODEL and depth % 2 == 0
    lay = _Layout(batch, seq, dec_batch, dec_seq)

    cond = jnp.zeros((lay.mod_rows, d), F32).at[0].set(c_ctx).at[1:1 + dec_batch].set(c)
    mod = _adaln(cond, w_mod, b_mod)
    modtab = mod.reshape(depth, lay.mod_rows, 6, d).transpose(0, 2, 1, 3).reshape(depth * 6 * lay.mod_rows, 1, d)
    normg = norm_g.reshape(depth * 2, 1, d)

    x_parts = (x_prompt.reshape(lay.nc, d), x_sample.reshape(lay.nl, d))
    cache_k = cache_k_ctx.reshape(dec_batch, n_even, past, A_W)
    cache_v = cache_v_ctx.reshape(dec_batch, n_even, past, A_W)
    rows = dec_seq // GRID_W
    kh = min(NA_KH, rows)
    n_main = 3 * A_W + 4 * B_W
    pad128 = lambda a: jnp.pad(a.reshape(1, -1), ((0, 0), (0, LANES - a.size)))
    w_router_p = jnp.pad(w_router, ((0, 0), (0, 0), (0, LANES - N_EXP)))
    w_in16, w_out16 = _to_bf16(w_in), _to_bf16(w_out)
    ffn_gate, ffn_up, ffn_down = _to_bf16(ffn_gate), _to_bf16(ffn_up), _to_bf16(ffn_down)

    projs, s_list = [], []
    for l in range(depth):
        i = l // 2
        if l % 2 == 0:
            w_ab = jnp.pad(w_in[i][:, n_main:], ((0, 0), (0, LANES - 4 * H_B)))
            proj, ab = _even_project(lay, x_parts, normg, modtab, w_in16, w_ab, l)
            bias = _na_bias_table(rpb[i], rows, kh)
            oa_ctx, oa_lat = _attention(lay, proj, cache_k, cache_v, bias, i)
            qd, kd, vd, gates, gates_t = _delta_prep(lay, proj, conv_w, ab, pad128(a_log[i]), pad128(dt_bias[i]), i)
            o_f, o_b, s_fin = _delta_net(lay, qd, kd, vd, gates, gates_t, state_delta[:, i])
            x, h2 = _even_output(lay, oa_ctx, oa_lat, o_f, o_b, proj, o_gain.reshape(n_even, 1, DV), w_out16, x_parts,
                                 normg, modtab, l)
            x = _dense_ffn(lay, h2, x, ffn_gate, ffn_up, ffn_down, modtab, l)
            projs.append(proj)
            s_list.append(s_fin)
        else:
            x, h2, ridx, rw, counts = _pool_and_route(lay, x, normg, modtab, pool_w,
                                                      pool_scale.reshape(-1, 1, d), w_router_p, l)
            dest, pad, g_expert, g_row, g_nsub = _moe_plan(ridx, counts, lay.n)
            xs = _moe_dispatch(h2, dest, pad, _sorted_rows(lay.n))
            ys = _moe_experts(xs, g_expert, g_row, g_nsub, moe_gate, moe_up, moe_down, i)
            x = _moe_combine(lay, x, ys, dest, rw, modtab, final_g.reshape(1, d), l, final=(l == depth - 1))
            x_parts = (x,)
    y_ctx, y_lat = x
    new_k, new_v = _kv_outputs(lay, projs)
    return (y_ctx.reshape(batch, seq, d), y_lat.reshape(dec_batch, dec_seq, d), new_k, new_v,
            jnp.stack(s_list, axis=1))
```
